```python
import math
import jax
import jax.numpy as jnp
from jax import lax
import numpy as np

D_MODEL = 1024
BATCH = 32
SEQ = 2048
DEPTH = 1

N_MEM = 256
D_FF = 2816
NORM_EPS = 1e-6
NEG_INF = -1e30
FORCE_SCORE = 1e4

NSA_HEADS = 8
NSA_GROUPS = 2
NSA_HPG = NSA_HEADS // NSA_GROUPS
NSA_HEAD_DIM = 64
CMP_BLOCK = 32
CMP_STRIDE = 16
CMP_HIDDEN = 256
SEL_BLOCK = 64
SEL_TOPK = 16
WINDOW = 512
WIN_QBLOCK = 128
SEL_QCHUNK = 16

REL_BUCKETS = 32
REL_MAX_DIST = 128

DN_HEADS = 4
DN_HEAD_DIM = 128
DN_CONV = 4
DN_CHUNK = 64

X_HEADS = 4
X_HEAD_DIM = 128

N_BRANCH = 3
NSA_Q = NSA_HEADS * NSA_HEAD_DIM
NSA_KV = NSA_GROUPS * NSA_HEAD_DIM
DN_W = DN_HEADS * DN_HEAD_DIM
X_W = X_HEADS * X_HEAD_DIM
IN_SPLITS = (NSA_Q, 6 * NSA_KV, 3 * NSA_HEADS, 3 * DN_W, DN_HEADS, DN_HEADS, DN_W, X_W)
IN_WIDTH = NSA_Q + 6 * NSA_KV + 3 * NSA_HEADS + 3 * DN_W + 2 * DN_HEADS + DN_W + X_W

kernel_name = "hybrid_nsa_deltanet_memory_macaron_block"


def rms_norm(x, gain):
    xf = x.astype(jnp.float32)
    y = xf * lax.rsqrt(jnp.mean(xf * xf, axis=-1, keepdims=True) + NORM_EPS)
    return (y * gain.astype(jnp.float32)).astype(x.dtype)


def l2_norm(x):
    xf = x.astype(jnp.float32)
    return xf * lax.rsqrt(jnp.sum(xf * xf, axis=-1, keepdims=True) + NORM_EPS)


def swiglu(x, w_gate, w_up, w_down):
    return (jax.nn.silu(x @ w_gate) * (x @ w_up)) @ w_down


def split_cols(a, sizes):
    out, start = [], 0
    for n in sizes:
        out.append(a[..., start:start + n])
        start += n
    return out


def rel_bucket(dist):
    exact = REL_BUCKETS // 2
    d = jnp.maximum(dist, 1).astype(jnp.float32)
    log_bucket = exact + (jnp.log(d / exact) / math.log(REL_MAX_DIST / exact)
                          * (REL_BUCKETS - exact)).astype(jnp.int32)
    return jnp.where(dist < exact, dist, jnp.minimum(log_bucket, REL_BUCKETS - 1))


def masked_softmax(logits, mask):
    return jax.nn.softmax(jnp.where(mask, logits.astype(jnp.float32), NEG_INF), axis=-1)


def compress_blocks(kv, pos_emb, w1, w2):
    b, s, g, dh = kv.shape
    r = CMP_BLOCK // CMP_STRIDE
    n_chunk = s // CMP_STRIDE
    n_cmp = n_chunk - r + 1
    ch = kv.reshape(b, n_chunk, CMP_STRIDE, g, dh)
    blocks = jnp.concatenate([ch[:, j:j + n_cmp] for j in range(r)], axis=2)
    blocks = blocks + pos_emb[:, None, :]
    flat = blocks.transpose(0, 1, 3, 2, 4).reshape(b, n_cmp, g, CMP_BLOCK * dh)
    return jax.nn.silu(flat @ w1) @ w2


def nsa_compressed(q, k_c, v_c, rel_bias):
    seq, n_cmp = q.shape[1], k_c.shape[1]
    t = jnp.arange(seq)
    block_end = jnp.arange(n_cmp) * CMP_STRIDE + CMP_BLOCK - 1
    dist = t[:, None] - block_end[None, :]
    bias = rel_bias[rel_bucket(jnp.maximum(dist, 0))]
    bias = bias.reshape(seq, n_cmp, NSA_GROUPS, NSA_HPG).transpose(2, 3, 0, 1)
    logits = jnp.einsum('bsgjd,bcgd->bgjsc', q, k_c) * NSA_HEAD_DIM ** -0.5 + bias
    has_block = (t >= CMP_BLOCK - 1).astype(jnp.float32)[:, None]
    p = masked_softmax(logits, dist >= 0) * has_block
    out = jnp.einsum('bgjsc,bcgd->bsgjd', p.astype(v_c.dtype), v_c)
    return out, p


def select_blocks(p_cmp, seq):
    n_cmp = p_cmp.shape[-1]
    n_blk = seq // SEL_BLOCK
    n_sel = min(SEL_TOPK, n_blk)
    c0 = jnp.arange(n_cmp) * CMP_STRIDE
    s0 = jnp.arange(n_blk) * SEL_BLOCK
    overlap = jnp.maximum(jnp.minimum(c0[:, None] + CMP_BLOCK, s0[None, :] + SEL_BLOCK)
                          - jnp.maximum(c0[:, None], s0[None, :]), 0)
    to_sel = overlap.astype(jnp.float32) / CMP_BLOCK
    importance = jnp.einsum('bgjsc,cn->bgsn', p_cmp, to_sel)
    cur = jnp.arange(seq) // SEL_BLOCK
    blk = jnp.arange(n_blk)
    forced = (blk[None] == 0) | (blk[None] == cur[:, None]) | (blk[None] == cur[:, None] - 1)
    causal = blk[None] <= cur[:, None]
    score = jnp.where(forced, FORCE_SCORE, jnp.where(causal, importance, -FORCE_SCORE))
    vals, idx = lax.top_k(score, n_sel)
    return idx, vals > -1.0


def nsa_selected(q, k_s, v_s, idx, ok, rel_bias):
    b, s, g, j, dh = q.shape
    n_blk = s // SEL_BLOCK
    n_sel = idx.shape[-1]
    kb = k_s.reshape(b, n_blk, SEL_BLOCK, g, dh).transpose(0, 3, 1, 2, 4)
    vb = v_s.reshape(b, n_blk, SEL_BLOCK, g, dh).transpose(0, 3, 1, 2, 4)
    table_g = rel_bias.reshape(REL_BUCKETS, NSA_GROUPS, NSA_HPG)
    offs = jnp.arange(SEL_BLOCK)

    def one(qg, kg, vg, ix, okg, t, tab):
        ks = kg[ix]
        vs = vg[ix]
        dist = t[:, None, None] - (ix[..., None] * SEL_BLOCK + offs)
        mask = okg[..., None] & (dist >= 0)
        bias = tab[rel_bucket(jnp.maximum(dist, 0))].transpose(0, 3, 1, 2)
        logits = jnp.einsum('qjd,qknd->qjkn', qg, ks) * NSA_HEAD_DIM ** -0.5 + bias
        nq = logits.shape[0]
        p = masked_softmax(logits.reshape(nq, j, n_sel * SEL_BLOCK),
                           mask.reshape(nq, 1, n_sel * SEL_BLOCK)).reshape(nq, j, n_sel, SEL_BLOCK)
        return jnp.einsum('qjkn,qknd->qjd', p.astype(vs.dtype), vs)

    per_g = jax.vmap(one, in_axes=(1, 0, 0, 0, 0, None, 1), out_axes=1)
    per_bg = jax.vmap(per_g, in_axes=(0, 0, 0, 0, 0, None, None), out_axes=0)
    n_chunk = s // SEL_QCHUNK
    qs = q.reshape(b, n_chunk, SEL_QCHUNK, g, j, dh).transpose(1, 0, 2, 3, 4, 5)
    ixs = idx.reshape(b, g, n_chunk, SEL_QCHUNK, n_sel).transpose(2, 0, 1, 3, 4)
    oks = ok.reshape(b, g, n_chunk, SEL_QCHUNK, n_sel).transpose(2, 0, 1, 3, 4)
    ts = jnp.arange(s).reshape(n_chunk, SEL_QCHUNK)

    def chunk(args):
        qc, ixc, okc, tc = args
        return per_bg(qc, kb, vb, ixc, okc, tc, table_g)

    out = lax.map(chunk, (qs, ixs, oks, ts))
    return out.transpose(1, 0, 2, 3, 4, 5).reshape(b, s, g, j, dh)


def nsa_window(q, k_w, v_w, rel_bias):
    b, s, g, j, dh = q.shape
    n_qb = s // WIN_QBLOCK
    span = WINDOW + WIN_QBLOCK
    pad = ((0, 0), (WINDOW, 0), (0, 0), (0, 0))
    kp = jnp.pad(k_w, pad)
    vp = jnp.pad(v_w, pad)
    qi = jnp.arange(WIN_QBLOCK)
    kj = jnp.arange(span)
    rel = qi[:, None] + WINDOW - kj[None, :]
    bias = rel_bias[rel_bucket(jnp.maximum(rel, 0))]
    bias = bias.reshape(WIN_QBLOCK, span, g, j).transpose(2, 3, 0, 1)
    band = (rel >= 0) & (rel < WINDOW)

    def block(i):
        start = i * WIN_QBLOCK
        qb = lax.dynamic_slice_in_dim(q, start, WIN_QBLOCK, axis=1)
        kb = lax.dynamic_slice_in_dim(kp, start, span, axis=1)
        vb = lax.dynamic_slice_in_dim(vp, start, span, axis=1)
        mask = band & ((start - WINDOW + kj) >= 0)[None, :]
        logits = jnp.einsum('bqgjd,bkgd->bgjqk', qb, kb) * NSA_HEAD_DIM ** -0.5 + bias
        p = masked_softmax(logits, mask)
        return jnp.einsum('bgjqk,bkgd->bqgjd', p.astype(vb.dtype), vb)

    out = lax.map(block, jnp.arange(n_qb))
    return out.transpose(1, 0, 2, 3, 4, 5).reshape(b, s, g, j, dh)


def causal_depthwise_conv(x, w):
    return lax.conv_general_dilated(
        x, w[:, None, :].astype(x.dtype), window_strides=(1,), padding=[(w.shape[0] - 1, 0)],
        dimension_numbers=('NWC', 'WIO', 'NWC'), feature_group_count=x.shape[-1])


def gated_delta_rule(q, k, v, log_decay, beta):
    f32 = jnp.float32
    b, s, h, dk = q.shape
    dv = v.shape[-1]
    c = DN_CHUNK
    n = s // c

    def chunks(a):
        a = a.astype(f32).reshape((b, n, c, h) + a.shape[3:])
        return jnp.moveaxis(a, 3, 1)

    qc = chunks(q) * dk ** -0.5
    kc, vc = chunks(k), chunks(v)
    gc = jnp.cumsum(chunks(log_decay), axis=-1)
    bc = chunks(beta)
    causal = jnp.tril(jnp.ones((c, c), bool))
    strict = jnp.tril(jnp.ones((c, c), bool), -1)
    decay = jnp.exp(jnp.where(causal, gc[..., :, None] - gc[..., None, :], -jnp.inf))
    kk = jnp.einsum('bhnid,bhnjd->bhnij', kc, kc)
    a_mat = jnp.where(strict, bc[..., :, None] * kk * decay, 0.0) + jnp.eye(c, dtype=f32)
    rhs = jnp.concatenate([vc * bc[..., None], kc * (bc * jnp.exp(gc))[..., None]], axis=-1)
    sol = lax.linalg.triangular_solve(a_mat, rhs, left_side=True, lower=True, unit_diagonal=True)
    u, w = sol[..., :dv], sol[..., dv:]
    qk = jnp.einsum('bhnid,bhnjd->bhnij', qc, kc) * decay
    q_dec = qc * jnp.exp(gc)[..., None]
    k_dec = kc * jnp.exp(gc[..., -1:] - gc)[..., None]
    g_last = jnp.exp(gc[..., -1])

    def step(state, inp):
        qk_n, qd_n, kd_n, u_n, w_n, gl_n = inp
        v_new = u_n - jnp.einsum('bhck,bhkv->bhcv', w_n, state)
        o = jnp.einsum('bhck,bhkv->bhcv', qd_n, state) + jnp.einsum('bhcm,bhmv->bhcv', qk_n, v_new)
        state = state * gl_n[..., None, None] + jnp.einsum('bhck,bhcv->bhkv', kd_n, v_new)
        return state, o

    xs = tuple(jnp.moveaxis(a, 2, 0) for a in (qk, q_dec, k_dec, u, w, g_last))
    _, o = lax.scan(step, jnp.zeros((b, h, dk, dv), f32), xs)
    return o.transpose(1, 0, 3, 2, 4).reshape(b, s, h, dv)


def memory_cross_attention(q, mem_n, w_mem_kv):
    b, m, _ = mem_n.shape
    kv = (mem_n @ w_mem_kv).reshape(b, m, 2, X_HEADS, X_HEAD_DIM)
    k, v = kv[:, :, 0], kv[:, :, 1]
    logits = jnp.einsum('bshd,bmhd->bhsm', q, k) * X_HEAD_DIM ** -0.5
    p = jax.nn.softmax(logits.astype(jnp.float32), axis=-1).astype(v.dtype)
    return jnp.einsum('bhsm,bmhd->bshd', p, v)


def setup_inputs(seed: int = 0) -> dict:
    key = jax.random.key(seed)
    keys = iter(jax.random.split(key, 48))
    f32 = jnp.float32
    L, D = DEPTH, D_MODEL

    def normal(shape, scale):
        return jax.random.normal(next(keys), shape, f32) * scale

    def gain(width):
        return 1.0 + 0.02 * jax.random.normal(next(keys), (L, width), f32)

    dt = jnp.exp(jax.random.uniform(next(keys), (L, DN_HEADS), f32, math.log(1e-3), math.log(1e-1)))
    a_log = jnp.log(jax.random.uniform(next(keys), (L, DN_HEADS), f32, 1.0, 16.0))
    return {
        "x": normal((BATCH, SEQ, D), 1.0),
        "mem": normal((BATCH, N_MEM, D), 1.0),
        "ffn1_pre_norm": gain(D),
        "ffn1_w_gate": normal((L, D, D_FF), D ** -0.5),
        "ffn1_w_up": normal((L, D, D_FF), D ** -0.5),
        "ffn1_w_down": normal((L, D_FF, D), D_FF ** -0.5),
        "ffn1_post_norm": gain(D),
        "mix_pre_norm": gain(D),
        "w_in": normal((L, D, IN_WIDTH), D ** -0.5),
        "cmp_pos_k": normal((L, CMP_BLOCK, NSA_HEAD_DIM), 0.1),
        "cmp_pos_v": normal((L, CMP_BLOCK, NSA_HEAD_DIM), 0.1),
        "cmp_k_w1": normal((L, CMP_BLOCK * NSA_HEAD_DIM, CMP_HIDDEN), (CMP_BLOCK * NSA_HEAD_DIM) ** -0.5),
        "cmp_k_w2": normal((L, CMP_HIDDEN, NSA_HEAD_DIM), CMP_HIDDEN ** -0.5),
        "cmp_v_w1": normal((L, CMP_BLOCK * NSA_HEAD_DIM, CMP_HIDDEN), (CMP_BLOCK * NSA_HEAD_DIM) ** -0.5),
        "cmp_v_w2": normal((L, CMP_HIDDEN, NSA_HEAD_DIM), CMP_HIDDEN ** -0.5),
        "rel_bias": normal((REL_BUCKETS, NSA_HEADS), 0.5),
        "dn_conv_w": normal((L, DN_CONV, 3 * DN_W), DN_CONV ** -0.5),
        "dn_a_log": a_log,
        "dn_dt_bias": dt + jnp.log(-jnp.expm1(-dt)),
        "dn_out_norm": gain(DN_HEAD_DIM),
        "mem_norm": gain(D),
        "w_mem_kv": normal((L, D, 2 * X_W), D ** -0.5),
        "w_branch_nsa": normal((L, NSA_Q, D), NSA_Q ** -0.5),
        "w_branch_dn": normal((L, DN_W, D), DN_W ** -0.5),
        "w_branch_mem": normal((L, X_W, D), X_W ** -0.5),
        "w_branch_gate": normal((L, D, N_BRANCH * D), D ** -0.5),
        "w_out": normal((L, D, D), D ** -0.5),
        "mix_post_norm": gain(D),
        "ffn2_pre_norm": gain(D),
        "ffn2_w_gate": normal((L, D, D_FF), D ** -0.5),
        "ffn2_w_up": normal((L, D, D_FF), D ** -0.5),
        "ffn2_w_down": normal((L, D_FF, D), D_FF ** -0.5),
        "ffn2_post_norm": gain(D),
    }


def reference(x, mem, ffn1_pre_norm, ffn1_w_gate, ffn1_w_up, ffn1_w_down, ffn1_post_norm,
              mix_pre_norm, w_in, cmp_pos_k, cmp_pos_v, cmp_k_w1, cmp_k_w2, cmp_v_w1, cmp_v_w2,
              rel_bias, dn_conv_w, dn_a_log, dn_dt_bias, dn_out_norm, mem_norm, w_mem_kv,
              w_branch_nsa, w_branch_dn, w_branch_mem, w_branch_gate, w_out, mix_post_norm,
              ffn2_pre_norm, ffn2_w_gate, ffn2_w_up, ffn2_w_down, ffn2_post_norm):
    b, s, d = x.shape
    f32 = jnp.float32
    for l in range(DEPTH):
        h = swiglu(rms_norm(x, ffn1_pre_norm[l]), ffn1_w_gate[l], ffn1_w_up[l], ffn1_w_down[l])
        x = x + 0.5 * rms_norm(h, ffn1_post_norm[l])

        u = rms_norm(x, mix_pre_norm[l])
        nsa_q, nsa_kv, nsa_g, dn_qkv, dn_a, dn_b, dn_z, mem_q = split_cols(u @ w_in[l], IN_SPLITS)

        q = nsa_q.reshape(b, s, NSA_GROUPS, NSA_HPG, NSA_HEAD_DIM)
        kv = nsa_kv.reshape(b, s, 6, NSA_GROUPS, NSA_HEAD_DIM)
        k_cmp = compress_blocks(kv[:, :, 0], cmp_pos_k[l], cmp_k_w1[l], cmp_k_w2[l])
        v_cmp = compress_blocks(kv[:, :, 1], cmp_pos_v[l], cmp_v_w1[l], cmp_v_w2[l])
        o_cmp, p_cmp = nsa_compressed(q, k_cmp, v_cmp, rel_bias)
        sel_idx, sel_ok = select_blocks(p_cmp, s)
        o_sel = nsa_selected(q, kv[:, :, 2], kv[:, :, 3], sel_idx, sel_ok, rel_bias)
        o_win = nsa_window(q, kv[:, :, 4], kv[:, :, 5], rel_bias)
        g = jax.nn.sigmoid(nsa_g).reshape(b, s, 3, NSA_GROUPS, NSA_HPG, 1)
        o_nsa = (g[:, :, 0] * o_cmp + g[:, :, 1] * o_sel + g[:, :, 2] * o_win).reshape(b, s, NSA_Q)

        qkv = jax.nn.silu(causal_depthwise_conv(dn_qkv, dn_conv_w[l]))
        dq, dk, dv = split_cols(qkv, (DN_W, DN_W, DN_W))
        dq = l2_norm(dq.reshape(b, s, DN_HEADS, DN_HEAD_DIM))
        dk = l2_norm(dk.reshape(b, s, DN_HEADS, DN_HEAD_DIM))
        dv = dv.reshape(b, s, DN_HEADS, DN_HEAD_DIM)
        log_decay = -jnp.exp(dn_a_log[l].astype(f32)) * jax.nn.softplus((dn_a + dn_dt_bias[l]).astype(f32))
        beta = jax.nn.sigmoid(dn_b.astype(f32))
        o_dn = gated_delta_rule(dq, dk, dv, log_decay, beta).astype(x.dtype)
        o_dn = rms_norm(o_dn, dn_out_norm[l]) * jax.nn.silu(dn_z.reshape(b, s, DN_HEADS, DN_HEAD_DIM))
        o_dn = o_dn.reshape(b, s, DN_W)

        o_mem = memory_cross_attention(mem_q.reshape(b, s, X_HEADS, X_HEAD_DIM),
                                       rms_norm(mem, mem_norm[l]), w_mem_kv[l]).reshape(b, s, X_W)

        gates = jax.nn.sigmoid(u @ w_branch_gate[l]).reshape(b, s, N_BRANCH, d)
        merged = (gates[:, :, 0] * (o_nsa @ w_branch_nsa[l])
                  + gates[:, :, 1] * (o_dn @ w_branch_dn[l])
                  + gates[:, :, 2] * (o_mem @ w_branch_mem[l]))
        x = x + rms_norm(merged @ w_out[l], mix_post_norm[l])

        h = swiglu(rms_norm(x, ffn2_pre_norm[l]), ffn2_w_gate[l], ffn2_w_up[l], ffn2_w_down[l])
        x = x + 0.5 * rms_norm(h, ffn2_post_norm[l])
    return x
```

```python
import math

import jax
import jax.numpy as jnp
import numpy as np
from jax import lax
from jax.experimental import pallas as pl
from jax.experimental.pallas import tpu as pltpu

F32 = jnp.float32
BF16 = jnp.bfloat16

D_MODEL = 1024
D_FF = 2816
NORM_EPS = 1e-6
NEG_INF = -1e30
FORCE_SCORE = 1e4

NSA_HEADS = 8
NSA_GROUPS = 2
NSA_HPG = NSA_HEADS // NSA_GROUPS
NSA_HEAD_DIM = 64
CMP_BLOCK = 32
CMP_STRIDE = 16
CMP_HIDDEN = 256
SEL_BLOCK = 64
SEL_SHIFT = 6
SEL_TOPK = 16
WINDOW = 512
REL_BUCKETS = 32
REL_MAX_DIST = 128

DN_HEADS = 4
DN_HEAD_DIM = 128
DN_CONV = 4
X_HEADS = 4
X_HEAD_DIM = 128
N_BRANCH = 3

NSA_Q = NSA_HEADS * NSA_HEAD_DIM
NSA_KV = NSA_GROUPS * NSA_HEAD_DIM
DN_W = DN_HEADS * DN_HEAD_DIM
X_W = X_HEADS * X_HEAD_DIM

LANES = 128
VMEM_LIMIT = 56 * 1024 * 1024

TM = 512
FF_CHUNK = 256
TQ = 256
DN_CHUNK = 128
DN_LEVELS = 7
CONV_HALO = 16
TQ_MEM = 512

SM_GATE = 0
SM_A = N_BRANCH * NSA_HEADS
SM_B = SM_A + DN_HEADS


def _dot(a, b):
    return jnp.dot(a, b, preferred_element_type=F32)


def _dot_nt(a, b):
    return lax.dot_general(a, b, (((1,), (1,)), ((), ())), preferred_element_type=F32)


def _rms(x, gain):
    return x * lax.rsqrt(jnp.mean(x * x, axis=-1, keepdims=True) + NORM_EPS) * gain


def _silu(x):
    return x * jax.nn.sigmoid(x)


def _const_spec(shape):
    zeros = (0,) * len(shape)
    return pl.BlockSpec(shape, lambda *_: zeros, pipeline_mode=pl.Buffered(1))


def _params(*sem):
    return pltpu.CompilerParams(dimension_semantics=sem, vmem_limit_bytes=VMEM_LIMIT)


def _ffn_kernel(x_ref, pre_ref, wg_ref, wu_ref, wd_ref, post_ref, o_ref):
    x = x_ref[...]
    xn = _rms(x, pre_ref[...]).astype(BF16)
    acc = jnp.zeros(x.shape, F32)
    for c in range(D_FF // FF_CHUNK):
        cols = slice(c * FF_CHUNK, (c + 1) * FF_CHUNK)
        gate = _dot(xn, wg_ref[:, cols])
        up = _dot(xn, wu_ref[:, cols])
        acc = acc + _dot((_silu(gate) * up).astype(BF16), wd_ref[cols, :])
    o_ref[...] = x + 0.5 * _rms(acc, post_ref[...])


def _ffn(x, pre, wg, wu, wd, post):
    t = x.shape[0]
    row = pl.BlockSpec((TM, D_MODEL), lambda i: (i, 0))
    return pl.pallas_call(
        _ffn_kernel,
        grid=(t // TM,),
        in_specs=[row, _const_spec((1, D_MODEL)), _const_spec((D_MODEL, D_FF)), _const_spec((D_MODEL, D_FF)),
                  _const_spec((D_FF, D_MODEL)), _const_spec((1, D_MODEL))],
        out_specs=row,
        out_shape=jax.ShapeDtypeStruct((t, D_MODEL), F32),
        compiler_params=_params("parallel"),
        name="ffn",
    )(x, pre.reshape(1, -1), wg.astype(BF16), wu.astype(BF16), wd.astype(BF16), post.reshape(1, -1))


PROJ_SPLITS = (("q", NSA_Q), ("kc", NSA_KV), ("vc", NSA_KV), ("kvsw", 4 * NSA_KV), ("dn", 3 * DN_W),
               ("z", DN_W), ("mq", X_W), ("sm", LANES))
PROJ_WIDTH = sum(w for _, w in PROJ_SPLITS)
PROJ_SCALES = {"q": NSA_HEAD_DIM ** -0.5, "mq": X_HEAD_DIM ** -0.5}


def _proj_kernel(x_ref, g_ref, w_ref, *out_refs):
    xn = _rms(x_ref[...], g_ref[...]).astype(BF16)
    start = 0
    for (name, width), ref in zip(PROJ_SPLITS, out_refs):
        for c0 in range(0, width, 512):
            cw = min(512, width - c0)
            y = _dot(xn, w_ref[:, start + c0:start + c0 + cw])
            if name in PROJ_SCALES:
                y = y * PROJ_SCALES[name]
            ref[:, c0:c0 + cw] = y.astype(ref.dtype)
        start += width


def _arrange_w_in(w_in):
    o_kv = NSA_Q
    o_g = o_kv + 6 * NSA_KV
    o_dn = o_g + N_BRANCH * NSA_HEADS
    o_a = o_dn + 3 * DN_W
    o_z = o_a + 2 * DN_HEADS
    o_mq = o_z + DN_W
    kv = w_in[:, o_kv:o_g].reshape(D_MODEL, 6, NSA_GROUPS, NSA_HEAD_DIM)
    kvsw = kv[:, 2:6].transpose(0, 2, 1, 3).reshape(D_MODEL, 4 * NSA_KV)
    small = jnp.concatenate([w_in[:, o_g:o_dn], w_in[:, o_a:o_z],
                             jnp.zeros((D_MODEL, LANES - N_BRANCH * NSA_HEADS - 2 * DN_HEADS), w_in.dtype)], axis=1)
    return jnp.concatenate([w_in[:, :o_kv], kv[:, 0].reshape(D_MODEL, NSA_KV), kv[:, 1].reshape(D_MODEL, NSA_KV),
                            kvsw, w_in[:, o_dn:o_a], w_in[:, o_z:o_mq], w_in[:, o_mq:], small], axis=1).astype(BF16)


def _proj(x, gain, w_arranged):
    t = x.shape[0]
    outs, specs = [], []
    for name, width in PROJ_SPLITS:
        outs.append(jax.ShapeDtypeStruct((t, width), F32 if name == "sm" else BF16))
        specs.append(pl.BlockSpec((TM, width), lambda i: (i, 0)))
    return pl.pallas_call(
        _proj_kernel,
        grid=(t // TM,),
        in_specs=[pl.BlockSpec((TM, D_MODEL), lambda i: (i, 0)), _const_spec((1, D_MODEL)),
                  _const_spec((D_MODEL, PROJ_WIDTH))],
        out_specs=specs,
        out_shape=outs,
        compiler_params=_params("parallel"),
        name="proj",
    )(x, gain.reshape(1, -1), w_arranged)


CMP_ROW = CMP_STRIDE * NSA_KV


def _cmp_kernel(kc_ref, vc_ref, posk_ref, posv_ref, w1k_ref, w1v_ref, w2k_ref, w2v_ref, ko_ref, vo_ref):
    n_rows = kc_ref.shape[1]

    def one(src_ref, pos_ref, w1_ref, w2_ref, out_ref):
        src = src_ref[0].astype(F32)
        first = _dot((src + pos_ref[0:1, :]).astype(BF16), w1_ref[0])
        second = _dot((src + pos_ref[1:2, :]).astype(BF16), w1_ref[1])
        hidden = _silu(first + pltpu.roll(second, n_rows - 1, 0))
        out = _dot(hidden.astype(BF16), w2_ref[...])
        for g in range(NSA_GROUPS):
            out_ref[0, g] = out[:, g * NSA_HEAD_DIM:(g + 1) * NSA_HEAD_DIM].astype(out_ref.dtype)

    one(kc_ref, posk_ref, w1k_ref, w2k_ref, ko_ref)
    one(vc_ref, posv_ref, w1v_ref, w2v_ref, vo_ref)


def _arrange_cmp(pos, w1, w2):
    hid2 = NSA_GROUPS * CMP_HIDDEN
    w1r = w1.reshape(2, CMP_STRIDE, NSA_HEAD_DIM, CMP_HIDDEN)
    eye = jnp.eye(NSA_GROUPS, dtype=w1.dtype)
    w1g = jnp.einsum("hidk,ge->higdek", w1r, eye).reshape(2, CMP_ROW, hid2).astype(BF16)
    w2g = jnp.einsum("kd,ge->gked", w2, eye).reshape(hid2, NSA_KV).astype(BF16)
    posr = jnp.broadcast_to(pos.reshape(2, CMP_STRIDE, 1, NSA_HEAD_DIM),
                            (2, CMP_STRIDE, NSA_GROUPS, NSA_HEAD_DIM)).reshape(2, CMP_ROW).astype(F32)
    return posr, w1g, w2g


def _cmp(kc, vc, posk, posv, w1k, w1v, w2k, w2v, batch, seq):
    n_rows = seq // CMP_STRIDE
    src = pl.BlockSpec((1, n_rows, CMP_ROW), lambda b: (b, 0, 0))
    out = pl.BlockSpec((1, NSA_GROUPS, n_rows, NSA_HEAD_DIM), lambda b: (b, 0, 0, 0))
    out_shape = jax.ShapeDtypeStruct((batch, NSA_GROUPS, n_rows, NSA_HEAD_DIM), BF16)
    hid2 = NSA_GROUPS * CMP_HIDDEN
    return pl.pallas_call(
        _cmp_kernel,
        grid=(batch,),
        in_specs=[src, src, _const_spec((2, CMP_ROW)), _const_spec((2, CMP_ROW)),
                  _const_spec((2, CMP_ROW, hid2)), _const_spec((2, CMP_ROW, hid2)),
                  _const_spec((hid2, NSA_KV)), _const_spec((hid2, NSA_KV))],
        out_specs=[out, out],
        out_shape=[out_shape, out_shape],
        compiler_params=_params("parallel"),
        name="cmp",
    )(kc.reshape(batch, n_rows, CMP_ROW), vc.reshape(batch, n_rows, CMP_ROW), posk, posv, w1k, w1v, w2k, w2v)


def _rel_bucket_table(n):
    exact = REL_BUCKETS // 2
    dist = np.arange(n)
    d = np.maximum(dist, 1).astype(np.float32)
    log_bucket = exact + (np.log(d / np.float32(exact)) / np.float32(math.log(REL_MAX_DIST / exact))
                          * np.float32(REL_BUCKETS - exact)).astype(np.int32)
    return np.where(dist < exact, dist, np.minimum(log_bucket, REL_BUCKETS - 1))


def _nsa_tables(rel_bias, seq):
    bucket = _rel_bucket_table(seq + 2 * TQ)
    n_cmp_pad = seq // CMP_STRIDE
    rel_bias = rel_bias.astype(F32)
    t = np.arange(seq)[:, None]
    block_end = np.arange(n_cmp_pad)[None, :] * CMP_STRIDE + CMP_BLOCK - 1
    dist = t - block_end
    cb = jnp.where((dist >= 0)[..., None], rel_bias[bucket[np.maximum(dist, 0)]], NEG_INF)
    cb = cb.reshape(seq // TQ, TQ, n_cmp_pad, NSA_GROUPS, NSA_HPG).transpose(3, 0, 4, 1, 2)
    cb = cb.reshape(NSA_GROUPS, seq // TQ, NSA_HPG * TQ, n_cmp_pad)
    i = np.arange(TQ)[:, None]
    j = np.arange(TQ)[None, :]
    far = rel_bias[REL_BUCKETS - 1]
    t0 = jnp.where((i >= j)[..., None], rel_bias[bucket[np.maximum(i - j, 0)]] - far, NEG_INF)
    t1 = rel_bias[bucket[TQ + i - j]] - far
    w2 = jnp.broadcast_to(jnp.asarray(np.where(j > i, 0.0, NEG_INF), F32)[..., None], t0.shape)
    tabs = jnp.stack([t0, t1, w2])
    tabs = tabs.reshape(3, TQ, TQ, NSA_GROUPS, NSA_HPG).transpose(3, 0, 4, 1, 2)
    tabs = tabs.reshape(NSA_GROUPS, 3, NSA_HPG * TQ, TQ)
    c0 = np.arange(n_cmp_pad)[None, :] * CMP_STRIDE
    s0 = np.arange(LANES)[:, None] * SEL_BLOCK
    overlap = np.maximum(np.minimum(c0 + CMP_BLOCK, s0 + SEL_BLOCK) - np.maximum(c0, s0), 0)
    valid = (np.arange(LANES)[:, None] < seq // SEL_BLOCK) & (np.arange(n_cmp_pad)[None, :] < n_cmp_pad - 1)
    to_sel_t = jnp.asarray(np.where(valid, overlap / CMP_BLOCK, 0.0), BF16)
    return cb, tabs, to_sel_t


def _nsa_kernel(q_ref, kv_ref, kc_ref, vc_ref, sm_ref, cb_ref, tab_ref, tosel_ref, o_ref):
    qi = pl.program_id(1)
    rows = NSA_HPG * TQ
    n_sel_blocks = kv_ref.shape[0] // SEL_BLOCK
    tok = qi * TQ + (lax.broadcasted_iota(jnp.int32, (rows, 1), 0) & (TQ - 1))
    has_block = jnp.where(tok >= CMP_BLOCK - 1, 1.0, 0.0)
    sm = sm_ref[...]
    kt1 = jnp.maximum(qi - 1, 0)
    kt2 = jnp.maximum(qi - 2, 0)
    pen1 = jnp.where(qi >= 1, 0.0, NEG_INF)
    pen2 = jnp.where(qi >= 2, 0.0, NEG_INF)

    def stack_heads(cols):
        return jnp.concatenate(cols, axis=0)

    def tile4(x):
        return jnp.broadcast_to(x[None], (NSA_HPG,) + x.shape).reshape(rows, x.shape[-1])

    def flash_step(qs, state, k_t, v_t, add):
        m, l, acc = state
        s = _dot_nt(qs, k_t) + add
        m_new = jnp.maximum(m, jnp.max(s, axis=-1, keepdims=True))
        alpha = jnp.exp(m - m_new)
        p = jnp.exp(s - m_new)
        l = alpha * l + jnp.sum(p, axis=-1, keepdims=True)
        acc = alpha * acc + _dot(p.astype(BF16), v_t)
        return m_new, l, acc

    for g in range(NSA_GROUPS):
        base = g * NSA_HPG * NSA_HEAD_DIM
        qs = stack_heads([q_ref[:, base + j * NSA_HEAD_DIM:base + (j + 1) * NSA_HEAD_DIM] for j in range(NSA_HPG)])

        def gate(branch, g=g):
            c0 = SM_GATE + branch * NSA_HEADS + g * NSA_HPG
            return stack_heads([jax.nn.sigmoid(sm[:, c0 + j:c0 + j + 1]) for j in range(NSA_HPG)])

        def kv_tile(kt, which, base=base):
            c0 = base + which * NSA_HEAD_DIM
            return kv_ref[pl.ds(pl.multiple_of(kt * TQ, TQ), TQ), c0:c0 + NSA_HEAD_DIM]

        s = _dot_nt(qs, kc_ref[0, g]) + cb_ref[g, 0]
        e = jnp.exp(s - jnp.max(s, axis=-1, keepdims=True))
        p = e / jnp.sum(e, axis=-1, keepdims=True) * has_block
        o_cmp = _dot(p.astype(BF16), vc_ref[0, g])
        p_sum = p[0:TQ] + p[TQ:2 * TQ] + p[2 * TQ:3 * TQ] + p[3 * TQ:4 * TQ]
        p_hi = p_sum.astype(BF16)
        p_lo = (p_sum - p_hi.astype(F32)).astype(BF16)
        imp_t = _dot_nt(tosel_ref[...], p_hi) + _dot_nt(tosel_ref[...], p_lo)

        imp_t = imp_t[0:n_sel_blocks]
        blk = lax.broadcasted_iota(jnp.int32, (n_sel_blocks, TQ), 0)
        cur = jnp.right_shift(qi * TQ + lax.broadcasted_iota(jnp.int32, (n_sel_blocks, TQ), 1), SEL_SHIFT)
        score = jnp.where(blk <= cur, imp_t, -FORCE_SCORE)
        for forced_blk in (0, cur, cur - 1):
            score = jnp.where(blk == forced_blk, FORCE_SCORE, score)
        rank = jnp.zeros((n_sel_blocks, TQ), F32)
        for m_blk in range(n_sel_blocks):
            other = score[m_blk:m_blk + 1, :]
            tie = jnp.where(blk > m_blk, 1.0, 0.0)
            rank = rank + jnp.where(other > score, 1.0, jnp.where(other == score, tie, 0.0))
        drop_t = jnp.where(rank < SEL_TOPK, jnp.where(score > -1.0, 0.0, 1.0), 1.0)
        drop_t = jnp.concatenate([drop_t, jnp.ones((LANES - n_sel_blocks, TQ), F32)], axis=0)
        drop = drop_t.T.astype(BF16)

        def sel_add(kt, drop=drop):
            key = kt * TQ + lax.broadcasted_iota(jnp.int32, (LANES, TQ), 1)
            hit = lax.broadcasted_iota(jnp.int32, (LANES, TQ), 0) == jnp.right_shift(key, SEL_SHIFT)
            return _dot(drop, jnp.where(hit, NEG_INF, 0.0).astype(BF16))

        init = (jnp.full((rows, 1), NEG_INF, F32), jnp.zeros((rows, 1), F32), jnp.zeros((rows, NSA_HEAD_DIM), F32))
        tab0 = tab_ref[g, 0]
        tab1 = tab_ref[g, 1] + pen1

        st = flash_step(qs, init, kv_tile(qi, 0), kv_tile(qi, 1), tab0 + tile4(sel_add(qi)))
        st = flash_step(qs, st, kv_tile(kt1, 0), kv_tile(kt1, 1), tab1 + tile4(sel_add(kt1)))

        def far_body(kt, st, qs=qs, kv_tile=kv_tile, sel_add=sel_add):
            return flash_step(qs, st, kv_tile(kt, 0), kv_tile(kt, 1), tile4(sel_add(kt)))

        _, l, acc = lax.fori_loop(0, kt1, far_body, st)
        o_sel = acc / l

        st = flash_step(qs, init, kv_tile(qi, 2), kv_tile(qi, 3), tab0)
        st = flash_step(qs, st, kv_tile(kt1, 2), kv_tile(kt1, 3), tab1)
        _, l, acc = flash_step(qs, st, kv_tile(kt2, 2), kv_tile(kt2, 3), tab_ref[g, 2] + pen2)
        o_win = acc / l

        o = gate(0) * o_cmp + gate(1) * o_sel + gate(2) * o_win
        o = jnp.concatenate([o[j * TQ:(j + 1) * TQ] for j in range(NSA_HPG)], axis=1)
        o_ref[:, base:base + NSA_HPG * NSA_HEAD_DIM] = o.astype(o_ref.dtype)


def _nsa(q, kvsw, k_cmp, v_cmp, sm, cb, tabs, to_sel_t, batch, seq):
    assert WINDOW == 2 * TQ and seq % TQ == 0
    nq = seq // TQ
    n_cmp_pad = seq // CMP_STRIDE
    rows = NSA_HPG * TQ
    cmp_spec = pl.BlockSpec((1, NSA_GROUPS, n_cmp_pad, NSA_HEAD_DIM), lambda b, i: (b, 0, 0, 0))
    return pl.pallas_call(
        _nsa_kernel,
        grid=(batch, nq),
        in_specs=[
            pl.BlockSpec((TQ, NSA_Q), lambda b, i: (b * nq + i, 0)),
            pl.BlockSpec((seq, 4 * NSA_KV), lambda b, i: (b, 0)),
            cmp_spec, cmp_spec,
            pl.BlockSpec((TQ, LANES), lambda b, i: (b * nq + i, 0)),
            pl.BlockSpec((NSA_GROUPS, 1, rows, n_cmp_pad), lambda b, i: (0, i, 0, 0)),
            _const_spec((NSA_GROUPS, 3, rows, TQ)),
            _const_spec((LANES, n_cmp_pad)),
        ],
        out_specs=pl.BlockSpec((TQ, NSA_Q), lambda b, i: (b * nq + i, 0)),
        out_shape=jax.ShapeDtypeStruct((batch * seq, NSA_Q), BF16),
        compiler_params=_params("parallel", "arbitrary"),
        name="nsa",
    )(q, kvsw, k_cmp, v_cmp, sm, cb, tabs, to_sel_t)


def _softplus(x):
    return jnp.maximum(x, 0.0) + jnp.log(1.0 + jnp.exp(-jnp.abs(x)))


def _l2_normalize(x):
    return x * lax.rsqrt(jnp.sum(x * x, axis=-1, keepdims=True) + NORM_EPS)


def _dn_kernel(dn_ref, z_ref, sm_ref, cw_ref, alog_ref, dtb_ref, gn_ref, o_ref, state_ref):
    c = DN_CHUNK
    n_chunks = dn_ref.shape[0] // c
    ii = lax.broadcasted_iota(jnp.int32, (c, c), 0)
    jj = lax.broadcasted_iota(jnp.int32, (c, c), 1)
    causal = ii >= jj
    strict = ii > jj
    split = ii ^ jj
    tri_ones = jnp.where(causal, 1.0, 0.0).astype(BF16)
    eye = jnp.where(ii == jj, 1.0, 0.0)
    state_ref[...] = jnp.zeros(state_ref.shape, F32)

    def chunk(ci, carry):
        r0 = pl.multiple_of(ci * c, c)
        rows = pl.ds(r0, c)
        x = dn_ref[rows, :].astype(F32)
        halo_rows = pl.ds(pl.multiple_of(jnp.maximum(r0 - CONV_HALO, 0), CONV_HALO), CONV_HALO)
        halo = dn_ref[halo_rows, :].astype(F32) * jnp.where(ci > 0, 1.0, 0.0)
        xc = jnp.concatenate([halo, x], axis=0)
        y = x * cw_ref[DN_CONV - 1:DN_CONV, :]
        for k in range(1, DN_CONV):
            y = y + pltpu.roll(xc, k, 0)[CONV_HALO:, :] * cw_ref[DN_CONV - 1 - k:DN_CONV - k, :]
        y = _silu(y)

        smc = sm_ref[rows, :]
        log_decay = -jnp.exp(alog_ref[...]) * _softplus(smc + dtb_ref[...])
        beta = jax.nn.sigmoid(smc)
        ld1 = log_decay.astype(BF16)
        rest = log_decay - ld1.astype(F32)
        ld2 = rest.astype(BF16)
        ld3 = (rest - ld2.astype(F32)).astype(BF16)
        gc = _dot(tri_ones, ld1) + _dot(tri_ones, ld2) + _dot(tri_ones, ld3)
        gc_t = gc.T
        gc_last = gc[c - 1:c, :]

        for h in range(DN_HEADS):
            hd = slice(h * DN_HEAD_DIM, (h + 1) * DN_HEAD_DIM)
            qh = _l2_normalize(y[:, hd]) * DN_HEAD_DIM ** -0.5
            kh = _l2_normalize(y[:, DN_W + h * DN_HEAD_DIM:DN_W + (h + 1) * DN_HEAD_DIM])
            vh = y[:, 2 * DN_W + h * DN_HEAD_DIM:2 * DN_W + (h + 1) * DN_HEAD_DIM]
            g_col = gc[:, SM_A + h:SM_A + h + 1]
            g_row = gc_t[SM_A + h:SM_A + h + 1, :]
            g_end = gc_last[:, SM_A + h:SM_A + h + 1]
            b_col = beta[:, SM_B + h:SM_B + h + 1]
            decay = jnp.exp(jnp.where(causal, g_col - g_row, NEG_INF))
            qb = qh.astype(BF16)
            kb = kh.astype(BF16)
            n_mat = jnp.where(strict, b_col * _dot_nt(kb, kb) * decay, 0.0)
            inv = eye - jnp.where(split == 1, n_mat, 0.0)
            for lvl in range(1, DN_LEVELS):
                n_lvl = jnp.where(jnp.right_shift(split, lvl) == 1, n_mat, 0.0).astype(BF16)
                inv_b = inv.astype(BF16)
                inv = inv - _dot(inv_b, _dot(n_lvl, inv_b).astype(BF16))
            e_col = jnp.exp(g_col)
            rhs = jnp.concatenate([vh * b_col, kh * (b_col * e_col)], axis=1).astype(BF16)
            sol = _dot(inv.astype(BF16), rhs)
            u = sol[:, :DN_HEAD_DIM]
            w = sol[:, DN_HEAD_DIM:]
            qk = _dot_nt(qb, kb) * decay
            state = state_ref[h]
            state_b = state.astype(BF16)
            v_new = (u - _dot(w.astype(BF16), state_b)).astype(BF16)
            o = _dot((qh * e_col).astype(BF16), state_b) + _dot(qk.astype(BF16), v_new)
            k_dec = kh * jnp.exp(g_end - g_col)
            state_ref[h] = state * jnp.exp(g_end) + _dot(k_dec.T.astype(BF16), v_new)
            zh = z_ref[rows, hd].astype(F32)
            o_ref[rows, hd] = (_rms(o, gn_ref[...]) * _silu(zh)).astype(o_ref.dtype)
        return carry

    lax.fori_loop(0, n_chunks, chunk, 0)


def _dn(dn, z, sm, conv_w, a_log, dt_bias, out_gain, batch, seq):
    assert DN_CHUNK == 1 << DN_LEVELS and seq % DN_CHUNK == 0
    pad = jnp.zeros((1, LANES), F32)
    return pl.pallas_call(
        _dn_kernel,
        grid=(batch,),
        in_specs=[pl.BlockSpec((seq, 3 * DN_W), lambda b: (b, 0)), pl.BlockSpec((seq, DN_W), lambda b: (b, 0)),
                  pl.BlockSpec((seq, LANES), lambda b: (b, 0)), _const_spec((DN_CONV, 3 * DN_W)),
                  _const_spec((1, LANES)), _const_spec((1, LANES)), _const_spec((1, DN_HEAD_DIM))],
        out_specs=pl.BlockSpec((seq, DN_W), lambda b: (b, 0)),
        out_shape=jax.ShapeDtypeStruct((batch * seq, DN_W), BF16),
        scratch_shapes=[pltpu.VMEM((DN_HEADS, DN_HEAD_DIM, DN_HEAD_DIM), F32)],
        compiler_params=_params("parallel"),
        name="dn",
    )(dn, z, sm, conv_w.astype(F32), pad.at[0, SM_A:SM_A + DN_HEADS].set(a_log.astype(F32)),
      pad.at[0, SM_A:SM_A + DN_HEADS].set(dt_bias.astype(F32)), out_gain.reshape(1, -1).astype(F32))


def _memkv_kernel(mem_ref, g_ref, w_ref, k_ref, v_ref):
    mn = _rms(mem_ref[0], g_ref[...]).astype(BF16)
    k_ref[0] = _dot(mn, w_ref[:, :X_W]).astype(k_ref.dtype)
    v_ref[0] = _dot(mn, w_ref[:, X_W:]).astype(v_ref.dtype)


def _memkv(mem, gain, w_kv):
    batch, n_mem, _ = mem.shape
    out = pl.BlockSpec((1, n_mem, X_W), lambda b: (b, 0, 0))
    shape = jax.ShapeDtypeStruct((batch, n_mem, X_W), BF16)
    return pl.pallas_call(
        _memkv_kernel,
        grid=(batch,),
        in_specs=[pl.BlockSpec((1, n_mem, D_MODEL), lambda b: (b, 0, 0)), _const_spec((1, D_MODEL)),
                  _const_spec((D_MODEL, 2 * X_W))],
        out_specs=[out, out],
        out_shape=[shape, shape],
        compiler_params=_params("parallel"),
        name="memkv",
    )(mem, gain.reshape(1, -1), w_kv.astype(BF16))


def _memattn_kernel(q_ref, k_ref, v_ref, o_ref):
    for h in range(X_HEADS):
        hd = slice(h * X_HEAD_DIM, (h + 1) * X_HEAD_DIM)
        s = _dot_nt(q_ref[:, hd], k_ref[0, :, hd])
        e = jnp.exp(s - jnp.max(s, axis=-1, keepdims=True))
        p = e / jnp.sum(e, axis=-1, keepdims=True)
        o_ref[:, hd] = _dot(p.astype(BF16), v_ref[0, :, hd]).astype(o_ref.dtype)


def _memattn(mq, mk, mv, batch, seq):
    nq = seq // TQ_MEM
    n_mem = mk.shape[1]
    kv = pl.BlockSpec((1, n_mem, X_W), lambda b, i: (b, 0, 0))
    row = pl.BlockSpec((TQ_MEM, X_W), lambda b, i: (b * nq + i, 0))
    return pl.pallas_call(
        _memattn_kernel,
        grid=(batch, nq),
        in_specs=[row, kv, kv],
        out_specs=row,
        out_shape=jax.ShapeDtypeStruct((batch * seq, X_W), BF16),
        compiler_params=_params("parallel", "arbitrary"),
        name="memattn",
    )(mq, mk, mv)


def _merge_kernel(x_ref, on_ref, od_ref, om_ref, gpre_ref, wbg_ref, wbn_ref, wbd_ref, wbm_ref, wo_ref, gpost_ref,
                  o_ref):
    x = x_ref[...]
    u = _rms(x, gpre_ref[...]).astype(BF16)
    merged = jnp.zeros(x.shape, F32)
    for i, (br_ref, w_ref) in enumerate(((on_ref, wbn_ref), (od_ref, wbd_ref), (om_ref, wbm_ref))):
        gates = jax.nn.sigmoid(_dot(u, wbg_ref[:, i * D_MODEL:(i + 1) * D_MODEL]))
        merged = merged + gates * _dot(br_ref[...], w_ref[...])
    y = _dot(merged.astype(BF16), wo_ref[...])
    o_ref[...] = x + _rms(y, gpost_ref[...])


def _merge(x, o_nsa, o_dn, o_mem, gpre, wbg, wbn, wbd, wbm, wo, gpost):
    t = x.shape[0]
    row = pl.BlockSpec((TM, D_MODEL), lambda i: (i, 0))
    br = pl.BlockSpec((TM, NSA_Q), lambda i: (i, 0))
    return pl.pallas_call(
        _merge_kernel,
        grid=(t // TM,),
        in_specs=[row, br, br, br, _const_spec((1, D_MODEL)), _const_spec((D_MODEL, N_BRANCH * D_MODEL)),
                  _const_spec((NSA_Q, D_MODEL)), _const_spec((DN_W, D_MODEL)), _const_spec((X_W, D_MODEL)),
                  _const_spec((D_MODEL, D_MODEL)), _const_spec((1, D_MODEL))],
        out_specs=row,
        out_shape=jax.ShapeDtypeStruct((t, D_MODEL), F32),
        compiler_params=_params("parallel"),
        name="merge",
    )(x, o_nsa, o_dn, o_mem, gpre.reshape(1, -1), wbg.astype(BF16), wbn.astype(BF16), wbd.astype(BF16),
      wbm.astype(BF16), wo.astype(BF16), gpost.reshape(1, -1))


def kernel(x, mem, ffn1_pre_norm, ffn1_w_gate, ffn1_w_up, ffn1_w_down, ffn1_post_norm, mix_pre_norm, w_in, cmp_pos_k, cmp_pos_v, cmp_k_w1, cmp_k_w2, cmp_v_w1, cmp_v_w2, rel_bias, dn_conv_w, dn_a_log, dn_dt_bias, dn_out_norm, mem_norm, w_mem_kv, w_branch_nsa, w_branch_dn, w_branch_mem, w_branch_gate, w_out, mix_post_norm, ffn2_pre_norm, ffn2_w_gate, ffn2_w_up, ffn2_w_down, ffn2_post_norm):
    batch, seq, d = x.shape
    xt = x.reshape(batch * seq, d)
    cb, tabs, to_sel_t = _nsa_tables(rel_bias, seq)
    for l in range(ffn1_pre_norm.shape[0]):
        xt = _ffn(xt, ffn1_pre_norm[l], ffn1_w_gate[l], ffn1_w_up[l], ffn1_w_down[l], ffn1_post_norm[l])

        q, kc, vc, kvsw, dn, z, mq, sm = _proj(xt, mix_pre_norm[l], _arrange_w_in(w_in[l]))
        posk, w1k, w2k = _arrange_cmp(cmp_pos_k[l], cmp_k_w1[l], cmp_k_w2[l])
        posv, w1v, w2v = _arrange_cmp(cmp_pos_v[l], cmp_v_w1[l], cmp_v_w2[l])
        k_cmp, v_cmp = _cmp(kc, vc, posk, posv, w1k, w1v, w2k, w2v, batch, seq)
        o_nsa = _nsa(q, kvsw, k_cmp, v_cmp, sm, cb, tabs, to_sel_t, batch, seq)
        o_dn = _dn(dn, z, sm, dn_conv_w[l], dn_a_log[l], dn_dt_bias[l], dn_out_norm[l], batch, seq)
        mk, mv = _memkv(mem, mem_norm[l], w_mem_kv[l])
        o_mem = _memattn(mq, mk, mv, batch, seq)
        xt = _merge(xt, o_nsa, o_dn, o_mem, mix_pre_norm[l], w_branch_gate[l], w_branch_nsa[l], w_branch_dn[l],
                    w_branch_mem[l], w_out[l], mix_post_norm[l])

        xt = _ffn(xt, ffn2_pre_norm[l], ffn2_w_gate[l], ffn2_w_up[l], ffn2_w_down[l], ffn2_post_norm[l])
    return xt.reshape(batch, seq, d)
```

```python
import math

import jax
import jax.numpy as jnp
import numpy as np
from jax import lax
from jax.experimental import pallas as pl
from jax.experimental.pallas import tpu as pltpu

F32 = jnp.float32
BF16 = jnp.bfloat16

D_MODEL = 1024
D_FF = 2816
NORM_EPS = 1e-6
NEG_INF = -1e30
FORCE_SCORE = 1e4

NSA_HEADS = 8
NSA_GROUPS = 2
NSA_HPG = NSA_HEADS // NSA_GROUPS
NSA_HEAD_DIM = 64
CMP_BLOCK = 32
CMP_STRIDE = 16
CMP_HIDDEN = 256
SEL_BLOCK = 64
SEL_SHIFT = 6
SEL_TOPK = 16
WINDOW = 512
REL_BUCKETS = 32
REL_MAX_DIST = 128

DN_HEADS = 4
DN_HEAD_DIM = 128
DN_CONV = 4
X_HEADS = 4
X_HEAD_DIM = 128
N_BRANCH = 3

NSA_Q = NSA_HEADS * NSA_HEAD_DIM
NSA_KV = NSA_GROUPS * NSA_HEAD_DIM
DN_W = DN_HEADS * DN_HEAD_DIM
X_W = X_HEADS * X_HEAD_DIM

LANES = 128
VMEM_LIMIT = 56 * 1024 * 1024

TM = 512
FF_CHUNK = 256
TQ = 256
NSA_RB = 128
DN_CHUNK = 128
DN_LEVELS = 7
DN_GROUP = 2
CONV_HALO = 16
TQ_MEM = 512

SM_GATE = 0
SM_A = N_BRANCH * NSA_HEADS
SM_B = SM_A + DN_HEADS


def _dot(a, b):
    return jnp.dot(a, b, preferred_element_type=F32)


def _dot_nt(a, b):
    return lax.dot_general(a, b, (((1,), (1,)), ((), ())), preferred_element_type=F32)


def _bmm(a, b):
    return lax.dot_general(a, b, (((2,), (1,)), ((0,), (0,))), preferred_element_type=F32)


def _bmm_nt(a, b):
    return lax.dot_general(a, b, (((2,), (2,)), ((0,), (0,))), preferred_element_type=F32)


def _rms(x, gain):
    return x * lax.rsqrt(jnp.mean(x * x, axis=-1, keepdims=True) + NORM_EPS) * gain


def _silu(x):
    return x * jax.nn.sigmoid(x)


def _const_spec(shape):
    zeros = (0,) * len(shape)
    return pl.BlockSpec(shape, lambda *_: zeros, pipeline_mode=pl.Buffered(1))


def _params(*sem):
    return pltpu.CompilerParams(dimension_semantics=sem, vmem_limit_bytes=VMEM_LIMIT)


def _ffn_kernel(x_ref, pre_ref, wg_ref, wu_ref, wd_ref, post_ref, o_ref):
    x = x_ref[...]
    xn = _rms(x, pre_ref[...]).astype(BF16)
    acc = jnp.zeros(x.shape, F32)
    for c in range(D_FF // FF_CHUNK):
        cols = slice(c * FF_CHUNK, (c + 1) * FF_CHUNK)
        gate = _dot(xn, wg_ref[:, cols])
        up = _dot(xn, wu_ref[:, cols])
        acc = acc + _dot((_silu(gate) * up).astype(BF16), wd_ref[cols, :])
    o_ref[...] = x + 0.5 * _rms(acc, post_ref[...])


def _ffn(x, pre, wg, wu, wd, post):
    t = x.shape[0]
    row = pl.BlockSpec((TM, D_MODEL), lambda i: (i, 0))
    return pl.pallas_call(
        _ffn_kernel,
        grid=(t // TM,),
        in_specs=[row, _const_spec((1, D_MODEL)), _const_spec((D_MODEL, D_FF)), _const_spec((D_MODEL, D_FF)),
                  _const_spec((D_FF, D_MODEL)), _const_spec((1, D_MODEL))],
        out_specs=row,
        out_shape=jax.ShapeDtypeStruct((t, D_MODEL), F32),
        compiler_params=_params("parallel"),
        name="ffn",
    )(x, pre.reshape(1, -1), wg.astype(BF16), wu.astype(BF16), wd.astype(BF16), post.reshape(1, -1))


PROJ_SPLITS = (("q", NSA_Q), ("kc", NSA_KV), ("vc", NSA_KV), ("kvsw", 4 * NSA_KV), ("dn", 3 * DN_W),
               ("z", DN_W), ("mq", X_W), ("sm", LANES))
PROJ_WIDTH = sum(w for _, w in PROJ_SPLITS)
LOG2E = math.log2(math.e)
PROJ_SCALES = {"q": NSA_HEAD_DIM ** -0.5 * LOG2E, "mq": X_HEAD_DIM ** -0.5}


def _proj_kernel(x_ref, g_ref, w_ref, *out_refs):
    xn = _rms(x_ref[...], g_ref[...]).astype(BF16)
    start = 0
    for (name, width), ref in zip(PROJ_SPLITS, out_refs):
        for c0 in range(0, width, 512):
            cw = min(512, width - c0)
            y = _dot(xn, w_ref[:, start + c0:start + c0 + cw])
            if name in PROJ_SCALES:
                y = y * PROJ_SCALES[name]
            ref[:, c0:c0 + cw] = y.astype(ref.dtype)
        start += width


def _arrange_w_in(w_in):
    o_kv = NSA_Q
    o_g = o_kv + 6 * NSA_KV
    o_dn = o_g + N_BRANCH * NSA_HEADS
    o_a = o_dn + 3 * DN_W
    o_z = o_a + 2 * DN_HEADS
    o_mq = o_z + DN_W
    kv = w_in[:, o_kv:o_g].reshape(D_MODEL, 6, NSA_GROUPS, NSA_HEAD_DIM)
    kvsw = kv[:, 2:6].transpose(0, 2, 1, 3).reshape(D_MODEL, 4 * NSA_KV)
    small = jnp.concatenate([w_in[:, o_g:o_dn], w_in[:, o_a:o_z],
                             jnp.zeros((D_MODEL, LANES - N_BRANCH * NSA_HEADS - 2 * DN_HEADS), w_in.dtype)], axis=1)
    return jnp.concatenate([w_in[:, :o_kv], kv[:, 0].reshape(D_MODEL, NSA_KV), kv[:, 1].reshape(D_MODEL, NSA_KV),
                            kvsw, w_in[:, o_dn:o_a], w_in[:, o_z:o_mq], w_in[:, o_mq:], small], axis=1).astype(BF16)


def _proj(x, gain, w_arranged):
    t = x.shape[0]
    outs, specs = [], []
    for name, width in PROJ_SPLITS:
        outs.append(jax.ShapeDtypeStruct((t, width), F32 if name == "sm" else BF16))
        specs.append(pl.BlockSpec((TM, width), lambda i: (i, 0)))
    return pl.pallas_call(
        _proj_kernel,
        grid=(t // TM,),
        in_specs=[pl.BlockSpec((TM, D_MODEL), lambda i: (i, 0)), _const_spec((1, D_MODEL)),
                  _const_spec((D_MODEL, PROJ_WIDTH))],
        out_specs=specs,
        out_shape=outs,
        compiler_params=_params("parallel"),
        name="proj",
    )(x, gain.reshape(1, -1), w_arranged)


CMP_ROW = CMP_STRIDE * NSA_KV


def _cmp_kernel(kc_ref, vc_ref, posk_ref, posv_ref, w1k_ref, w1v_ref, w2k_ref, w2v_ref, ko_ref, vo_ref):
    n_rows = kc_ref.shape[1]

    def one(src_ref, pos_ref, w1_ref, w2_ref, out_ref):
        src = src_ref[0].astype(F32)
        first = _dot((src + pos_ref[0:1, :]).astype(BF16), w1_ref[0])
        second = _dot((src + pos_ref[1:2, :]).astype(BF16), w1_ref[1])
        hidden = _silu(first + pltpu.roll(second, n_rows - 1, 0))
        out = _dot(hidden.astype(BF16), w2_ref[...])
        for g in range(NSA_GROUPS):
            out_ref[0, g] = out[:, g * NSA_HEAD_DIM:(g + 1) * NSA_HEAD_DIM].astype(out_ref.dtype)

    one(kc_ref, posk_ref, w1k_ref, w2k_ref, ko_ref)
    one(vc_ref, posv_ref, w1v_ref, w2v_ref, vo_ref)


def _arrange_cmp(pos, w1, w2):
    hid2 = NSA_GROUPS * CMP_HIDDEN
    w1r = w1.reshape(2, CMP_STRIDE, NSA_HEAD_DIM, CMP_HIDDEN)
    eye = jnp.eye(NSA_GROUPS, dtype=w1.dtype)
    w1g = jnp.einsum("hidk,ge->higdek", w1r, eye).reshape(2, CMP_ROW, hid2).astype(BF16)
    w2g = jnp.einsum("kd,ge->gked", w2, eye).reshape(hid2, NSA_KV).astype(BF16)
    posr = jnp.broadcast_to(pos.reshape(2, CMP_STRIDE, 1, NSA_HEAD_DIM),
                            (2, CMP_STRIDE, NSA_GROUPS, NSA_HEAD_DIM)).reshape(2, CMP_ROW).astype(F32)
    return posr, w1g, w2g


def _cmp(kc, vc, posk, posv, w1k, w1v, w2k, w2v, batch, seq):
    n_rows = seq // CMP_STRIDE
    src = pl.BlockSpec((1, n_rows, CMP_ROW), lambda b: (b, 0, 0))
    out = pl.BlockSpec((1, NSA_GROUPS, n_rows, NSA_HEAD_DIM), lambda b: (b, 0, 0, 0))
    out_shape = jax.ShapeDtypeStruct((batch, NSA_GROUPS, n_rows, NSA_HEAD_DIM), BF16)
    hid2 = NSA_GROUPS * CMP_HIDDEN
    return pl.pallas_call(
        _cmp_kernel,
        grid=(batch,),
        in_specs=[src, src, _const_spec((2, CMP_ROW)), _const_spec((2, CMP_ROW)),
                  _const_spec((2, CMP_ROW, hid2)), _const_spec((2, CMP_ROW, hid2)),
                  _const_spec((hid2, NSA_KV)), _const_spec((hid2, NSA_KV))],
        out_specs=[out, out],
        out_shape=[out_shape, out_shape],
        compiler_params=_params("parallel"),
        name="cmp",
    )(kc.reshape(batch, n_rows, CMP_ROW), vc.reshape(batch, n_rows, CMP_ROW), posk, posv, w1k, w1v, w2k, w2v)


def _rel_bucket_table(n):
    exact = REL_BUCKETS // 2
    dist = np.arange(n)
    d = np.maximum(dist, 1).astype(np.float32)
    log_bucket = exact + (np.log(d / np.float32(exact)) / np.float32(math.log(REL_MAX_DIST / exact))
                          * np.float32(REL_BUCKETS - exact)).astype(np.int32)
    return np.where(dist < exact, dist, np.minimum(log_bucket, REL_BUCKETS - 1))


def _toeplitz(w, n_rows, n_cols):
    p = w.shape[-1]
    flat = jnp.tile(w, (1,) * (w.ndim - 1) + (n_rows,))[..., :n_rows * (p - 1)]
    return flat.reshape(w.shape[:-1] + (n_rows, p - 1))[..., :n_cols]


def _nsa_tables(rel_bias, seq):
    n_dist = seq + 2 * TQ
    masked = n_dist
    bucket = _rel_bucket_table(n_dist)
    n_cmp_pad = seq // CMP_STRIDE
    by_dist = (rel_bias.astype(F32) * LOG2E)[bucket].T
    neg = jnp.full((NSA_HEADS, 1), NEG_INF, F32)
    p = 2 * n_cmp_pad
    k = np.arange(p)[None, :]
    a_minus_c = np.where(k < n_cmp_pad, -k, p - k)
    dist = CMP_STRIDE * a_minus_c + np.arange(CMP_STRIDE)[:, None] - (CMP_BLOCK - 1)
    w = jnp.concatenate([by_dist, neg], axis=1)[:, np.where(dist >= 0, dist, masked)]
    cb = _toeplitz(w, n_cmp_pad, n_cmp_pad).transpose(0, 2, 1, 3)
    cb = cb.reshape(NSA_GROUPS, NSA_HPG, seq // TQ, TQ, n_cmp_pad).transpose(0, 2, 1, 3, 4)
    cb = cb.reshape(NSA_GROUPS, seq // TQ, NSA_HPG * TQ, n_cmp_pad)
    shifted = jnp.concatenate([by_dist - by_dist[:, n_dist - 1:], neg], axis=1)
    k = np.arange(2 * TQ)
    i_minus_j = np.where(k < TQ, -k, 2 * TQ - k)
    t0 = _toeplitz(shifted[:, np.where(i_minus_j >= 0, i_minus_j, masked)], TQ, TQ)
    t1 = _toeplitz(shifted[:, TQ + i_minus_j], TQ, TQ)
    w2 = jnp.where(lax.broadcasted_iota(jnp.int32, (TQ, TQ), 1) > lax.broadcasted_iota(jnp.int32, (TQ, TQ), 0),
                   0.0, NEG_INF).astype(F32)
    tabs = jnp.stack([t0, t1, jnp.broadcast_to(w2, t0.shape)], axis=1)
    tabs = tabs.reshape(NSA_GROUPS, NSA_HPG, 3, TQ, TQ).transpose(0, 2, 1, 3, 4)
    tabs = tabs.reshape(NSA_GROUPS, 3, NSA_HPG * TQ, TQ)
    c0 = np.arange(n_cmp_pad)[None, :] * CMP_STRIDE
    s0 = np.arange(LANES)[:, None] * SEL_BLOCK
    overlap = np.maximum(np.minimum(c0 + CMP_BLOCK, s0 + SEL_BLOCK) - np.maximum(c0, s0), 0)
    valid = (np.arange(LANES)[:, None] < seq // SEL_BLOCK) & (np.arange(n_cmp_pad)[None, :] < n_cmp_pad - 1)
    to_sel_t = jnp.asarray(np.where(valid, overlap / CMP_BLOCK, 0.0), BF16)
    return cb, tabs, to_sel_t


def _nsa_kernel(q_ref, kv_ref, kc_ref, vc_ref, sm_ref, cb_ref, tab_ref, tosel_ref, o_ref):
    qi = pl.program_id(1)
    n_sel_blocks = kv_ref.shape[0] // SEL_BLOCK
    halves = TQ // NSA_RB
    n_blocks = NSA_HPG * halves
    sm = sm_ref[...]
    ones_cols = jnp.ones((TQ, NSA_HEAD_DIM), BF16)

    def q_rows(r):
        half = r % halves
        return slice(half * NSA_RB, (half + 1) * NSA_RB)

    def tab_rows(r):
        return slice(r * NSA_RB, (r + 1) * NSA_RB)

    def flash_step(q_blocks, state, k_t, v_t, add_fn):
        v_ext = jnp.concatenate([v_t, ones_cols], axis=1)
        out = []
        for r in range(n_blocks):
            m, acc = state[r]
            s = _dot_nt(q_blocks[r], k_t) + add_fn(r)
            m_new = jnp.maximum(m, jnp.max(s, axis=-1, keepdims=True))
            acc = jnp.exp2(m - m_new) * acc + _dot(jnp.exp2(s - m_new).astype(BF16), v_ext)
            out.append((m_new, acc))
        return out

    def normalized(state, r):
        acc = state[r][1]
        return acc[:, :NSA_HEAD_DIM] / acc[:, NSA_HEAD_DIM:NSA_HEAD_DIM + 1]

    for g in range(NSA_GROUPS):
        base = g * NSA_HPG * NSA_HEAD_DIM
        q_blocks = [q_ref[q_rows(r), base + (r // halves) * NSA_HEAD_DIM:base + (r // halves + 1) * NSA_HEAD_DIM]
                    for r in range(n_blocks)]

        def gate(branch, r, g=g):
            col = SM_GATE + branch * NSA_HEADS + g * NSA_HPG + r // halves
            return jax.nn.sigmoid(sm[q_rows(r), col:col + 1])

        def kv_tile(kt, which, base=base):
            c0 = base + which * NSA_HEAD_DIM
            return kv_ref[pl.ds(pl.multiple_of(kt * TQ, TQ), TQ), c0:c0 + NSA_HEAD_DIM]

        o_cmp = []
        p_sum = [jnp.zeros((NSA_RB, LANES), F32) for _ in range(halves)]
        for r in range(n_blocks):
            half = r % halves
            tok = qi * TQ + half * NSA_RB + lax.broadcasted_iota(jnp.int32, (NSA_RB, 1), 0)
            s = _dot_nt(q_blocks[r], kc_ref[0, g]) + cb_ref[g, 0, tab_rows(r), :]
            e = jnp.exp2(s - jnp.max(s, axis=-1, keepdims=True))
            p = e / jnp.sum(e, axis=-1, keepdims=True) * jnp.where(tok >= CMP_BLOCK - 1, 1.0, 0.0)
            o_cmp.append(_dot(p.astype(BF16), vc_ref[0, g]))
            p_sum[half] = p_sum[half] + p
        p_sum = jnp.concatenate(p_sum, axis=0)
        p_hi = p_sum.astype(BF16)
        p_lo = (p_sum - p_hi.astype(F32)).astype(BF16)
        imp_t = _dot_nt(tosel_ref[...], p_hi) + _dot_nt(tosel_ref[...], p_lo)

        imp_t = imp_t[0:n_sel_blocks]
        blk = lax.broadcasted_iota(jnp.int32, (n_sel_blocks, TQ), 0)
        cur = jnp.right_shift(qi * TQ + lax.broadcasted_iota(jnp.int32, (n_sel_blocks, TQ), 1), SEL_SHIFT)
        score = jnp.where(blk <= cur, imp_t, -FORCE_SCORE)
        for forced_blk in (0, cur, cur - 1):
            score = jnp.where(blk == forced_blk, FORCE_SCORE, score)
        rank = jnp.zeros((n_sel_blocks, TQ), F32)
        for m_blk in range(n_sel_blocks):
            other = score[m_blk:m_blk + 1, :]
            tie = jnp.where(blk > m_blk, 1.0, 0.0)
            rank = rank + jnp.where(other > score, 1.0, jnp.where(other == score, tie, 0.0))
        drop_t = jnp.where(rank < SEL_TOPK, jnp.where(score > -1.0, 0.0, 1.0), 1.0)
        drop_t = jnp.concatenate([drop_t, jnp.ones((LANES - n_sel_blocks, TQ), F32)], axis=0)
        drop = drop_t.T.astype(BF16)

        def sel_mask(kt, drop=drop):
            key = kt * TQ + lax.broadcasted_iota(jnp.int32, (LANES, TQ), 1)
            hit = lax.broadcasted_iota(jnp.int32, (LANES, TQ), 0) == jnp.right_shift(key, SEL_SHIFT)
            return _dot(drop, jnp.where(hit, NEG_INF, 0.0).astype(BF16))

        def sel_step(st, kt, tab, q_blocks=q_blocks, kv_tile=kv_tile, sel_mask=sel_mask, g=g):
            mask = sel_mask(kt)
            if tab is None:
                add_fn = lambda r: mask[q_rows(r)]
            else:
                add_fn = lambda r: tab_ref[g, tab, tab_rows(r), :] + mask[q_rows(r)]
            return flash_step(q_blocks, st, kv_tile(kt, 0), kv_tile(kt, 1), add_fn)

        def win_step(st, kt, tab, q_blocks=q_blocks, kv_tile=kv_tile, g=g):
            return flash_step(q_blocks, st, kv_tile(kt, 2), kv_tile(kt, 3), lambda r: tab_ref[g, tab, tab_rows(r), :])

        init = [(jnp.full((NSA_RB, 1), NEG_INF, F32), jnp.zeros((NSA_RB, 2 * NSA_HEAD_DIM), F32))
                for _ in range(n_blocks)]

        st = sel_step(init, qi, 0)
        st = lax.cond(qi >= 1, lambda st, f=sel_step: f(st, qi - 1, 1), lambda st: st, st)
        sel = lax.fori_loop(0, jnp.maximum(qi - 1, 0), lambda kt, st, f=sel_step: f(st, kt, None), st)

        st = win_step(init, qi, 0)
        st = lax.cond(qi >= 1, lambda st, f=win_step: f(st, qi - 1, 1), lambda st: st, st)
        win = lax.cond(qi >= 2, lambda st, f=win_step: f(st, qi - 2, 2), lambda st: st, st)

        for r in range(n_blocks):
            o = gate(0, r) * o_cmp[r] + gate(1, r) * normalized(sel, r) + gate(2, r) * normalized(win, r)
            c0 = base + (r // halves) * NSA_HEAD_DIM
            o_ref[q_rows(r), c0:c0 + NSA_HEAD_DIM] = o.astype(o_ref.dtype)


def _nsa(q, kvsw, k_cmp, v_cmp, sm, cb, tabs, to_sel_t, batch, seq):
    assert WINDOW == 2 * TQ and seq % TQ == 0
    nq = seq // TQ
    n_cmp_pad = seq // CMP_STRIDE
    rows = NSA_HPG * TQ
    cmp_spec = pl.BlockSpec((1, NSA_GROUPS, n_cmp_pad, NSA_HEAD_DIM), lambda b, i: (b, 0, 0, 0))
    return pl.pallas_call(
        _nsa_kernel,
        grid=(batch, nq),
        in_specs=[
            pl.BlockSpec((TQ, NSA_Q), lambda b, i: (b * nq + i, 0)),
            pl.BlockSpec((seq, 4 * NSA_KV), lambda b, i: (b, 0)),
            cmp_spec, cmp_spec,
            pl.BlockSpec((TQ, LANES), lambda b, i: (b * nq + i, 0)),
            pl.BlockSpec((NSA_GROUPS, 1, rows, n_cmp_pad), lambda b, i: (0, i, 0, 0)),
            _const_spec((NSA_GROUPS, 3, rows, TQ)),
            _const_spec((LANES, n_cmp_pad)),
        ],
        out_specs=pl.BlockSpec((TQ, NSA_Q), lambda b, i: (b * nq + i, 0)),
        out_shape=jax.ShapeDtypeStruct((batch * seq, NSA_Q), BF16),
        compiler_params=_params("parallel", "arbitrary"),
        name="nsa",
    )(q, kvsw, k_cmp, v_cmp, sm, cb, tabs, to_sel_t)


def _softplus(x):
    return jnp.maximum(x, 0.0) + jnp.log(1.0 + jnp.exp(-jnp.abs(x)))


def _l2_normalize(x):
    return x * lax.rsqrt(jnp.sum(x * x, axis=-1, keepdims=True) + NORM_EPS)


def _dn_kernel(dn_ref, z_ref, sm_ref, cw_ref, alog_ref, dtb_ref, gn_ref, o_ref, state_ref):
    c = DN_CHUNK
    n_chunks = dn_ref.shape[0] // c
    ii = lax.broadcasted_iota(jnp.int32, (c, c), 0)
    jj = lax.broadcasted_iota(jnp.int32, (c, c), 1)
    causal = ii >= jj
    strict = ii > jj
    split = ii ^ jj
    tri_ones = jnp.where(causal, 1.0, 0.0).astype(BF16)
    eye = jnp.where(ii == jj, 1.0, 0.0)
    state_ref[...] = jnp.zeros(state_ref.shape, F32)

    span = c * DN_GROUP

    def group(gi, carry):
        r0 = pl.multiple_of(gi * span, span)
        rows = pl.ds(r0, span)
        x = dn_ref[rows, :].astype(F32)
        halo_rows = pl.ds(pl.multiple_of(jnp.maximum(r0 - CONV_HALO, 0), CONV_HALO), CONV_HALO)
        halo = dn_ref[halo_rows, :].astype(F32) * jnp.where(gi > 0, 1.0, 0.0)
        xc = jnp.concatenate([halo, x], axis=0)
        y = x * cw_ref[DN_CONV - 1:DN_CONV, :]
        for k in range(1, DN_CONV):
            y = y + pltpu.roll(xc, k, 0)[CONV_HALO:, :] * cw_ref[DN_CONV - 1 - k:DN_CONV - k, :]
        y = _silu(y)

        smc = sm_ref[rows, :]
        log_decay = -jnp.exp(alog_ref[...]) * _softplus(smc + dtb_ref[...])
        beta = jax.nn.sigmoid(smc)
        ld1 = log_decay.astype(BF16)
        rest = log_decay - ld1.astype(F32)
        ld2 = rest.astype(BF16)
        ld3 = (rest - ld2.astype(F32)).astype(BF16)

        q_l, k_l, v_l, gcol_l, grow_l, gend_l, bcol_l = [], [], [], [], [], [], []
        for k in range(DN_GROUP):
            cr = slice(k * c, (k + 1) * c)
            gc = _dot(tri_ones, ld1[cr]) + _dot(tri_ones, ld2[cr]) + _dot(tri_ones, ld3[cr])
            gc_t = gc.T
            for h in range(DN_HEADS):
                q_l.append(_l2_normalize(y[cr, h * DN_HEAD_DIM:(h + 1) * DN_HEAD_DIM]) * DN_HEAD_DIM ** -0.5)
                k_l.append(_l2_normalize(y[cr, DN_W + h * DN_HEAD_DIM:DN_W + (h + 1) * DN_HEAD_DIM]))
                v_l.append(y[cr, 2 * DN_W + h * DN_HEAD_DIM:2 * DN_W + (h + 1) * DN_HEAD_DIM])
                gcol_l.append(gc[:, SM_A + h:SM_A + h + 1])
                grow_l.append(gc_t[SM_A + h:SM_A + h + 1, :])
                gend_l.append(gc[c - 1:c, SM_A + h:SM_A + h + 1])
                bcol_l.append(beta[cr, SM_B + h:SM_B + h + 1])
        qs, ks, vs = jnp.stack(q_l), jnp.stack(k_l), jnp.stack(v_l)
        g_col, g_row, g_end, b_col = jnp.stack(gcol_l), jnp.stack(grow_l), jnp.stack(gend_l), jnp.stack(bcol_l)

        decay = jnp.exp(jnp.where(causal, g_col - g_row, NEG_INF))
        qb = qs.astype(BF16)
        kb = ks.astype(BF16)
        n_mat = jnp.where(strict, b_col * _bmm_nt(kb, kb) * decay, 0.0)
        inv = eye - jnp.where(split == 1, n_mat, 0.0)
        for lvl in range(1, DN_LEVELS):
            n_lvl = jnp.where(jnp.right_shift(split, lvl) == 1, n_mat, 0.0).astype(BF16)
            inv_b = inv.astype(BF16)
            inv = inv - _bmm(inv_b, _bmm(n_lvl, inv_b).astype(BF16))
        e_col = jnp.exp(g_col)
        rhs = jnp.concatenate([vs * b_col, ks * (b_col * e_col)], axis=2).astype(BF16)
        sol = _bmm(inv.astype(BF16), rhs)
        u = sol[:, :, :DN_HEAD_DIM]
        w = sol[:, :, DN_HEAD_DIM:].astype(BF16)
        qk = (_bmm_nt(qb, kb) * decay).astype(BF16)
        q_dec = (qs * e_col).astype(BF16)
        k_dec = ks * jnp.exp(g_end - g_col)
        k_dec_t = jnp.stack([k_dec[n].T for n in range(DN_GROUP * DN_HEADS)]).astype(BF16)
        g_chunk = jnp.exp(g_end)

        state = state_ref[...]
        for k in range(DN_GROUP):
            hs = slice(k * DN_HEADS, (k + 1) * DN_HEADS)
            state_b = state.astype(BF16)
            v_new = (u[hs] - _bmm(w[hs], state_b)).astype(BF16)
            o = _bmm(q_dec[hs], state_b) + _bmm(qk[hs], v_new)
            state = state * g_chunk[hs] + _bmm(k_dec_t[hs], v_new)
            out_rows = pl.ds(r0 + k * c, c)
            for h in range(DN_HEADS):
                hd = slice(h * DN_HEAD_DIM, (h + 1) * DN_HEAD_DIM)
                zh = z_ref[out_rows, hd].astype(F32)
                o_ref[out_rows, hd] = (_rms(o[h], gn_ref[...]) * _silu(zh)).astype(o_ref.dtype)
        state_ref[...] = state
        return carry

    lax.fori_loop(0, n_chunks // DN_GROUP, group, 0)


def _dn(dn, z, sm, conv_w, a_log, dt_bias, out_gain, batch, seq):
    assert DN_CHUNK == 1 << DN_LEVELS and seq % (DN_CHUNK * DN_GROUP) == 0
    pad = jnp.zeros((1, LANES), F32)
    return pl.pallas_call(
        _dn_kernel,
        grid=(batch,),
        in_specs=[pl.BlockSpec((seq, 3 * DN_W), lambda b: (b, 0)), pl.BlockSpec((seq, DN_W), lambda b: (b, 0)),
                  pl.BlockSpec((seq, LANES), lambda b: (b, 0)), _const_spec((DN_CONV, 3 * DN_W)),
                  _const_spec((1, LANES)), _const_spec((1, LANES)), _const_spec((1, DN_HEAD_DIM))],
        out_specs=pl.BlockSpec((seq, DN_W), lambda b: (b, 0)),
        out_shape=jax.ShapeDtypeStruct((batch * seq, DN_W), BF16),
        scratch_shapes=[pltpu.VMEM((DN_HEADS, DN_HEAD_DIM, DN_HEAD_DIM), F32)],
        compiler_params=_params("parallel"),
        name="dn",
    )(dn, z, sm, conv_w.astype(F32), pad.at[0, SM_A:SM_A + DN_HEADS].set(a_log.astype(F32)),
      pad.at[0, SM_A:SM_A + DN_HEADS].set(dt_bias.astype(F32)), out_gain.reshape(1, -1).astype(F32))


def _memkv_kernel(mem_ref, g_ref, w_ref, k_ref, v_ref):
    mn = _rms(mem_ref[0], g_ref[...]).astype(BF16)
    k_ref[0] = _dot(mn, w_ref[:, :X_W]).astype(k_ref.dtype)
    v_ref[0] = _dot(mn, w_ref[:, X_W:]).astype(v_ref.dtype)


def _memkv(mem, gain, w_kv):
    batch, n_mem, _ = mem.shape
    out = pl.BlockSpec((1, n_mem, X_W), lambda b: (b, 0, 0))
    shape = jax.ShapeDtypeStruct((batch, n_mem, X_W), BF16)
    return pl.pallas_call(
        _memkv_kernel,
        grid=(batch,),
        in_specs=[pl.BlockSpec((1, n_mem, D_MODEL), lambda b: (b, 0, 0)), _const_spec((1, D_MODEL)),
                  _const_spec((D_MODEL, 2 * X_W))],
        out_specs=[out, out],
        out_shape=[shape, shape],
        compiler_params=_params("parallel"),
        name="memkv",
    )(mem, gain.reshape(1, -1), w_kv.astype(BF16))


def _memattn_kernel(q_ref, k_ref, v_ref, o_ref):
    for h in range(X_HEADS):
        hd = slice(h * X_HEAD_DIM, (h + 1) * X_HEAD_DIM)
        s = _dot_nt(q_ref[:, hd], k_ref[0, :, hd])
        e = jnp.exp(s - jnp.max(s, axis=-1, keepdims=True))
        p = e / jnp.sum(e, axis=-1, keepdims=True)
        o_ref[:, hd] = _dot(p.astype(BF16), v_ref[0, :, hd]).astype(o_ref.dtype)


def _memattn(mq, mk, mv, batch, seq):
    nq = seq // TQ_MEM
    n_mem = mk.shape[1]
    kv = pl.BlockSpec((1, n_mem, X_W), lambda b, i: (b, 0, 0))
    row = pl.BlockSpec((TQ_MEM, X_W), lambda b, i: (b * nq + i, 0))
    return pl.pallas_call(
        _memattn_kernel,
        grid=(batch, nq),
        in_specs=[row, kv, kv],
        out_specs=row,
        out_shape=jax.ShapeDtypeStruct((batch * seq, X_W), BF16),
        compiler_params=_params("parallel", "arbitrary"),
        name="memattn",
    )(mq, mk, mv)


def _merge_kernel(x_ref, on_ref, od_ref, om_ref, gpre_ref, wbg_ref, wbn_ref, wbd_ref, wbm_ref, wo_ref, gpost_ref,
                  o_ref):
    x = x_ref[...]
    u = _rms(x, gpre_ref[...]).astype(BF16)
    merged = jnp.zeros(x.shape, F32)
    for i, (br_ref, w_ref) in enumerate(((on_ref, wbn_ref), (od_ref, wbd_ref), (om_ref, wbm_ref))):
        gates = jax.nn.sigmoid(_dot(u, wbg_ref[:, i * D_MODEL:(i + 1) * D_MODEL]))
        merged = merged + gates * _dot(br_ref[...], w_ref[...])
    y = _dot(merged.astype(BF16), wo_ref[...])
    o_ref[...] = x + _rms(y, gpost_ref[...])


def _merge(x, o_nsa, o_dn, o_mem, gpre, wbg, wbn, wbd, wbm, wo, gpost):
    t = x.shape[0]
    row = pl.BlockSpec((TM, D_MODEL), lambda i: (i, 0))
    br = pl.BlockSpec((TM, NSA_Q), lambda i: (i, 0))
    return pl.pallas_call(
        _merge_kernel,
        grid=(t // TM,),
        in_specs=[row, br, br, br, _const_spec((1, D_MODEL)), _const_spec((D_MODEL, N_BRANCH * D_MODEL)),
                  _const_spec((NSA_Q, D_MODEL)), _const_spec((DN_W, D_MODEL)), _const_spec((X_W, D_MODEL)),
                  _const_spec((D_MODEL, D_MODEL)), _const_spec((1, D_MODEL))],
        out_specs=row,
        out_shape=jax.ShapeDtypeStruct((t, D_MODEL), F32),
        compiler_params=_params("parallel"),
        name="merge",
    )(x, o_nsa, o_dn, o_mem, gpre.reshape(1, -1), wbg.astype(BF16), wbn.astype(BF16), wbd.astype(BF16),
      wbm.astype(BF16), wo.astype(BF16), gpost.reshape(1, -1))


def kernel(x, mem, ffn1_pre_norm, ffn1_w_gate, ffn1_w_up, ffn1_w_down, ffn1_post_norm, mix_pre_norm, w_in, cmp_pos_k, cmp_pos_v, cmp_k_w1, cmp_k_w2, cmp_v_w1, cmp_v_w2, rel_bias, dn_conv_w, dn_a_log, dn_dt_bias, dn_out_norm, mem_norm, w_mem_kv, w_branch_nsa, w_branch_dn, w_branch_mem, w_branch_gate, w_out, mix_post_norm, ffn2_pre_norm, ffn2_w_gate, ffn2_w_up, ffn2_w_down, ffn2_post_norm):
    batch, seq, d = x.shape
    xt = x.reshape(batch * seq, d)
    cb, tabs, to_sel_t = _nsa_tables(rel_bias, seq)
    for l in range(ffn1_pre_norm.shape[0]):
        xt = _ffn(xt, ffn1_pre_norm[l], ffn1_w_gate[l], ffn1_w_up[l], ffn1_w_down[l], ffn1_post_norm[l])

        q, kc, vc, kvsw, dn, z, mq, sm = _proj(xt, mix_pre_norm[l], _arrange_w_in(w_in[l]))
        posk, w1k, w2k = _arrange_cmp(cmp_pos_k[l], cmp_k_w1[l], cmp_k_w2[l])
        posv, w1v, w2v = _arrange_cmp(cmp_pos_v[l], cmp_v_w1[l], cmp_v_w2[l])
        k_cmp, v_cmp = _cmp(kc, vc, posk, posv, w1k, w1v, w2k, w2v, batch, seq)
        o_nsa = _nsa(q, kvsw, k_cmp, v_cmp, sm, cb, tabs, to_sel_t, batch, seq)
        o_dn = _dn(dn, z, sm, dn_conv_w[l], dn_a_log[l], dn_dt_bias[l], dn_out_norm[l], batch, seq)
        mk, mv = _memkv(mem, mem_norm[l], w_mem_kv[l])
        o_mem = _memattn(mq, mk, mv, batch, seq)
        xt = _merge(xt, o_nsa, o_dn, o_mem, mix_pre_norm[l], w_branch_gate[l], w_branch_nsa[l], w_branch_dn[l],
                    w_branch_mem[l], w_out[l], mix_post_norm[l])

        xt = _ffn(xt, ffn2_pre_norm[l], ffn2_w_gate[l], ffn2_w_up[l], ffn2_w_down[l], ffn2_post_norm[l])
    return xt.reshape(batch, seq, d)
```

```python
import math

import jax
import jax.numpy as jnp
import numpy as np
from jax import lax
from jax.experimental import pallas as pl
from jax.experimental.pallas import tpu as pltpu

F32 = jnp.float32
BF16 = jnp.bfloat16

D_MODEL = 1024
D_FF = 2816
NORM_EPS = 1e-6
NEG_INF = -1e30
FORCE_SCORE = 1e4

NSA_HEADS = 8
NSA_GROUPS = 2
NSA_HPG = NSA_HEADS // NSA_GROUPS
NSA_HEAD_DIM = 64
CMP_BLOCK = 32
CMP_STRIDE = 16
CMP_HIDDEN = 256
SEL_BLOCK = 64
SEL_SHIFT = 6
SEL_TOPK = 16
WINDOW = 512
REL_BUCKETS = 32
REL_MAX_DIST = 128

DN_HEADS = 4
DN_HEAD_DIM = 128
DN_CONV = 4
X_HEADS = 4
X_HEAD_DIM = 128
N_BRANCH = 3

NSA_Q = NSA_HEADS * NSA_HEAD_DIM
NSA_KV = NSA_GROUPS * NSA_HEAD_DIM
DN_W = DN_HEADS * DN_HEAD_DIM
X_W = X_HEADS * X_HEAD_DIM

LANES = 128
VMEM_LIMIT = 56 * 1024 * 1024

TM = 512
FF_CHUNK = 256
TQ = 256
NSA_RB = 128
NSA_WAVE = 4
DN_CHUNK = 128
DN_LEVELS = 7
DN_GROUP = 4
CONV_HALO = 16
TQ_MEM = 512

SM_GATE = 0
SM_A = N_BRANCH * NSA_HEADS
SM_B = SM_A + DN_HEADS


def _dot(a, b):
    return jnp.dot(a, b, preferred_element_type=F32)


def _dot_nt(a, b):
    return lax.dot_general(a, b, (((1,), (1,)), ((), ())), preferred_element_type=F32)


def _bmm(a, b):
    return lax.dot_general(a, b, (((2,), (1,)), ((0,), (0,))), preferred_element_type=F32)


def _bmm_nt(a, b):
    return lax.dot_general(a, b, (((2,), (2,)), ((0,), (0,))), preferred_element_type=F32)


def _rms(x, gain):
    return x * lax.rsqrt(jnp.mean(x * x, axis=-1, keepdims=True) + NORM_EPS) * gain


def _silu(x):
    return x * jax.nn.sigmoid(x)


def _const_spec(shape):
    zeros = (0,) * len(shape)
    return pl.BlockSpec(shape, lambda *_: zeros, pipeline_mode=pl.Buffered(1))


def _params(*sem):
    return pltpu.CompilerParams(dimension_semantics=sem, vmem_limit_bytes=VMEM_LIMIT)


def _ffn_kernel(x_ref, pre_ref, wg_ref, wu_ref, wd_ref, post_ref, o_ref):
    x = x_ref[...]
    xn = _rms(x, pre_ref[...]).astype(BF16)
    acc = jnp.zeros(x.shape, F32)
    for c in range(D_FF // FF_CHUNK):
        cols = slice(c * FF_CHUNK, (c + 1) * FF_CHUNK)
        gate = _dot(xn, wg_ref[:, cols])
        up = _dot(xn, wu_ref[:, cols])
        acc = acc + _dot((_silu(gate) * up).astype(BF16), wd_ref[cols, :])
    o_ref[...] = x + 0.5 * _rms(acc, post_ref[...])


def _ffn(x, pre, wg, wu, wd, post):
    t = x.shape[0]
    row = pl.BlockSpec((TM, D_MODEL), lambda i: (i, 0))
    return pl.pallas_call(
        _ffn_kernel,
        grid=(t // TM,),
        in_specs=[row, _const_spec((1, D_MODEL)), _const_spec((D_MODEL, D_FF)), _const_spec((D_MODEL, D_FF)),
                  _const_spec((D_FF, D_MODEL)), _const_spec((1, D_MODEL))],
        out_specs=row,
        out_shape=jax.ShapeDtypeStruct((t, D_MODEL), F32),
        compiler_params=_params("parallel"),
        name="ffn",
    )(x, pre.reshape(1, -1), wg.astype(BF16), wu.astype(BF16), wd.astype(BF16), post.reshape(1, -1))


PROJ_SPLITS = (("q", NSA_Q), ("kc", NSA_KV), ("vc", NSA_KV), ("kvsw", 4 * NSA_KV), ("dn", 3 * DN_W),
               ("z", DN_W), ("mq", X_W), ("sm", LANES))
PROJ_WIDTH = sum(w for _, w in PROJ_SPLITS)
LOG2E = math.log2(math.e)
PROJ_SCALES = {"q": NSA_HEAD_DIM ** -0.5 * LOG2E, "mq": X_HEAD_DIM ** -0.5}


def _proj_kernel(x_ref, g_ref, w_ref, *out_refs):
    xn = _rms(x_ref[...], g_ref[...]).astype(BF16)
    start = 0
    for (name, width), ref in zip(PROJ_SPLITS, out_refs):
        for c0 in range(0, width, 512):
            cw = min(512, width - c0)
            y = _dot(xn, w_ref[:, start + c0:start + c0 + cw])
            if name in PROJ_SCALES:
                y = y * PROJ_SCALES[name]
            ref[:, c0:c0 + cw] = y.astype(ref.dtype)
        start += width


def _arrange_w_in(w_in):
    o_kv = NSA_Q
    o_g = o_kv + 6 * NSA_KV
    o_dn = o_g + N_BRANCH * NSA_HEADS
    o_a = o_dn + 3 * DN_W
    o_z = o_a + 2 * DN_HEADS
    o_mq = o_z + DN_W
    kv = w_in[:, o_kv:o_g].reshape(D_MODEL, 6, NSA_GROUPS, NSA_HEAD_DIM)
    kvsw = kv[:, 2:6].transpose(0, 2, 1, 3).reshape(D_MODEL, 4 * NSA_KV)
    small = jnp.concatenate([w_in[:, o_g:o_dn], w_in[:, o_a:o_z],
                             jnp.zeros((D_MODEL, LANES - N_BRANCH * NSA_HEADS - 2 * DN_HEADS), w_in.dtype)], axis=1)
    return jnp.concatenate([w_in[:, :o_kv], kv[:, 0].reshape(D_MODEL, NSA_KV), kv[:, 1].reshape(D_MODEL, NSA_KV),
                            kvsw, w_in[:, o_dn:o_a], w_in[:, o_z:o_mq], w_in[:, o_mq:], small], axis=1).astype(BF16)


def _proj(x, gain, w_arranged):
    t = x.shape[0]
    outs, specs = [], []
    for name, width in PROJ_SPLITS:
        outs.append(jax.ShapeDtypeStruct((t, width), F32 if name == "sm" else BF16))
        specs.append(pl.BlockSpec((TM, width), lambda i: (i, 0)))
    return pl.pallas_call(
        _proj_kernel,
        grid=(t // TM,),
        in_specs=[pl.BlockSpec((TM, D_MODEL), lambda i: (i, 0)), _const_spec((1, D_MODEL)),
                  _const_spec((D_MODEL, PROJ_WIDTH))],
        out_specs=specs,
        out_shape=outs,
        compiler_params=_params("parallel"),
        name="proj",
    )(x, gain.reshape(1, -1), w_arranged)


CMP_ROW = CMP_STRIDE * NSA_KV


def _cmp_kernel(kc_ref, vc_ref, posk_ref, posv_ref, w1k_ref, w1v_ref, w2k_ref, w2v_ref, ko_ref, vo_ref):
    n_rows = kc_ref.shape[1]

    def one(src_ref, pos_ref, w1_ref, w2_ref, out_ref):
        src = src_ref[0].astype(F32)
        first = _dot((src + pos_ref[0:1, :]).astype(BF16), w1_ref[0])
        second = _dot((src + pos_ref[1:2, :]).astype(BF16), w1_ref[1])
        hidden = _silu(first + pltpu.roll(second, n_rows - 1, 0))
        out = _dot(hidden.astype(BF16), w2_ref[...])
        for g in range(NSA_GROUPS):
            out_ref[0, g] = out[:, g * NSA_HEAD_DIM:(g + 1) * NSA_HEAD_DIM].astype(out_ref.dtype)

    one(kc_ref, posk_ref, w1k_ref, w2k_ref, ko_ref)
    one(vc_ref, posv_ref, w1v_ref, w2v_ref, vo_ref)


def _arrange_cmp(pos, w1, w2):
    hid2 = NSA_GROUPS * CMP_HIDDEN
    w1r = w1.reshape(2, CMP_STRIDE, NSA_HEAD_DIM, CMP_HIDDEN)
    eye = jnp.eye(NSA_GROUPS, dtype=w1.dtype)
    w1g = jnp.einsum("hidk,ge->higdek", w1r, eye).reshape(2, CMP_ROW, hid2).astype(BF16)
    w2g = jnp.einsum("kd,ge->gked", w2, eye).reshape(hid2, NSA_KV).astype(BF16)
    posr = jnp.broadcast_to(pos.reshape(2, CMP_STRIDE, 1, NSA_HEAD_DIM),
                            (2, CMP_STRIDE, NSA_GROUPS, NSA_HEAD_DIM)).reshape(2, CMP_ROW).astype(F32)
    return posr, w1g, w2g


def _cmp(kc, vc, posk, posv, w1k, w1v, w2k, w2v, batch, seq):
    n_rows = seq // CMP_STRIDE
    src = pl.BlockSpec((1, n_rows, CMP_ROW), lambda b: (b, 0, 0))
    out = pl.BlockSpec((1, NSA_GROUPS, n_rows, NSA_HEAD_DIM), lambda b: (b, 0, 0, 0))
    out_shape = jax.ShapeDtypeStruct((batch, NSA_GROUPS, n_rows, NSA_HEAD_DIM), BF16)
    hid2 = NSA_GROUPS * CMP_HIDDEN
    return pl.pallas_call(
        _cmp_kernel,
        grid=(batch,),
        in_specs=[src, src, _const_spec((2, CMP_ROW)), _const_spec((2, CMP_ROW)),
                  _const_spec((2, CMP_ROW, hid2)), _const_spec((2, CMP_ROW, hid2)),
                  _const_spec((hid2, NSA_KV)), _const_spec((hid2, NSA_KV))],
        out_specs=[out, out],
        out_shape=[out_shape, out_shape],
        compiler_params=_params("parallel"),
        name="cmp",
    )(kc.reshape(batch, n_rows, CMP_ROW), vc.reshape(batch, n_rows, CMP_ROW), posk, posv, w1k, w1v, w2k, w2v)


def _rel_bucket_table(n):
    exact = REL_BUCKETS // 2
    dist = np.arange(n)
    d = np.maximum(dist, 1).astype(np.float32)
    log_bucket = exact + (np.log(d / np.float32(exact)) / np.float32(math.log(REL_MAX_DIST / exact))
                          * np.float32(REL_BUCKETS - exact)).astype(np.int32)
    return np.where(dist < exact, dist, np.minimum(log_bucket, REL_BUCKETS - 1))


def _toeplitz(w, n_rows, n_cols):
    p = w.shape[-1]
    flat = jnp.tile(w, (1,) * (w.ndim - 1) + (n_rows,))[..., :n_rows * (p - 1)]
    return flat.reshape(w.shape[:-1] + (n_rows, p - 1))[..., :n_cols]


def _nsa_tables(rel_bias, seq):
    bucket = _rel_bucket_table(seq + 2 * TQ)
    n_cmp_pad = seq // CMP_STRIDE
    bias = rel_bias.astype(F32).T * LOG2E
    neg = jnp.full((NSA_HEADS, 1), NEG_INF, F32)

    def by_distance(values, dist):
        cls = np.where(dist >= 0, bucket[np.maximum(dist, 0)], REL_BUCKETS)
        one_hot = jnp.asarray(cls.reshape(-1)[None, :] == np.arange(REL_BUCKETS + 1)[:, None], F32)
        picked = jnp.dot(jnp.concatenate([values, neg], axis=1), one_hot, precision=lax.Precision.HIGHEST)
        return picked.reshape((NSA_HEADS,) + dist.shape)

    p = 2 * n_cmp_pad
    k = np.arange(p)[None, :]
    a_minus_c = np.where(k < n_cmp_pad, -k, p - k)
    dist = CMP_STRIDE * a_minus_c + np.arange(CMP_STRIDE)[:, None] - (CMP_BLOCK - 1)
    cb = _toeplitz(by_distance(bias, dist), n_cmp_pad, n_cmp_pad).transpose(0, 2, 1, 3)
    cb = cb.reshape(NSA_GROUPS, NSA_HPG, seq // TQ, TQ, n_cmp_pad).transpose(0, 2, 1, 3, 4)
    cb = cb.reshape(NSA_GROUPS, seq // TQ, NSA_HPG * TQ, n_cmp_pad)
    shifted = bias - bias[:, REL_BUCKETS - 1:]
    k = np.arange(2 * TQ)
    i_minus_j = np.where(k < TQ, -k, 2 * TQ - k)
    t0 = _toeplitz(by_distance(shifted, i_minus_j), TQ, TQ)
    t1 = _toeplitz(by_distance(shifted, TQ + i_minus_j), TQ, TQ)
    w2 = jnp.where(lax.broadcasted_iota(jnp.int32, (TQ, TQ), 1) > lax.broadcasted_iota(jnp.int32, (TQ, TQ), 0),
                   0.0, NEG_INF).astype(F32)
    tabs = jnp.stack([t0, t1, jnp.broadcast_to(w2, t0.shape)], axis=1)
    tabs = tabs.reshape(NSA_GROUPS, NSA_HPG, 3, TQ, TQ).transpose(0, 2, 1, 3, 4)
    tabs = tabs.reshape(NSA_GROUPS, 3, NSA_HPG * TQ, TQ)
    c0 = np.arange(n_cmp_pad)[None, :] * CMP_STRIDE
    s0 = np.arange(LANES)[:, None] * SEL_BLOCK
    overlap = np.maximum(np.minimum(c0 + CMP_BLOCK, s0 + SEL_BLOCK) - np.maximum(c0, s0), 0)
    valid = (np.arange(LANES)[:, None] < seq // SEL_BLOCK) & (np.arange(n_cmp_pad)[None, :] < n_cmp_pad - 1)
    to_sel_t = jnp.asarray(np.where(valid, overlap / CMP_BLOCK, 0.0), BF16)
    return cb, tabs, to_sel_t


def _nsa_kernel(q_ref, kv_ref, kc_ref, vc_ref, sm_ref, cb_ref, tab_ref, tosel_ref, o_ref):
    qi = pl.program_id(1)
    n_sel_blocks = kv_ref.shape[0] // SEL_BLOCK
    halves = TQ // NSA_RB
    n_blocks = NSA_HPG * halves
    sm = sm_ref[...]
    ones_cols = jnp.ones((TQ, NSA_HEAD_DIM), BF16)

    def q_rows(r):
        half = r % halves
        return slice(half * NSA_RB, (half + 1) * NSA_RB)

    def tab_rows(r):
        return slice(r * NSA_RB, (r + 1) * NSA_RB)

    def flash_step(streams):
        chains = []
        for si, (q_blocks, state, k_t, v_t, add_fn) in enumerate(streams):
            v_ext = jnp.concatenate([v_t, ones_cols], axis=1)
            chains += [(si, r, q_blocks[r], state[r], k_t, v_ext, add_fn) for r in range(n_blocks)]
        waves = [chains[i:i + NSA_WAVE] for i in range(0, len(chains), NSA_WAVE)]
        s, m_new, p, new = {}, {}, {}, {}
        for step in range(len(waves) + 2):
            for si, r, q, _, k_t, _, add_fn in (waves[step] if step < len(waves) else ()):
                s[si, r] = _dot_nt(q, k_t) + add_fn(r)
            for si, r, _, (m, _), _, _, _ in (waves[step - 1] if 0 < step <= len(waves) else ()):
                m_new[si, r] = jnp.maximum(m, jnp.max(s[si, r], axis=-1, keepdims=True))
                p[si, r] = jnp.exp2(s[si, r] - m_new[si, r]).astype(BF16)
            for si, r, _, (m, acc), _, v_ext, _ in (waves[step - 2] if step >= 2 else ()):
                new[si, r] = (m_new[si, r], jnp.exp2(m - m_new[si, r]) * acc + _dot(p[si, r], v_ext))
        return [[new[si, r] for r in range(n_blocks)] for si in range(len(streams))]

    def normalized(state):
        acc = jnp.concatenate([acc for _, acc in state], axis=0)
        return acc[:, :NSA_HEAD_DIM] / acc[:, NSA_HEAD_DIM:NSA_HEAD_DIM + 1]

    groups = range(NSA_GROUPS)

    def head_cols(g, r):
        c0 = (g * NSA_HPG + r // halves) * NSA_HEAD_DIM
        return slice(c0, c0 + NSA_HEAD_DIM)

    q_blocks = [[q_ref[q_rows(r), head_cols(g, r)] for r in range(n_blocks)] for g in groups]

    def kv_tile(g, kt, which):
        c0 = (g * 4 + which) * NSA_HEAD_DIM
        return kv_ref[pl.ds(pl.multiple_of(kt * TQ, TQ), TQ), c0:c0 + NSA_HEAD_DIM]

    def compressed(g):
        rows = NSA_HPG * TQ
        tok = qi * TQ + (lax.broadcasted_iota(jnp.int32, (rows, 1), 0) & (TQ - 1))
        s = _dot_nt(jnp.concatenate(q_blocks[g], axis=0), kc_ref[0, g]) + cb_ref[g, 0]
        e = jnp.exp2(s - jnp.max(s, axis=-1, keepdims=True))
        p = e / jnp.sum(e, axis=-1, keepdims=True) * jnp.where(tok >= CMP_BLOCK - 1, 1.0, 0.0)
        out = _dot(p.astype(BF16), vc_ref[0, g])
        p_sum = p[0:TQ]
        for j in range(1, NSA_HPG):
            p_sum = p_sum + p[j * TQ:(j + 1) * TQ]
        return out, p_sum

    def dropped_blocks(p_sum):
        p_hi = p_sum.astype(BF16)
        p_lo = (p_sum - p_hi.astype(F32)).astype(BF16)
        imp_t = _dot_nt(tosel_ref[...], p_hi) + _dot_nt(tosel_ref[...], p_lo)
        imp_t = imp_t[0:n_sel_blocks]
        blk = lax.broadcasted_iota(jnp.int32, (n_sel_blocks, TQ), 0)
        cur = jnp.right_shift(qi * TQ + lax.broadcasted_iota(jnp.int32, (n_sel_blocks, TQ), 1), SEL_SHIFT)
        score = jnp.where(blk <= cur, imp_t, -FORCE_SCORE)
        for forced_blk in (0, cur, cur - 1):
            score = jnp.where(blk == forced_blk, FORCE_SCORE, score)
        rank = jnp.zeros((n_sel_blocks, TQ), F32)
        for m_blk in range(n_sel_blocks):
            other = score[m_blk:m_blk + 1, :]
            tie = jnp.where(blk > m_blk, 1.0, 0.0)
            rank = rank + jnp.where(other > score, 1.0, jnp.where(other == score, tie, 0.0))
        drop_t = jnp.where(rank < SEL_TOPK, jnp.where(score > -1.0, 0.0, 1.0), 1.0)
        drop_t = jnp.concatenate([drop_t, jnp.ones((LANES - n_sel_blocks, TQ), F32)], axis=0)
        return drop_t.T.astype(BF16)

    def sel_mask(drop, kt):
        key = kt * TQ + lax.broadcasted_iota(jnp.int32, (LANES, TQ), 1)
        hit = lax.broadcasted_iota(jnp.int32, (LANES, TQ), 0) == jnp.right_shift(key, SEL_SHIFT)
        return _dot(drop, jnp.where(hit, NEG_INF, 0.0).astype(BF16))

    def sel_stream(st, g, kt, tab):
        mask = sel_mask(drop[g], kt)
        if tab is None:
            add_fn = lambda r: mask[q_rows(r)]
        else:
            add_fn = lambda r: tab_ref[g, tab, tab_rows(r), :] + mask[q_rows(r)]
        return q_blocks[g], st, kv_tile(g, kt, 0), kv_tile(g, kt, 1), add_fn

    def win_stream(st, g, kt, tab):
        return q_blocks[g], st, kv_tile(g, kt, 2), kv_tile(g, kt, 3), lambda r: tab_ref[g, tab, tab_rows(r), :]

    init = [(jnp.full((NSA_RB, 1), NEG_INF, F32), jnp.zeros((NSA_RB, 2 * NSA_HEAD_DIM), F32))
            for _ in range(n_blocks)]

    win = flash_step([win_stream(init, g, qi, 0) for g in groups])
    cmp_out = [compressed(g) for g in groups]
    drop = [dropped_blocks(cmp_out[g][1]) for g in groups]
    sel = flash_step([sel_stream(init, g, qi, 0) for g in groups])

    def previous_tile(state):
        sel, win = state
        new = flash_step([sel_stream(sel[g], g, qi - 1, 1) for g in groups]
                         + [win_stream(win[g], g, qi - 1, 1) for g in groups])
        return new[:NSA_GROUPS], new[NSA_GROUPS:]

    def two_back(state):
        sel, win = state
        new = flash_step([sel_stream(sel[g], g, qi - 2, None) for g in groups]
                         + [win_stream(win[g], g, qi - 2, 2) for g in groups])
        return new[:NSA_GROUPS], new[NSA_GROUPS:]

    sel, win = lax.cond(qi >= 1, previous_tile, lambda state: state, (sel, win))
    sel, win = lax.cond(qi >= 2, two_back, lambda state: state, (sel, win))
    sel = lax.fori_loop(0, jnp.maximum(qi - 2, 0),
                        lambda kt, sel: flash_step([sel_stream(sel[g], g, kt, None) for g in groups]), sel)

    gates_all = jax.nn.sigmoid(sm)
    for g in groups:
        def gate(branch, g=g):
            c0 = SM_GATE + branch * NSA_HEADS + g * NSA_HPG
            return jnp.concatenate([gates_all[:, c0 + j:c0 + j + 1] for j in range(NSA_HPG)], axis=0)

        o = (gate(0) * cmp_out[g][0] + gate(1) * normalized(sel[g]) + gate(2) * normalized(win[g]))
        o = jnp.concatenate([o[j * TQ:(j + 1) * TQ] for j in range(NSA_HPG)], axis=1)
        o_ref[:, g * NSA_HPG * NSA_HEAD_DIM:(g + 1) * NSA_HPG * NSA_HEAD_DIM] = o.astype(o_ref.dtype)


def _nsa(q, kvsw, k_cmp, v_cmp, sm, cb, tabs, to_sel_t, batch, seq):
    assert WINDOW == 2 * TQ and seq % TQ == 0
    nq = seq // TQ
    n_cmp_pad = seq // CMP_STRIDE
    rows = NSA_HPG * TQ
    cmp_spec = pl.BlockSpec((1, NSA_GROUPS, n_cmp_pad, NSA_HEAD_DIM), lambda b, i: (b, 0, 0, 0))
    return pl.pallas_call(
        _nsa_kernel,
        grid=(batch, nq),
        in_specs=[
            pl.BlockSpec((TQ, NSA_Q), lambda b, i: (b * nq + i, 0)),
            pl.BlockSpec((seq, 4 * NSA_KV), lambda b, i: (b, 0)),
            cmp_spec, cmp_spec,
            pl.BlockSpec((TQ, LANES), lambda b, i: (b * nq + i, 0)),
            pl.BlockSpec((NSA_GROUPS, 1, rows, n_cmp_pad), lambda b, i: (0, i, 0, 0)),
            _const_spec((NSA_GROUPS, 3, rows, TQ)),
            _const_spec((LANES, n_cmp_pad)),
        ],
        out_specs=pl.BlockSpec((TQ, NSA_Q), lambda b, i: (b * nq + i, 0)),
        out_shape=jax.ShapeDtypeStruct((batch * seq, NSA_Q), BF16),
        compiler_params=_params("parallel", "arbitrary"),
        name="nsa",
    )(q, kvsw, k_cmp, v_cmp, sm, cb, tabs, to_sel_t)


def _softplus(x):
    return jnp.maximum(x, 0.0) + jnp.log(1.0 + jnp.exp(-jnp.abs(x)))


def _l2_normalize(x):
    return x * lax.rsqrt(jnp.sum(x * x, axis=-1, keepdims=True) + NORM_EPS)


def _dn_kernel(dn_ref, z_ref, sm_ref, cw_ref, alog_ref, dtb_ref, gn_ref, o_ref, state_ref):
    c = DN_CHUNK
    n_chunks = dn_ref.shape[0] // c
    ii = lax.broadcasted_iota(jnp.int32, (c, c), 0)
    jj = lax.broadcasted_iota(jnp.int32, (c, c), 1)
    causal = ii >= jj
    strict = ii > jj
    split = ii ^ jj
    tri_ones = jnp.where(causal, 1.0, 0.0).astype(BF16)
    eye = jnp.where(ii == jj, 1.0, 0.0)
    state_ref[...] = jnp.zeros(state_ref.shape, F32)

    span = c * DN_GROUP

    def group(gi, carry):
        r0 = pl.multiple_of(gi * span, span)
        rows = pl.ds(r0, span)
        x = dn_ref[rows, :].astype(F32)
        halo_rows = pl.ds(pl.multiple_of(jnp.maximum(r0 - CONV_HALO, 0), CONV_HALO), CONV_HALO)
        halo = dn_ref[halo_rows, :].astype(F32) * jnp.where(gi > 0, 1.0, 0.0)
        xc = jnp.concatenate([halo, x], axis=0)
        y = x * cw_ref[DN_CONV - 1:DN_CONV, :]
        for k in range(1, DN_CONV):
            y = y + pltpu.roll(xc, k, 0)[CONV_HALO:, :] * cw_ref[DN_CONV - 1 - k:DN_CONV - k, :]
        y = _silu(y)

        smc = sm_ref[rows, :]
        log_decay = -jnp.exp(alog_ref[...]) * _softplus(smc + dtb_ref[...])
        beta = jax.nn.sigmoid(smc)
        ld1 = log_decay.astype(BF16)
        rest = log_decay - ld1.astype(F32)
        ld2 = rest.astype(BF16)
        ld3 = (rest - ld2.astype(F32)).astype(BF16)

        q_l, k_l, v_l, gcol_l, grow_l, gend_l, bcol_l = [], [], [], [], [], [], []
        for k in range(DN_GROUP):
            cr = slice(k * c, (k + 1) * c)
            gc = _dot(tri_ones, ld1[cr]) + _dot(tri_ones, ld2[cr]) + _dot(tri_ones, ld3[cr])
            gc_t = gc.T
            for h in range(DN_HEADS):
                q_l.append(_l2_normalize(y[cr, h * DN_HEAD_DIM:(h + 1) * DN_HEAD_DIM]) * DN_HEAD_DIM ** -0.5)
                k_l.append(_l2_normalize(y[cr, DN_W + h * DN_HEAD_DIM:DN_W + (h + 1) * DN_HEAD_DIM]))
                v_l.append(y[cr, 2 * DN_W + h * DN_HEAD_DIM:2 * DN_W + (h + 1) * DN_HEAD_DIM])
                gcol_l.append(gc[:, SM_A + h:SM_A + h + 1])
                grow_l.append(gc_t[SM_A + h:SM_A + h + 1, :])
                gend_l.append(gc[c - 1:c, SM_A + h:SM_A + h + 1])
                bcol_l.append(beta[cr, SM_B + h:SM_B + h + 1])
        qs, ks, vs = jnp.stack(q_l), jnp.stack(k_l), jnp.stack(v_l)
        g_col, g_row, g_end, b_col = jnp.stack(gcol_l), jnp.stack(grow_l), jnp.stack(gend_l), jnp.stack(bcol_l)

        decay = jnp.exp(jnp.where(causal, g_col - g_row, NEG_INF))
        qb = qs.astype(BF16)
        kb = ks.astype(BF16)
        n_mat = jnp.where(strict, b_col * _bmm_nt(kb, kb) * decay, 0.0)
        inv = eye - jnp.where(split == 1, n_mat, 0.0)
        for lvl in range(1, DN_LEVELS):
            n_lvl = jnp.where(jnp.right_shift(split, lvl) == 1, n_mat, 0.0).astype(BF16)
            inv_b = inv.astype(BF16)
            inv = inv - _bmm(inv_b, _bmm(n_lvl, inv_b).astype(BF16))
        e_col = jnp.exp(g_col)
        rhs = jnp.concatenate([vs * b_col, ks * (b_col * e_col)], axis=2).astype(BF16)
        sol = _bmm(inv.astype(BF16), rhs)
        u = sol[:, :, :DN_HEAD_DIM]
        w = sol[:, :, DN_HEAD_DIM:].astype(BF16)
        qk = (_bmm_nt(qb, kb) * decay).astype(BF16)
        q_dec = (qs * e_col).astype(BF16)
        k_dec = ks * jnp.exp(g_end - g_col)
        k_dec_t = jnp.stack([k_dec[n].T for n in range(DN_GROUP * DN_HEADS)]).astype(BF16)
        g_chunk = jnp.exp(g_end)

        state = state_ref[...]
        for k in range(DN_GROUP):
            hs = slice(k * DN_HEADS, (k + 1) * DN_HEADS)
            state_b = state.astype(BF16)
            v_new = (u[hs] - _bmm(w[hs], state_b)).astype(BF16)
            o = _bmm(q_dec[hs], state_b) + _bmm(qk[hs], v_new)
            state = state * g_chunk[hs] + _bmm(k_dec_t[hs], v_new)
            out_rows = pl.ds(r0 + k * c, c)
            for h in range(DN_HEADS):
                hd = slice(h * DN_HEAD_DIM, (h + 1) * DN_HEAD_DIM)
                zh = z_ref[out_rows, hd].astype(F32)
                o_ref[out_rows, hd] = (_rms(o[h], gn_ref[...]) * _silu(zh)).astype(o_ref.dtype)
        state_ref[...] = state
        return carry

    lax.fori_loop(0, n_chunks // DN_GROUP, group, 0)


def _dn(dn, z, sm, conv_w, a_log, dt_bias, out_gain, batch, seq):
    assert DN_CHUNK == 1 << DN_LEVELS and seq % (DN_CHUNK * DN_GROUP) == 0
    pad = jnp.zeros((1, LANES), F32)
    return pl.pallas_call(
        _dn_kernel,
        grid=(batch,),
        in_specs=[pl.BlockSpec((seq, 3 * DN_W), lambda b: (b, 0)), pl.BlockSpec((seq, DN_W), lambda b: (b, 0)),
                  pl.BlockSpec((seq, LANES), lambda b: (b, 0)), _const_spec((DN_CONV, 3 * DN_W)),
                  _const_spec((1, LANES)), _const_spec((1, LANES)), _const_spec((1, DN_HEAD_DIM))],
        out_specs=pl.BlockSpec((seq, DN_W), lambda b: (b, 0)),
        out_shape=jax.ShapeDtypeStruct((batch * seq, DN_W), BF16),
        scratch_shapes=[pltpu.VMEM((DN_HEADS, DN_HEAD_DIM, DN_HEAD_DIM), F32)],
        compiler_params=_params("parallel"),
        name="dn",
    )(dn, z, sm, conv_w.astype(F32), pad.at[0, SM_A:SM_A + DN_HEADS].set(a_log.astype(F32)),
      pad.at[0, SM_A:SM_A + DN_HEADS].set(dt_bias.astype(F32)), out_gain.reshape(1, -1).astype(F32))


def _memkv_kernel(mem_ref, g_ref, w_ref, k_ref, v_ref):
    mn = _rms(mem_ref[0], g_ref[...]).astype(BF16)
    k_ref[0] = _dot(mn, w_ref[:, :X_W]).astype(k_ref.dtype)
    v_ref[0] = _dot(mn, w_ref[:, X_W:]).astype(v_ref.dtype)


def _memkv(mem, gain, w_kv):
    batch, n_mem, _ = mem.shape
    out = pl.BlockSpec((1, n_mem, X_W), lambda b: (b, 0, 0))
    shape = jax.ShapeDtypeStruct((batch, n_mem, X_W), BF16)
    return pl.pallas_call(
        _memkv_kernel,
        grid=(batch,),
        in_specs=[pl.BlockSpec((1, n_mem, D_MODEL), lambda b: (b, 0, 0)), _const_spec((1, D_MODEL)),
                  _const_spec((D_MODEL, 2 * X_W))],
        out_specs=[out, out],
        out_shape=[shape, shape],
        compiler_params=_params("parallel"),
        name="memkv",
    )(mem, gain.reshape(1, -1), w_kv.astype(BF16))


def _memattn_kernel(q_ref, k_ref, v_ref, o_ref):
    for h in range(X_HEADS):
        hd = slice(h * X_HEAD_DIM, (h + 1) * X_HEAD_DIM)
        s = _dot_nt(q_ref[:, hd], k_ref[0, :, hd])
        e = jnp.exp(s - jnp.max(s, axis=-1, keepdims=True))
        p = e / jnp.sum(e, axis=-1, keepdims=True)
        o_ref[:, hd] = _dot(p.astype(BF16), v_ref[0, :, hd]).astype(o_ref.dtype)


def _memattn(mq, mk, mv, batch, seq):
    nq = seq // TQ_MEM
    n_mem = mk.shape[1]
    kv = pl.BlockSpec((1, n_mem, X_W), lambda b, i: (b, 0, 0))
    row = pl.BlockSpec((TQ_MEM, X_W), lambda b, i: (b * nq + i, 0))
    return pl.pallas_call(
        _memattn_kernel,
        grid=(batch, nq),
        in_specs=[row, kv, kv],
        out_specs=row,
        out_shape=jax.ShapeDtypeStruct((batch * seq, X_W), BF16),
        compiler_params=_params("parallel", "arbitrary"),
        name="memattn",
    )(mq, mk, mv)


def _merge_kernel(x_ref, on_ref, od_ref, om_ref, gpre_ref, wbg_ref, wbn_ref, wbd_ref, wbm_ref, wo_ref, gpost_ref,
                  o_ref):
    x = x_ref[...]
    u = _rms(x, gpre_ref[...]).astype(BF16)
    merged = jnp.zeros(x.shape, F32)
    for i, (br_ref, w_ref) in enumerate(((on_ref, wbn_ref), (od_ref, wbd_ref), (om_ref, wbm_ref))):
        gates = jax.nn.sigmoid(_dot(u, wbg_ref[:, i * D_MODEL:(i + 1) * D_MODEL]))
        merged = merged + gates * _dot(br_ref[...], w_ref[...])
    y = _dot(merged.astype(BF16), wo_ref[...])
    o_ref[...] = x + _rms(y, gpost_ref[...])


def _merge(x, o_nsa, o_dn, o_mem, gpre, wbg, wbn, wbd, wbm, wo, gpost):
    t = x.shape[0]
    row = pl.BlockSpec((TM, D_MODEL), lambda i: (i, 0))
    br = pl.BlockSpec((TM, NSA_Q), lambda i: (i, 0))
    return pl.pallas_call(
        _merge_kernel,
        grid=(t // TM,),
        in_specs=[row, br, br, br, _const_spec((1, D_MODEL)), _const_spec((D_MODEL, N_BRANCH * D_MODEL)),
                  _const_spec((NSA_Q, D_MODEL)), _const_spec((DN_W, D_MODEL)), _const_spec((X_W, D_MODEL)),
                  _const_spec((D_MODEL, D_MODEL)), _const_spec((1, D_MODEL))],
        out_specs=row,
        out_shape=jax.ShapeDtypeStruct((t, D_MODEL), F32),
        compiler_params=_params("parallel"),
        name="merge",
    )(x, o_nsa, o_dn, o_mem, gpre.reshape(1, -1), wbg.astype(BF16), wbn.astype(BF16), wbd.astype(BF16),
      wbm.astype(BF16), wo.astype(BF16), gpost.reshape(1, -1))


def kernel(x, mem, ffn1_pre_norm, ffn1_w_gate, ffn1_w_up, ffn1_w_down, ffn1_post_norm, mix_pre_norm, w_in, cmp_pos_k, cmp_pos_v, cmp_k_w1, cmp_k_w2, cmp_v_w1, cmp_v_w2, rel_bias, dn_conv_w, dn_a_log, dn_dt_bias, dn_out_norm, mem_norm, w_mem_kv, w_branch_nsa, w_branch_dn, w_branch_mem, w_branch_gate, w_out, mix_post_norm, ffn2_pre_norm, ffn2_w_gate, ffn2_w_up, ffn2_w_down, ffn2_post_norm):
    batch, seq, d = x.shape
    xt = x.reshape(batch * seq, d)
    cb, tabs, to_sel_t = _nsa_tables(rel_bias, seq)
    for l in range(ffn1_pre_norm.shape[0]):
        xt = _ffn(xt, ffn1_pre_norm[l], ffn1_w_gate[l], ffn1_w_up[l], ffn1_w_down[l], ffn1_post_norm[l])

        q, kc, vc, kvsw, dn, z, mq, sm = _proj(xt, mix_pre_norm[l], _arrange_w_in(w_in[l]))
        posk, w1k, w2k = _arrange_cmp(cmp_pos_k[l], cmp_k_w1[l], cmp_k_w2[l])
        posv, w1v, w2v = _arrange_cmp(cmp_pos_v[l], cmp_v_w1[l], cmp_v_w2[l])
        k_cmp, v_cmp = _cmp(kc, vc, posk, posv, w1k, w1v, w2k, w2v, batch, seq)
        o_nsa = _nsa(q, kvsw, k_cmp, v_cmp, sm, cb, tabs, to_sel_t, batch, seq)
        o_dn = _dn(dn, z, sm, dn_conv_w[l], dn_a_log[l], dn_dt_bias[l], dn_out_norm[l], batch, seq)
        mk, mv = _memkv(mem, mem_norm[l], w_mem_kv[l])
        o_mem = _memattn(mq, mk, mv, batch, seq)
        xt = _merge(xt, o_nsa, o_dn, o_mem, mix_pre_norm[l], w_branch_gate[l], w_branch_nsa[l], w_branch_dn[l],
                    w_branch_mem[l], w_out[l], mix_post_norm[l])

        xt = _ffn(xt, ffn2_pre_norm[l], ffn2_w_gate[l], ffn2_w_up[l], ffn2_w_down[l], ffn2_post_norm[l])
    return xt.reshape(batch, seq, d)
```

```python
import math

import jax
import jax.numpy as jnp
import numpy as np
from jax import lax
from jax.experimental import pallas as pl
from jax.experimental.pallas import tpu as pltpu

F32 = jnp.float32
BF16 = jnp.bfloat16

D_MODEL = 1024
D_FF = 2816
NORM_EPS = 1e-6
NEG_INF = -1e30
FORCE_SCORE = 1e4

NSA_HEADS = 8
NSA_GROUPS = 2
NSA_HPG = NSA_HEADS // NSA_GROUPS
NSA_HEAD_DIM = 64
CMP_BLOCK = 32
CMP_STRIDE = 16
CMP_HIDDEN = 256
SEL_BLOCK = 64
SEL_SHIFT = 6
SEL_TOPK = 16
WINDOW = 512
REL_BUCKETS = 32
REL_MAX_DIST = 128

DN_HEADS = 4
DN_HEAD_DIM = 128
DN_CONV = 4
X_HEADS = 4
X_HEAD_DIM = 128
N_BRANCH = 3

NSA_Q = NSA_HEADS * NSA_HEAD_DIM
NSA_KV = NSA_GROUPS * NSA_HEAD_DIM
DN_W = DN_HEADS * DN_HEAD_DIM
X_W = X_HEADS * X_HEAD_DIM

LANES = 128
VMEM_LIMIT = 56 * 1024 * 1024

TM = 512
FF_CHUNK = 256
TQ = 256
NSA_RB = 128
NSA_WAVE = 4
DN_CHUNK = 128
DN_LEVELS = 7
DN_GROUP = 4
CONV_HALO = 16
TQ_MEM = 512

SM_GATE = 0
SM_A = N_BRANCH * NSA_HEADS
SM_B = SM_A + DN_HEADS


def _dot(a, b):
    return jnp.dot(a, b, preferred_element_type=F32)


def _dot_nt(a, b):
    return lax.dot_general(a, b, (((1,), (1,)), ((), ())), preferred_element_type=F32)


def _bmm(a, b):
    return lax.dot_general(a, b, (((2,), (1,)), ((0,), (0,))), preferred_element_type=F32)


def _bmm_nt(a, b):
    return lax.dot_general(a, b, (((2,), (2,)), ((0,), (0,))), preferred_element_type=F32)


def _rms(x, gain):
    return x * lax.rsqrt(jnp.mean(x * x, axis=-1, keepdims=True) + NORM_EPS) * gain


def _silu(x):
    return x * jax.nn.sigmoid(x)


def _const_spec(shape):
    zeros = (0,) * len(shape)
    return pl.BlockSpec(shape, lambda *_: zeros, pipeline_mode=pl.Buffered(1))


def _params(*sem):
    return pltpu.CompilerParams(dimension_semantics=sem, vmem_limit_bytes=VMEM_LIMIT)


def _ffn_kernel(x_ref, pre_ref, wg_ref, wu_ref, wd_ref, post_ref, o_ref):
    x = x_ref[...]
    xn = _rms(x, pre_ref[...]).astype(BF16)
    acc = jnp.zeros(x.shape, F32)
    for c in range(D_FF // FF_CHUNK):
        cols = slice(c * FF_CHUNK, (c + 1) * FF_CHUNK)
        gate = _dot(xn, wg_ref[:, cols])
        up = _dot(xn, wu_ref[:, cols])
        acc = acc + _dot((_silu(gate) * up).astype(BF16), wd_ref[cols, :])
    o_ref[...] = x + 0.5 * _rms(acc, post_ref[...])


def _ffn(x, pre, wg, wu, wd, post):
    t = x.shape[0]
    row = pl.BlockSpec((TM, D_MODEL), lambda i: (i, 0))
    return pl.pallas_call(
        _ffn_kernel,
        grid=(t // TM,),
        in_specs=[row, _const_spec((1, D_MODEL)), _const_spec((D_MODEL, D_FF)), _const_spec((D_MODEL, D_FF)),
                  _const_spec((D_FF, D_MODEL)), _const_spec((1, D_MODEL))],
        out_specs=row,
        out_shape=jax.ShapeDtypeStruct((t, D_MODEL), F32),
        compiler_params=_params("parallel"),
        name="ffn",
    )(x, pre.reshape(1, -1), wg.astype(BF16), wu.astype(BF16), wd.astype(BF16), post.reshape(1, -1))


PROJ_SPLITS = (("q", NSA_Q), ("kc", NSA_KV), ("vc", NSA_KV), ("kvsw", 4 * NSA_KV), ("dn", 3 * DN_W),
               ("z", DN_W), ("mq", X_W), ("sm", LANES))
PROJ_WIDTH = sum(w for _, w in PROJ_SPLITS)
LOG2E = math.log2(math.e)
PROJ_SCALES = {"q": NSA_HEAD_DIM ** -0.5 * LOG2E, "mq": X_HEAD_DIM ** -0.5}


def _proj_kernel(x_ref, g_ref, w_ref, *out_refs):
    xn = _rms(x_ref[...], g_ref[...]).astype(BF16)
    start = 0
    for (name, width), ref in zip(PROJ_SPLITS, out_refs):
        for c0 in range(0, width, 512):
            cw = min(512, width - c0)
            y = _dot(xn, w_ref[:, start + c0:start + c0 + cw])
            if name in PROJ_SCALES:
                y = y * PROJ_SCALES[name]
            ref[:, c0:c0 + cw] = y.astype(ref.dtype)
        start += width


def _arrange_w_in(w_in):
    o_kv = NSA_Q
    o_g = o_kv + 6 * NSA_KV
    o_dn = o_g + N_BRANCH * NSA_HEADS
    o_a = o_dn + 3 * DN_W
    o_z = o_a + 2 * DN_HEADS
    o_mq = o_z + DN_W
    kv = w_in[:, o_kv:o_g].reshape(D_MODEL, 6, NSA_GROUPS, NSA_HEAD_DIM)
    kvsw = kv[:, 2:6].transpose(0, 2, 1, 3).reshape(D_MODEL, 4 * NSA_KV)
    small = jnp.concatenate([w_in[:, o_g:o_dn], w_in[:, o_a:o_z],
                             jnp.zeros((D_MODEL, LANES - N_BRANCH * NSA_HEADS - 2 * DN_HEADS), w_in.dtype)], axis=1)
    return jnp.concatenate([w_in[:, :o_kv], kv[:, 0].reshape(D_MODEL, NSA_KV), kv[:, 1].reshape(D_MODEL, NSA_KV),
                            kvsw, w_in[:, o_dn:o_a], w_in[:, o_z:o_mq], w_in[:, o_mq:], small], axis=1).astype(BF16)


def _proj(x, gain, w_arranged):
    t = x.shape[0]
    outs, specs = [], []
    for name, width in PROJ_SPLITS:
        outs.append(jax.ShapeDtypeStruct((t, width), F32 if name == "sm" else BF16))
        specs.append(pl.BlockSpec((TM, width), lambda i: (i, 0)))
    return pl.pallas_call(
        _proj_kernel,
        grid=(t // TM,),
        in_specs=[pl.BlockSpec((TM, D_MODEL), lambda i: (i, 0)), _const_spec((1, D_MODEL)),
                  _const_spec((D_MODEL, PROJ_WIDTH))],
        out_specs=specs,
        out_shape=outs,
        compiler_params=_params("parallel"),
        name="proj",
    )(x, gain.reshape(1, -1), w_arranged)


CMP_ROW = CMP_STRIDE * NSA_KV


def _cmp_kernel(kc_ref, vc_ref, posk_ref, posv_ref, w1k_ref, w1v_ref, w2k_ref, w2v_ref, ko_ref, vo_ref):
    n_rows = kc_ref.shape[1]

    def one(src_ref, pos_ref, w1_ref, w2_ref, out_ref):
        src = src_ref[0].astype(F32)
        first = _dot((src + pos_ref[0:1, :]).astype(BF16), w1_ref[0])
        second = _dot((src + pos_ref[1:2, :]).astype(BF16), w1_ref[1])
        hidden = _silu(first + pltpu.roll(second, n_rows - 1, 0))
        out = _dot(hidden.astype(BF16), w2_ref[...])
        for g in range(NSA_GROUPS):
            out_ref[0, g] = out[:, g * NSA_HEAD_DIM:(g + 1) * NSA_HEAD_DIM].astype(out_ref.dtype)

    one(kc_ref, posk_ref, w1k_ref, w2k_ref, ko_ref)
    one(vc_ref, posv_ref, w1v_ref, w2v_ref, vo_ref)


def _arrange_cmp(pos, w1, w2):
    hid2 = NSA_GROUPS * CMP_HIDDEN
    w1r = w1.reshape(2, CMP_STRIDE, NSA_HEAD_DIM, CMP_HIDDEN)
    eye = jnp.eye(NSA_GROUPS, dtype=w1.dtype)
    w1g = jnp.einsum("hidk,ge->higdek", w1r, eye).reshape(2, CMP_ROW, hid2).astype(BF16)
    w2g = jnp.einsum("kd,ge->gked", w2, eye).reshape(hid2, NSA_KV).astype(BF16)
    posr = jnp.broadcast_to(pos.reshape(2, CMP_STRIDE, 1, NSA_HEAD_DIM),
                            (2, CMP_STRIDE, NSA_GROUPS, NSA_HEAD_DIM)).reshape(2, CMP_ROW).astype(F32)
    return posr, w1g, w2g


def _cmp(kc, vc, posk, posv, w1k, w1v, w2k, w2v, batch, seq):
    n_rows = seq // CMP_STRIDE
    src = pl.BlockSpec((1, n_rows, CMP_ROW), lambda b: (b, 0, 0))
    out = pl.BlockSpec((1, NSA_GROUPS, n_rows, NSA_HEAD_DIM), lambda b: (b, 0, 0, 0))
    out_shape = jax.ShapeDtypeStruct((batch, NSA_GROUPS, n_rows, NSA_HEAD_DIM), BF16)
    hid2 = NSA_GROUPS * CMP_HIDDEN
    return pl.pallas_call(
        _cmp_kernel,
        grid=(batch,),
        in_specs=[src, src, _const_spec((2, CMP_ROW)), _const_spec((2, CMP_ROW)),
                  _const_spec((2, CMP_ROW, hid2)), _const_spec((2, CMP_ROW, hid2)),
                  _const_spec((hid2, NSA_KV)), _const_spec((hid2, NSA_KV))],
        out_specs=[out, out],
        out_shape=[out_shape, out_shape],
        compiler_params=_params("parallel"),
        name="cmp",
    )(kc.reshape(batch, n_rows, CMP_ROW), vc.reshape(batch, n_rows, CMP_ROW), posk, posv, w1k, w1v, w2k, w2v)


def _rel_bucket_table(n):
    exact = REL_BUCKETS // 2
    dist = np.arange(n)
    d = np.maximum(dist, 1).astype(np.float32)
    log_bucket = exact + (np.log(d / np.float32(exact)) / np.float32(math.log(REL_MAX_DIST / exact))
                          * np.float32(REL_BUCKETS - exact)).astype(np.int32)
    return np.where(dist < exact, dist, np.minimum(log_bucket, REL_BUCKETS - 1))


def _toeplitz(w, n_rows, n_cols):
    p = w.shape[-1]
    flat = jnp.tile(w, (1,) * (w.ndim - 1) + (n_rows,))[..., :n_rows * (p - 1)]
    return flat.reshape(w.shape[:-1] + (n_rows, p - 1))[..., :n_cols]


def _nsa_tables(rel_bias, seq):
    bucket = _rel_bucket_table(seq + 2 * TQ)
    n_cmp_pad = seq // CMP_STRIDE
    bias = rel_bias.astype(F32).T * LOG2E
    neg = jnp.full((NSA_HEADS, 1), NEG_INF, F32)

    def by_distance(values, dist):
        cls = np.where(dist >= 0, bucket[np.maximum(dist, 0)], REL_BUCKETS)
        one_hot = jnp.asarray(cls.reshape(-1)[None, :] == np.arange(REL_BUCKETS + 1)[:, None], F32)
        picked = jnp.dot(jnp.concatenate([values, neg], axis=1), one_hot, precision=lax.Precision.HIGHEST)
        return picked.reshape((NSA_HEADS,) + dist.shape)

    p = 2 * n_cmp_pad
    k = np.arange(p)[None, :]
    a_minus_c = np.where(k < n_cmp_pad, -k, p - k)
    dist = CMP_STRIDE * a_minus_c + np.arange(CMP_STRIDE)[:, None] - (CMP_BLOCK - 1)
    cb = _toeplitz(by_distance(bias, dist), n_cmp_pad, n_cmp_pad).transpose(0, 2, 1, 3)
    cb = cb.reshape(NSA_GROUPS, NSA_HPG, seq // TQ, TQ, n_cmp_pad).transpose(0, 2, 1, 3, 4)
    cb = cb.reshape(NSA_GROUPS, seq // TQ, NSA_HPG * TQ, n_cmp_pad)
    shifted = bias - bias[:, REL_BUCKETS - 1:]
    k = np.arange(2 * TQ)
    i_minus_j = np.where(k < TQ, -k, 2 * TQ - k)
    t0 = _toeplitz(by_distance(shifted, i_minus_j), TQ, TQ)
    t1 = _toeplitz(by_distance(shifted, TQ + i_minus_j), TQ, TQ)
    w2 = jnp.where(lax.broadcasted_iota(jnp.int32, (TQ, TQ), 1) > lax.broadcasted_iota(jnp.int32, (TQ, TQ), 0),
                   0.0, NEG_INF).astype(F32)
    tabs = jnp.stack([t0, t1, jnp.broadcast_to(w2, t0.shape)], axis=1)
    tabs = tabs.reshape(NSA_GROUPS, NSA_HPG, 3, TQ, TQ).transpose(0, 2, 1, 3, 4)
    tabs = tabs.reshape(NSA_GROUPS, 3, NSA_HPG * TQ, TQ)
    c0 = np.arange(n_cmp_pad)[None, :] * CMP_STRIDE
    s0 = np.arange(LANES)[:, None] * SEL_BLOCK
    overlap = np.maximum(np.minimum(c0 + CMP_BLOCK, s0 + SEL_BLOCK) - np.maximum(c0, s0), 0)
    valid = (np.arange(LANES)[:, None] < seq // SEL_BLOCK) & (np.arange(n_cmp_pad)[None, :] < n_cmp_pad - 1)
    to_sel_t = jnp.asarray(np.where(valid, overlap / CMP_BLOCK, 0.0), BF16)
    return cb, tabs, to_sel_t


SEL, WIN = 0, 1


def _nsa_kernel(q_ref, kv_ref, kc_ref, vc_ref, sm_ref, cb_ref, tab_ref, tosel_ref, o_ref, m_ref, acc_ref):
    qi = pl.program_id(1)
    n_sel_blocks = kv_ref.shape[0] // SEL_BLOCK
    halves = TQ // NSA_RB
    n_blocks = NSA_HPG * halves
    sm = sm_ref[...]
    ones_cols = jnp.ones((TQ, NSA_HEAD_DIM), BF16)

    def q_rows(r):
        half = r % halves
        return slice(half * NSA_RB, (half + 1) * NSA_RB)

    def tab_rows(r):
        return slice(r * NSA_RB, (r + 1) * NSA_RB)

    def flash_step(streams, first=False):
        chains = []
        for branch, g, q_list, k_t, v_t, add_fn in streams:
            v_ext = jnp.concatenate([v_t, ones_cols], axis=1)
            chains += [(branch, g, r, (q_list[r], k_t), v_ext, add_fn) for r in range(n_blocks)]
        waves = [chains[i:i + NSA_WAVE] for i in range(0, len(chains), NSA_WAVE)]
        s, m_old, m_new, p = {}, {}, {}, {}
        for step in range(len(waves) + 2):
            for branch, g, r, (q, k_t), _, add_fn in (waves[step] if step < len(waves) else ()):
                s[branch, g, r] = _dot_nt(q, k_t) if add_fn is None else _dot_nt(q, k_t) + add_fn(r)
            for branch, g, r, _, _, _ in (waves[step - 1] if 0 < step <= len(waves) else ()):
                c = (branch, g, r)
                m_new[c] = jnp.broadcast_to(jnp.max(s[c], axis=-1, keepdims=True), (NSA_RB, LANES))
                if not first:
                    m_old[c] = m_ref[branch, g, tab_rows(r), :]
                    m_new[c] = jnp.maximum(m_old[c], m_new[c])
                m_ref[branch, g, tab_rows(r), :] = m_new[c]
                p[c] = jnp.exp2((s[c] - jnp.concatenate([m_new[c]] * (TQ // LANES), axis=1)).astype(BF16))
            for branch, g, r, _, v_ext, _ in (waves[step - 2] if step >= 2 else ()):
                c = (branch, g, r)
                acc = _dot(p[c], v_ext)
                if not first:
                    acc = jnp.exp2(m_old[c] - m_new[c]) * acc_ref[branch, g, tab_rows(r), :] + acc
                acc_ref[branch, g, tab_rows(r), :] = acc

    def weighted(branch, g, gate):
        acc = acc_ref[branch, g]
        return acc[:, :NSA_HEAD_DIM] * (gate / acc[:, NSA_HEAD_DIM:NSA_HEAD_DIM + 1])

    groups = range(NSA_GROUPS)

    def head_cols(g, r):
        c0 = (g * NSA_HPG + r // halves) * NSA_HEAD_DIM
        return slice(c0, c0 + NSA_HEAD_DIM)

    q_blocks = [[q_ref[q_rows(r), head_cols(g, r)] for r in range(n_blocks)] for g in groups]

    def kv_tile(g, kt, which):
        c0 = (g * 4 + which) * NSA_HEAD_DIM
        return kv_ref[pl.ds(pl.multiple_of(kt * TQ, TQ), TQ), c0:c0 + NSA_HEAD_DIM]

    def compressed(g):
        rows = NSA_HPG * TQ
        tok = qi * TQ + (lax.broadcasted_iota(jnp.int32, (rows, 1), 0) & (TQ - 1))
        s = _dot_nt(jnp.concatenate(q_blocks[g], axis=0), kc_ref[0, g]) + cb_ref[g, 0]
        e = jnp.exp2(s - jnp.max(s, axis=-1, keepdims=True))
        p = e / jnp.sum(e, axis=-1, keepdims=True) * jnp.where(tok >= CMP_BLOCK - 1, 1.0, 0.0)
        out = _dot(p.astype(BF16), vc_ref[0, g])
        p_sum = p[0:TQ]
        for j in range(1, NSA_HPG):
            p_sum = p_sum + p[j * TQ:(j + 1) * TQ]
        return out, p_sum

    def dropped_blocks(p_sum):
        p_hi = p_sum.astype(BF16)
        p_lo = (p_sum - p_hi.astype(F32)).astype(BF16)
        imp_t = _dot_nt(tosel_ref[...], p_hi) + _dot_nt(tosel_ref[...], p_lo)
        imp_t = imp_t[0:n_sel_blocks]
        blk = lax.broadcasted_iota(jnp.int32, (n_sel_blocks, TQ), 0)
        cur = jnp.right_shift(qi * TQ + lax.broadcasted_iota(jnp.int32, (n_sel_blocks, TQ), 1), SEL_SHIFT)
        score = jnp.where(blk <= cur, imp_t, -FORCE_SCORE)
        for forced_blk in (0, cur, cur - 1):
            score = jnp.where(blk == forced_blk, FORCE_SCORE, score)
        rank = jnp.zeros((n_sel_blocks, TQ), F32)
        for m_blk in range(n_sel_blocks):
            other = score[m_blk:m_blk + 1, :]
            tie = jnp.where(blk > m_blk, 1.0, 0.0)
            rank = rank + jnp.where(other > score, 1.0, jnp.where(other == score, tie, 0.0))
        drop_t = jnp.where(rank < SEL_TOPK, jnp.where(score > -1.0, 0.0, 1.0), 1.0)
        drop_t = jnp.concatenate([drop_t, jnp.ones((LANES - n_sel_blocks, TQ), F32)], axis=0)
        return drop_t.T.astype(BF16)

    def sel_stream(g, kt, tab):
        key_blk = jnp.right_shift(kt * TQ + lax.broadcasted_iota(jnp.int32, (TQ, NSA_HEAD_DIM), 0), SEL_SHIFT)
        hit = lax.broadcasted_iota(jnp.int32, (TQ, NSA_HEAD_DIM), 1) == key_blk
        k_aug = jnp.concatenate([kv_tile(g, kt, 0), jnp.where(hit, NEG_INF, 0.0).astype(BF16)], axis=1)
        add_fn = None if tab is None else (lambda r: tab_ref[g, tab, tab_rows(r), :])
        return SEL, g, q_aug[g], k_aug, kv_tile(g, kt, 1), add_fn

    def win_stream(g, kt, tab):
        return WIN, g, q_blocks[g], kv_tile(g, kt, 2), kv_tile(g, kt, 3), lambda r: tab_ref[g, tab, tab_rows(r), :]

    flash_step([win_stream(g, qi, 0) for g in groups], first=True)
    cmp_out = [compressed(g) for g in groups]
    drop = [dropped_blocks(cmp_out[g][1]) for g in groups]
    q_aug = [[jnp.concatenate([q_blocks[g][r], drop[g][q_rows(r), :NSA_HEAD_DIM]], axis=1) for r in range(n_blocks)]
             for g in groups]
    flash_step([sel_stream(g, qi, 0) for g in groups], first=True)

    @pl.when(qi >= 1)
    def _():
        flash_step([sel_stream(g, qi - 1, 1) for g in groups] + [win_stream(g, qi - 1, 1) for g in groups])

    @pl.when(qi >= 2)
    def _():
        flash_step([sel_stream(g, qi - 2, None) for g in groups] + [win_stream(g, qi - 2, 2) for g in groups])

    @pl.loop(0, jnp.maximum(qi - 2, 0))
    def _(kt):
        flash_step([sel_stream(g, kt, None) for g in groups])

    gates_all = jax.nn.sigmoid(sm)
    for g in groups:
        def gate(branch, g=g):
            c0 = SM_GATE + branch * NSA_HEADS + g * NSA_HPG
            return jnp.concatenate([gates_all[:, c0 + j:c0 + j + 1] for j in range(NSA_HPG)], axis=0)

        o = gate(0) * cmp_out[g][0] + weighted(SEL, g, gate(1)) + weighted(WIN, g, gate(2))
        o = jnp.concatenate([o[j * TQ:(j + 1) * TQ] for j in range(NSA_HPG)], axis=1)
        o_ref[:, g * NSA_HPG * NSA_HEAD_DIM:(g + 1) * NSA_HPG * NSA_HEAD_DIM] = o.astype(o_ref.dtype)


def _nsa(q, kvsw, k_cmp, v_cmp, sm, cb, tabs, to_sel_t, batch, seq):
    assert WINDOW == 2 * TQ and seq % TQ == 0
    nq = seq // TQ
    n_cmp_pad = seq // CMP_STRIDE
    rows = NSA_HPG * TQ
    cmp_spec = pl.BlockSpec((1, NSA_GROUPS, n_cmp_pad, NSA_HEAD_DIM), lambda b, i: (b, 0, 0, 0))
    return pl.pallas_call(
        _nsa_kernel,
        grid=(batch, nq),
        in_specs=[
            pl.BlockSpec((TQ, NSA_Q), lambda b, i: (b * nq + i, 0)),
            pl.BlockSpec((seq, 4 * NSA_KV), lambda b, i: (b, 0)),
            cmp_spec, cmp_spec,
            pl.BlockSpec((TQ, LANES), lambda b, i: (b * nq + i, 0)),
            pl.BlockSpec((NSA_GROUPS, 1, rows, n_cmp_pad), lambda b, i: (0, i, 0, 0)),
            _const_spec((NSA_GROUPS, 3, rows, TQ)),
            _const_spec((LANES, n_cmp_pad)),
        ],
        out_specs=pl.BlockSpec((TQ, NSA_Q), lambda b, i: (b * nq + i, 0)),
        out_shape=jax.ShapeDtypeStruct((batch * seq, NSA_Q), BF16),
        scratch_shapes=[pltpu.VMEM((2, NSA_GROUPS, rows, LANES), F32),
                        pltpu.VMEM((2, NSA_GROUPS, rows, 2 * NSA_HEAD_DIM), F32)],
        compiler_params=_params("parallel", "arbitrary"),
        name="nsa",
    )(q, kvsw, k_cmp, v_cmp, sm, cb, tabs, to_sel_t)


def _softplus(x):
    return jnp.maximum(x, 0.0) + jnp.log(1.0 + jnp.exp(-jnp.abs(x)))


def _l2_normalize(x):
    return x * lax.rsqrt(jnp.sum(x * x, axis=-1, keepdims=True) + NORM_EPS)


def _dn_kernel(dn_ref, z_ref, sm_ref, cw_ref, alog_ref, dtb_ref, gn_ref, o_ref, state_ref):
    c = DN_CHUNK
    n_chunks = dn_ref.shape[0] // c
    ii = lax.broadcasted_iota(jnp.int32, (c, c), 0)
    jj = lax.broadcasted_iota(jnp.int32, (c, c), 1)
    causal = ii >= jj
    strict = ii > jj
    split = ii ^ jj
    tri_ones = jnp.where(causal, 1.0, 0.0).astype(BF16)
    eye = jnp.where(ii == jj, 1.0, 0.0)
    state_ref[...] = jnp.zeros(state_ref.shape, F32)

    span = c * DN_GROUP

    def group(gi, carry):
        r0 = pl.multiple_of(gi * span, span)
        rows = pl.ds(r0, span)
        x = dn_ref[rows, :].astype(F32)
        halo_rows = pl.ds(pl.multiple_of(jnp.maximum(r0 - CONV_HALO, 0), CONV_HALO), CONV_HALO)
        halo = dn_ref[halo_rows, :].astype(F32) * jnp.where(gi > 0, 1.0, 0.0)
        xc = jnp.concatenate([halo, x], axis=0)
        y = x * cw_ref[DN_CONV - 1:DN_CONV, :]
        for k in range(1, DN_CONV):
            y = y + pltpu.roll(xc, k, 0)[CONV_HALO:, :] * cw_ref[DN_CONV - 1 - k:DN_CONV - k, :]
        y = _silu(y)

        smc = sm_ref[rows, :]
        log_decay = -jnp.exp(alog_ref[...]) * _softplus(smc + dtb_ref[...])
        beta = jax.nn.sigmoid(smc)
        ld1 = log_decay.astype(BF16)
        rest = log_decay - ld1.astype(F32)
        ld2 = rest.astype(BF16)
        ld3 = (rest - ld2.astype(F32)).astype(BF16)

        q_l, k_l, v_l, gcol_l, grow_l, gend_l, bcol_l = [], [], [], [], [], [], []
        for k in range(DN_GROUP):
            cr = slice(k * c, (k + 1) * c)
            gc = _dot(tri_ones, ld1[cr]) + _dot(tri_ones, ld2[cr]) + _dot(tri_ones, ld3[cr])
            gc_t = gc.T
            for h in range(DN_HEADS):
                q_l.append(_l2_normalize(y[cr, h * DN_HEAD_DIM:(h + 1) * DN_HEAD_DIM]) * DN_HEAD_DIM ** -0.5)
                k_l.append(_l2_normalize(y[cr, DN_W + h * DN_HEAD_DIM:DN_W + (h + 1) * DN_HEAD_DIM]))
                v_l.append(y[cr, 2 * DN_W + h * DN_HEAD_DIM:2 * DN_W + (h + 1) * DN_HEAD_DIM])
                gcol_l.append(gc[:, SM_A + h:SM_A + h + 1])
                grow_l.append(gc_t[SM_A + h:SM_A + h + 1, :])
                gend_l.append(gc[c - 1:c, SM_A + h:SM_A + h + 1])
                bcol_l.append(beta[cr, SM_B + h:SM_B + h + 1])
        qs, ks, vs = jnp.stack(q_l), jnp.stack(k_l), jnp.stack(v_l)
        g_col, g_row, g_end, b_col = jnp.stack(gcol_l), jnp.stack(grow_l), jnp.stack(gend_l), jnp.stack(bcol_l)

        decay = jnp.exp(jnp.where(causal, g_col - g_row, NEG_INF))
        qb = qs.astype(BF16)
        kb = ks.astype(BF16)
        n_mat = jnp.where(strict, b_col * _bmm_nt(kb, kb) * decay, 0.0)
        inv = eye - jnp.where(split == 1, n_mat, 0.0)
        for lvl in range(1, DN_LEVELS):
            n_lvl = jnp.where(jnp.right_shift(split, lvl) == 1, n_mat, 0.0).astype(BF16)
            inv_b = inv.astype(BF16)
            inv = inv - _bmm(inv_b, _bmm(n_lvl, inv_b).astype(BF16))
        e_col = jnp.exp(g_col)
        rhs = jnp.concatenate([vs * b_col, ks * (b_col * e_col)], axis=2).astype(BF16)
        sol = _bmm(inv.astype(BF16), rhs)
        u = sol[:, :, :DN_HEAD_DIM]
        w = sol[:, :, DN_HEAD_DIM:].astype(BF16)
        qk = (_bmm_nt(qb, kb) * decay).astype(BF16)
        q_dec = (qs * e_col).astype(BF16)
        k_dec = ks * jnp.exp(g_end - g_col)
        k_dec_t = jnp.stack([k_dec[n].T for n in range(DN_GROUP * DN_HEADS)]).astype(BF16)
        g_chunk = jnp.exp(g_end)

        state = state_ref[...]
        for k in range(DN_GROUP):
            hs = slice(k * DN_HEADS, (k + 1) * DN_HEADS)
            state_b = state.astype(BF16)
            v_new = (u[hs] - _bmm(w[hs], state_b)).astype(BF16)
            o = _bmm(q_dec[hs], state_b) + _bmm(qk[hs], v_new)
            state = state * g_chunk[hs] + _bmm(k_dec_t[hs], v_new)
            out_rows = pl.ds(r0 + k * c, c)
            for h in range(DN_HEADS):
                hd = slice(h * DN_HEAD_DIM, (h + 1) * DN_HEAD_DIM)
                zh = z_ref[out_rows, hd].astype(F32)
                o_ref[out_rows, hd] = (_rms(o[h], gn_ref[...]) * _silu(zh)).astype(o_ref.dtype)
        state_ref[...] = state
        return carry

    lax.fori_loop(0, n_chunks // DN_GROUP, group, 0)


def _dn(dn, z, sm, conv_w, a_log, dt_bias, out_gain, batch, seq):
    assert DN_CHUNK == 1 << DN_LEVELS and seq % (DN_CHUNK * DN_GROUP) == 0
    pad = jnp.zeros((1, LANES), F32)
    return pl.pallas_call(
        _dn_kernel,
        grid=(batch,),
        in_specs=[pl.BlockSpec((seq, 3 * DN_W), lambda b: (b, 0)), pl.BlockSpec((seq, DN_W), lambda b: (b, 0)),
                  pl.BlockSpec((seq, LANES), lambda b: (b, 0)), _const_spec((DN_CONV, 3 * DN_W)),
                  _const_spec((1, LANES)), _const_spec((1, LANES)), _const_spec((1, DN_HEAD_DIM))],
        out_specs=pl.BlockSpec((seq, DN_W), lambda b: (b, 0)),
        out_shape=jax.ShapeDtypeStruct((batch * seq, DN_W), BF16),
        scratch_shapes=[pltpu.VMEM((DN_HEADS, DN_HEAD_DIM, DN_HEAD_DIM), F32)],
        compiler_params=_params("parallel"),
        name="dn",
    )(dn, z, sm, conv_w.astype(F32), pad.at[0, SM_A:SM_A + DN_HEADS].set(a_log.astype(F32)),
      pad.at[0, SM_A:SM_A + DN_HEADS].set(dt_bias.astype(F32)), out_gain.reshape(1, -1).astype(F32))


def _memkv_kernel(mem_ref, g_ref, w_ref, k_ref, v_ref):
    mn = _rms(mem_ref[0], g_ref[...]).astype(BF16)
    k_ref[0] = _dot(mn, w_ref[:, :X_W]).astype(k_ref.dtype)
    v_ref[0] = _dot(mn, w_ref[:, X_W:]).astype(v_ref.dtype)


def _memkv(mem, gain, w_kv):
    batch, n_mem, _ = mem.shape
    out = pl.BlockSpec((1, n_mem, X_W), lambda b: (b, 0, 0))
    shape = jax.ShapeDtypeStruct((batch, n_mem, X_W), BF16)
    return pl.pallas_call(
        _memkv_kernel,
        grid=(batch,),
        in_specs=[pl.BlockSpec((1, n_mem, D_MODEL), lambda b: (b, 0, 0)), _const_spec((1, D_MODEL)),
                  _const_spec((D_MODEL, 2 * X_W))],
        out_specs=[out, out],
        out_shape=[shape, shape],
        compiler_params=_params("parallel"),
        name="memkv",
    )(mem, gain.reshape(1, -1), w_kv.astype(BF16))


def _memattn_kernel(q_ref, k_ref, v_ref, o_ref):
    for h in range(X_HEADS):
        hd = slice(h * X_HEAD_DIM, (h + 1) * X_HEAD_DIM)
        s = _dot_nt(q_ref[:, hd], k_ref[0, :, hd])
        e = jnp.exp(s - jnp.max(s, axis=-1, keepdims=True))
        p = e / jnp.sum(e, axis=-1, keepdims=True)
        o_ref[:, hd] = _dot(p.astype(BF16), v_ref[0, :, hd]).astype(o_ref.dtype)


def _memattn(mq, mk, mv, batch, seq):
    nq = seq // TQ_MEM
    n_mem = mk.shape[1]
    kv = pl.BlockSpec((1, n_mem, X_W), lambda b, i: (b, 0, 0))
    row = pl.BlockSpec((TQ_MEM, X_W), lambda b, i: (b * nq + i, 0))
    return pl.pallas_call(
        _memattn_kernel,
        grid=(batch, nq),
        in_specs=[row, kv, kv],
        out_specs=row,
        out_shape=jax.ShapeDtypeStruct((batch * seq, X_W), BF16),
        compiler_params=_params("parallel", "arbitrary"),
        name="memattn",
    )(mq, mk, mv)


def _merge_kernel(x_ref, on_ref, od_ref, om_ref, gpre_ref, wbg_ref, wbn_ref, wbd_ref, wbm_ref, wo_ref, gpost_ref,
                  o_ref):
    x = x_ref[...]
    u = _rms(x, gpre_ref[...]).astype(BF16)
    merged = jnp.zeros(x.shape, F32)
    for i, (br_ref, w_ref) in enumerate(((on_ref, wbn_ref), (od_ref, wbd_ref), (om_ref, wbm_ref))):
        gates = jax.nn.sigmoid(_dot(u, wbg_ref[:, i * D_MODEL:(i + 1) * D_MODEL]))
        merged = merged + gates * _dot(br_ref[...], w_ref[...])
    y = _dot(merged.astype(BF16), wo_ref[...])
    o_ref[...] = x + _rms(y, gpost_ref[...])


def _merge(x, o_nsa, o_dn, o_mem, gpre, wbg, wbn, wbd, wbm, wo, gpost):
    t = x.shape[0]
    row = pl.BlockSpec((TM, D_MODEL), lambda i: (i, 0))
    br = pl.BlockSpec((TM, NSA_Q), lambda i: (i, 0))
    return pl.pallas_call(
        _merge_kernel,
        grid=(t // TM,),
        in_specs=[row, br, br, br, _const_spec((1, D_MODEL)), _const_spec((D_MODEL, N_BRANCH * D_MODEL)),
                  _const_spec((NSA_Q, D_MODEL)), _const_spec((DN_W, D_MODEL)), _const_spec((X_W, D_MODEL)),
                  _const_spec((D_MODEL, D_MODEL)), _const_spec((1, D_MODEL))],
        out_specs=row,
        out_shape=jax.ShapeDtypeStruct((t, D_MODEL), F32),
        compiler_params=_params("parallel"),
        name="merge",
    )(x, o_nsa, o_dn, o_mem, gpre.reshape(1, -1), wbg.astype(BF16), wbn.astype(BF16), wbd.astype(BF16),
      wbm.astype(BF16), wo.astype(BF16), gpost.reshape(1, -1))


def kernel(x, mem, ffn1_pre_norm, ffn1_w_gate, ffn1_w_up, ffn1_w_down, ffn1_post_norm, mix_pre_norm, w_in, cmp_pos_k, cmp_pos_v, cmp_k_w1, cmp_k_w2, cmp_v_w1, cmp_v_w2, rel_bias, dn_conv_w, dn_a_log, dn_dt_bias, dn_out_norm, mem_norm, w_mem_kv, w_branch_nsa, w_branch_dn, w_branch_mem, w_branch_gate, w_out, mix_post_norm, ffn2_pre_norm, ffn2_w_gate, ffn2_w_up, ffn2_w_down, ffn2_post_norm):
    batch, seq, d = x.shape
    xt = x.reshape(batch * seq, d)
    cb, tabs, to_sel_t = _nsa_tables(rel_bias, seq)
    for l in range(ffn1_pre_norm.shape[0]):
        xt = _ffn(xt, ffn1_pre_norm[l], ffn1_w_gate[l], ffn1_w_up[l], ffn1_w_down[l], ffn1_post_norm[l])

        q, kc, vc, kvsw, dn, z, mq, sm = _proj(xt, mix_pre_norm[l], _arrange_w_in(w_in[l]))
        posk, w1k, w2k = _arrange_cmp(cmp_pos_k[l], cmp_k_w1[l], cmp_k_w2[l])
        posv, w1v, w2v = _arrange_cmp(cmp_pos_v[l], cmp_v_w1[l], cmp_v_w2[l])
        k_cmp, v_cmp = _cmp(kc, vc, posk, posv, w1k, w1v, w2k, w2v, batch, seq)
        o_nsa = _nsa(q, kvsw, k_cmp, v_cmp, sm, cb, tabs, to_sel_t, batch, seq)
        o_dn = _dn(dn, z, sm, dn_conv_w[l], dn_a_log[l], dn_dt_bias[l], dn_out_norm[l], batch, seq)
        mk, mv = _memkv(mem, mem_norm[l], w_mem_kv[l])
        o_mem = _memattn(mq, mk, mv, batch, seq)
        xt = _merge(xt, o_nsa, o_dn, o_mem, mix_pre_norm[l], w_branch_gate[l], w_branch_nsa[l], w_branch_dn[l],
                    w_branch_mem[l], w_out[l], mix_post_norm[l])

        xt = _ffn(xt, ffn2_pre_norm[l], ffn2_w_gate[l], ffn2_w_up[l], ffn2_w_down[l], ffn2_post_norm[l])
    return xt.reshape(batch, seq, d)
```

```python
import math

import jax
import jax.numpy as jnp
import numpy as np
from jax import lax
from jax.experimental import pallas as pl
from jax.experimental.pallas import tpu as pltpu

F32 = jnp.float32
BF16 = jnp.bfloat16

D_MODEL = 1024
D_FF = 2816
NORM_EPS = 1e-6
NEG_INF = -1e30
FORCE_SCORE = 1e4

NSA_HEADS = 8
NSA_GROUPS = 2
NSA_HPG = NSA_HEADS // NSA_GROUPS
NSA_HEAD_DIM = 64
CMP_BLOCK = 32
CMP_STRIDE = 16
CMP_HIDDEN = 256
SEL_BLOCK = 64
SEL_SHIFT = 6
SEL_TOPK = 16
WINDOW = 512
REL_BUCKETS = 32
REL_MAX_DIST = 128

DN_HEADS = 4
DN_HEAD_DIM = 128
DN_CONV = 4
X_HEADS = 4
X_HEAD_DIM = 128
N_BRANCH = 3

NSA_Q = NSA_HEADS * NSA_HEAD_DIM
NSA_KV = NSA_GROUPS * NSA_HEAD_DIM
DN_W = DN_HEADS * DN_HEAD_DIM
X_W = X_HEADS * X_HEAD_DIM

LANES = 128
VMEM_LIMIT = 56 * 1024 * 1024

TM = 512
FF_CHUNK = 256
TQ = 256
NSA_WAVE = 4
DN_CHUNK = 128
DN_LEVELS = 7
DN_GROUP = 4
CONV_HALO = 16
TQ_MEM = 512

SM_GATE = 0
SM_A = N_BRANCH * NSA_HEADS
SM_B = SM_A + DN_HEADS


def _dot(a, b):
    return jnp.dot(a, b, preferred_element_type=F32)


def _dot_nt(a, b):
    return lax.dot_general(a, b, (((1,), (1,)), ((), ())), preferred_element_type=F32)


def _bmm(a, b):
    return lax.dot_general(a, b, (((2,), (1,)), ((0,), (0,))), preferred_element_type=F32)


def _bmm_nt(a, b):
    return lax.dot_general(a, b, (((2,), (2,)), ((0,), (0,))), preferred_element_type=F32)


def _rms(x, gain):
    return x * lax.rsqrt(jnp.mean(x * x, axis=-1, keepdims=True) + NORM_EPS) * gain


def _silu(x):
    return x * jax.nn.sigmoid(x)


def _const_spec(shape):
    zeros = (0,) * len(shape)
    return pl.BlockSpec(shape, lambda *_: zeros, pipeline_mode=pl.Buffered(1))


def _params(*sem):
    return pltpu.CompilerParams(dimension_semantics=sem, vmem_limit_bytes=VMEM_LIMIT)


def _ffn_kernel(x_ref, pre_ref, wg_ref, wu_ref, wd_ref, post_ref, o_ref):
    x = x_ref[...]
    xn = _rms(x, pre_ref[...]).astype(BF16)
    acc = jnp.zeros(x.shape, F32)
    for c in range(D_FF // FF_CHUNK):
        cols = slice(c * FF_CHUNK, (c + 1) * FF_CHUNK)
        gate = _dot(xn, wg_ref[:, cols])
        up = _dot(xn, wu_ref[:, cols])
        acc = acc + _dot((_silu(gate) * up).astype(BF16), wd_ref[cols, :])
    o_ref[...] = x + 0.5 * _rms(acc, post_ref[...])


def _ffn(x, pre, wg, wu, wd, post):
    t = x.shape[0]
    row = pl.BlockSpec((TM, D_MODEL), lambda i: (i, 0))
    return pl.pallas_call(
        _ffn_kernel,
        grid=(t // TM,),
        in_specs=[row, _const_spec((1, D_MODEL)), _const_spec((D_MODEL, D_FF)), _const_spec((D_MODEL, D_FF)),
                  _const_spec((D_FF, D_MODEL)), _const_spec((1, D_MODEL))],
        out_specs=row,
        out_shape=jax.ShapeDtypeStruct((t, D_MODEL), F32),
        compiler_params=_params("parallel"),
        name="ffn",
    )(x, pre.reshape(1, -1), wg.astype(BF16), wu.astype(BF16), wd.astype(BF16), post.reshape(1, -1))


PROJ_SPLITS = (("q", NSA_Q), ("kc", NSA_KV), ("vc", NSA_KV), ("kvsw", 4 * NSA_KV), ("dn", 3 * DN_W),
               ("z", DN_W), ("mq", X_W), ("sm", LANES))
PROJ_WIDTH = sum(w for _, w in PROJ_SPLITS)
LOG2E = math.log2(math.e)
PROJ_SCALES = {"q": NSA_HEAD_DIM ** -0.5 * LOG2E, "mq": X_HEAD_DIM ** -0.5}


def _proj_kernel(x_ref, g_ref, w_ref, *out_refs):
    xn = _rms(x_ref[...], g_ref[...]).astype(BF16)
    start = 0
    for (name, width), ref in zip(PROJ_SPLITS, out_refs):
        for c0 in range(0, width, 512):
            cw = min(512, width - c0)
            y = _dot(xn, w_ref[:, start + c0:start + c0 + cw])
            if name in PROJ_SCALES:
                y = y * PROJ_SCALES[name]
            ref[:, c0:c0 + cw] = y.astype(ref.dtype)
        start += width


def _arrange_w_in(w_in):
    o_kv = NSA_Q
    o_g = o_kv + 6 * NSA_KV
    o_dn = o_g + N_BRANCH * NSA_HEADS
    o_a = o_dn + 3 * DN_W
    o_z = o_a + 2 * DN_HEADS
    o_mq = o_z + DN_W
    kv = w_in[:, o_kv:o_g].reshape(D_MODEL, 6, NSA_GROUPS, NSA_HEAD_DIM)
    kvsw = kv[:, 2:6].transpose(0, 2, 1, 3).reshape(D_MODEL, 4 * NSA_KV)
    small = jnp.concatenate([w_in[:, o_g:o_dn], w_in[:, o_a:o_z],
                             jnp.zeros((D_MODEL, LANES - N_BRANCH * NSA_HEADS - 2 * DN_HEADS), w_in.dtype)], axis=1)
    return jnp.concatenate([w_in[:, :o_kv], kv[:, 0].reshape(D_MODEL, NSA_KV), kv[:, 1].reshape(D_MODEL, NSA_KV),
                            kvsw, w_in[:, o_dn:o_a], w_in[:, o_z:o_mq], w_in[:, o_mq:], small], axis=1).astype(BF16)


def _proj(x, gain, w_arranged):
    t = x.shape[0]
    outs, specs = [], []
    for name, width in PROJ_SPLITS:
        outs.append(jax.ShapeDtypeStruct((t, width), F32 if name == "sm" else BF16))
        specs.append(pl.BlockSpec((TM, width), lambda i: (i, 0)))
    return pl.pallas_call(
        _proj_kernel,
        grid=(t // TM,),
        in_specs=[pl.BlockSpec((TM, D_MODEL), lambda i: (i, 0)), _const_spec((1, D_MODEL)),
                  _const_spec((D_MODEL, PROJ_WIDTH))],
        out_specs=specs,
        out_shape=outs,
        compiler_params=_params("parallel"),
        name="proj",
    )(x, gain.reshape(1, -1), w_arranged)


CMP_ROW = CMP_STRIDE * NSA_KV


def _cmp_kernel(kc_ref, vc_ref, posk_ref, posv_ref, w1k_ref, w1v_ref, w2k_ref, w2v_ref, ko_ref, vo_ref):
    n_rows = kc_ref.shape[1]

    def one(src_ref, pos_ref, w1_ref, w2_ref, out_ref):
        src = src_ref[0].astype(F32)
        first = _dot((src + pos_ref[0:1, :]).astype(BF16), w1_ref[0])
        second = _dot((src + pos_ref[1:2, :]).astype(BF16), w1_ref[1])
        hidden = _silu(first + pltpu.roll(second, n_rows - 1, 0))
        out = _dot(hidden.astype(BF16), w2_ref[...])
        for g in range(NSA_GROUPS):
            out_ref[0, g] = out[:, g * NSA_HEAD_DIM:(g + 1) * NSA_HEAD_DIM].astype(out_ref.dtype)

    one(kc_ref, posk_ref, w1k_ref, w2k_ref, ko_ref)
    one(vc_ref, posv_ref, w1v_ref, w2v_ref, vo_ref)


def _arrange_cmp(pos, w1, w2):
    hid2 = NSA_GROUPS * CMP_HIDDEN
    w1r = w1.reshape(2, CMP_STRIDE, NSA_HEAD_DIM, CMP_HIDDEN)
    eye = jnp.eye(NSA_GROUPS, dtype=w1.dtype)
    w1g = jnp.einsum("hidk,ge->higdek", w1r, eye).reshape(2, CMP_ROW, hid2).astype(BF16)
    w2g = jnp.einsum("kd,ge->gked", w2, eye).reshape(hid2, NSA_KV).astype(BF16)
    posr = jnp.broadcast_to(pos.reshape(2, CMP_STRIDE, 1, NSA_HEAD_DIM),
                            (2, CMP_STRIDE, NSA_GROUPS, NSA_HEAD_DIM)).reshape(2, CMP_ROW).astype(F32)
    return posr, w1g, w2g


def _cmp(kc, vc, posk, posv, w1k, w1v, w2k, w2v, batch, seq):
    n_rows = seq // CMP_STRIDE
    src = pl.BlockSpec((1, n_rows, CMP_ROW), lambda b: (b, 0, 0))
    out = pl.BlockSpec((1, NSA_GROUPS, n_rows, NSA_HEAD_DIM), lambda b: (b, 0, 0, 0))
    out_shape = jax.ShapeDtypeStruct((batch, NSA_GROUPS, n_rows, NSA_HEAD_DIM), BF16)
    hid2 = NSA_GROUPS * CMP_HIDDEN
    return pl.pallas_call(
        _cmp_kernel,
        grid=(batch,),
        in_specs=[src, src, _const_spec((2, CMP_ROW)), _const_spec((2, CMP_ROW)),
                  _const_spec((2, CMP_ROW, hid2)), _const_spec((2, CMP_ROW, hid2)),
                  _const_spec((hid2, NSA_KV)), _const_spec((hid2, NSA_KV))],
        out_specs=[out, out],
        out_shape=[out_shape, out_shape],
        compiler_params=_params("parallel"),
        name="cmp",
    )(kc.reshape(batch, n_rows, CMP_ROW), vc.reshape(batch, n_rows, CMP_ROW), posk, posv, w1k, w1v, w2k, w2v)


def _rel_bucket_table(n):
    exact = REL_BUCKETS // 2
    dist = np.arange(n)
    d = np.maximum(dist, 1).astype(np.float32)
    log_bucket = exact + (np.log(d / np.float32(exact)) / np.float32(math.log(REL_MAX_DIST / exact))
                          * np.float32(REL_BUCKETS - exact)).astype(np.int32)
    return np.where(dist < exact, dist, np.minimum(log_bucket, REL_BUCKETS - 1))


def _toeplitz(w, n_rows, n_cols):
    p = w.shape[-1]
    flat = jnp.tile(w, (1,) * (w.ndim - 1) + (n_rows,))[..., :n_rows * (p - 1)]
    return flat.reshape(w.shape[:-1] + (n_rows, p - 1))[..., :n_cols]


def _nsa_tables(rel_bias, seq):
    bucket = _rel_bucket_table(seq + 2 * TQ)
    n_cmp_pad = seq // CMP_STRIDE
    bias = rel_bias.astype(F32).T * LOG2E
    neg = jnp.full((NSA_HEADS, 1), NEG_INF, F32)

    def by_distance(values, dist):
        cls = np.where(dist >= 0, bucket[np.maximum(dist, 0)], REL_BUCKETS)
        one_hot = jnp.asarray(cls.reshape(-1)[None, :] == np.arange(REL_BUCKETS + 1)[:, None], F32)
        picked = jnp.dot(jnp.concatenate([values, neg], axis=1), one_hot, precision=lax.Precision.HIGHEST)
        return picked.reshape((NSA_HEADS,) + dist.shape)

    p = 2 * n_cmp_pad
    k = np.arange(p)[None, :]
    a_minus_c = np.where(k < n_cmp_pad, -k, p - k)
    dist = CMP_STRIDE * a_minus_c + np.arange(CMP_STRIDE)[:, None] - (CMP_BLOCK - 1)
    cb = _toeplitz(by_distance(bias, dist), n_cmp_pad, n_cmp_pad).transpose(0, 2, 1, 3)
    cb = cb.reshape(NSA_GROUPS, NSA_HPG, seq // TQ, TQ, n_cmp_pad).transpose(0, 2, 1, 3, 4)
    cb = cb.reshape(NSA_GROUPS, seq // TQ, NSA_HPG * TQ, n_cmp_pad).swapaxes(-1, -2)
    shifted = bias - bias[:, REL_BUCKETS - 1:]
    k = np.arange(2 * TQ)
    i_minus_j = np.where(k < TQ, -k, 2 * TQ - k)
    t0 = _toeplitz(by_distance(shifted, i_minus_j), TQ, TQ)
    t1 = _toeplitz(by_distance(shifted, TQ + i_minus_j), TQ, TQ)
    w2 = jnp.where(lax.broadcasted_iota(jnp.int32, (TQ, TQ), 1) > lax.broadcasted_iota(jnp.int32, (TQ, TQ), 0),
                   0.0, NEG_INF).astype(F32)
    tabs = jnp.stack([t0, t1, jnp.broadcast_to(w2, t0.shape)], axis=1)
    tabs = tabs.reshape(NSA_GROUPS, NSA_HPG, 3, TQ, TQ).transpose(0, 2, 1, 4, 3)
    tabs = tabs.reshape(NSA_GROUPS, 3, NSA_HPG * TQ, TQ)
    c0 = np.arange(n_cmp_pad)[None, :] * CMP_STRIDE
    s0 = np.arange(LANES)[:, None] * SEL_BLOCK
    overlap = np.maximum(np.minimum(c0 + CMP_BLOCK, s0 + SEL_BLOCK) - np.maximum(c0, s0), 0)
    valid = (np.arange(LANES)[:, None] < seq // SEL_BLOCK) & (np.arange(n_cmp_pad)[None, :] < n_cmp_pad - 1)
    to_sel_t = jnp.asarray(np.where(valid, overlap / CMP_BLOCK, 0.0), BF16)
    return cb, tabs, to_sel_t


SEL, WIN = 0, 1


def _nsa_kernel(q_ref, kv_ref, kc_ref, vc_ref, sm_ref, cb_ref, tab_ref, tosel_ref, o_ref, m_ref, acc_ref):
    qi = pl.program_id(1)
    n_sel_blocks = kv_ref.shape[0] // SEL_BLOCK
    sm = sm_ref[...]
    ones_cols = jnp.ones((TQ, NSA_HEAD_DIM), BF16)
    heads = range(NSA_HPG)
    groups = range(NSA_GROUPS)

    def head_rows(j):
        return slice(j * TQ, (j + 1) * TQ)

    def flash_step(streams, first=False):
        chains = []
        for branch, g, q_list, k_t, v_t, add_fn in streams:
            v_ext_t = jnp.concatenate([v_t, ones_cols], axis=1).astype(F32).T.astype(BF16)
            chains += [(branch, g, j, (q_list[j], k_t), v_ext_t, add_fn) for j in heads]
        waves = [chains[i:i + NSA_WAVE] for i in range(0, len(chains), NSA_WAVE)]
        s, m_old, m_new, p = {}, {}, {}, {}
        for step in range(len(waves) + 2):
            for branch, g, j, (q, k_t), _, add_fn in (waves[step] if step < len(waves) else ()):
                s[branch, g, j] = _dot_nt(k_t, q) if add_fn is None else _dot_nt(k_t, q) + add_fn(j)
            for branch, g, j, _, _, _ in (waves[step - 1] if 0 < step <= len(waves) else ()):
                c = (branch, g, j)
                m_new[c] = jnp.broadcast_to(jnp.max(s[c], axis=0, keepdims=True), (8, TQ))
                if not first:
                    m_old[c] = m_ref[branch, g, j]
                    m_new[c] = jnp.maximum(m_old[c], m_new[c])
                m_ref[branch, g, j] = m_new[c]
                p[c] = jnp.exp2((s[c] - m_new[c][0:1]).astype(BF16))
            for branch, g, j, _, v_ext_t, _ in (waves[step - 2] if step >= 2 else ()):
                c = (branch, g, j)
                acc = _dot(v_ext_t, p[c])
                if not first:
                    acc = jnp.exp2(m_old[c] - m_new[c])[0:1] * acc_ref[branch, g, j] + acc
                acc_ref[branch, g, j] = acc

    def head_cols(g, j):
        c0 = (g * NSA_HPG + j) * NSA_HEAD_DIM
        return slice(c0, c0 + NSA_HEAD_DIM)

    q_heads = [[q_ref[:, head_cols(g, j)] for j in heads] for g in groups]

    def kv_tile(g, kt, which):
        c0 = (g * 4 + which) * NSA_HEAD_DIM
        return kv_ref[pl.ds(pl.multiple_of(kt * TQ, TQ), TQ), c0:c0 + NSA_HEAD_DIM]

    def compressed(g):
        cols = NSA_HPG * TQ
        tok = qi * TQ + (lax.broadcasted_iota(jnp.int32, (1, cols), 1) & (TQ - 1))
        s = _dot_nt(kc_ref[0, g], jnp.concatenate(q_heads[g], axis=0)) + cb_ref[g, 0]
        e = jnp.exp2(s - jnp.max(s, axis=0, keepdims=True))
        p = e * (jnp.where(tok >= CMP_BLOCK - 1, 1.0, 0.0) / jnp.sum(e, axis=0, keepdims=True))
        vc_pad = jnp.concatenate([vc_ref[0, g], jnp.zeros((vc_ref.shape[2], NSA_HEAD_DIM), BF16)], axis=1)
        out_t = _dot(vc_pad.astype(F32).T.astype(BF16), p.astype(BF16))
        p_sum = p[:, 0:TQ]
        for j in range(1, NSA_HPG):
            p_sum = p_sum + p[:, j * TQ:(j + 1) * TQ]
        return out_t, p_sum

    def dropped_blocks(p_sum):
        p_hi = p_sum.astype(BF16)
        p_lo = (p_sum - p_hi.astype(F32)).astype(BF16)
        imp_t = _dot(tosel_ref[...], p_hi) + _dot(tosel_ref[...], p_lo)
        imp_t = imp_t[0:n_sel_blocks]
        blk = lax.broadcasted_iota(jnp.int32, (n_sel_blocks, TQ), 0)
        cur = jnp.right_shift(qi * TQ + lax.broadcasted_iota(jnp.int32, (n_sel_blocks, TQ), 1), SEL_SHIFT)
        score = jnp.where(blk <= cur, imp_t, -FORCE_SCORE)
        for forced_blk in (0, cur, cur - 1):
            score = jnp.where(blk == forced_blk, FORCE_SCORE, score)
        ranks = []
        for r0 in range(0, n_sel_blocks, 8):
            mine = score[r0:r0 + 8]
            row = r0 + lax.broadcasted_iota(jnp.int32, (8, TQ), 0)
            rank = jnp.zeros(mine.shape, F32)
            for m_blk in range(n_sel_blocks):
                other = score[m_blk:m_blk + 1, :]
                if m_blk < r0:
                    rank = jnp.where(other >= mine, rank + 1.0, rank)
                elif m_blk >= r0 + 8:
                    rank = jnp.where(other > mine, rank + 1.0, rank)
                else:
                    tie = jnp.where(row > m_blk, 1.0, 0.0)
                    rank = rank + jnp.where(other > mine, 1.0, jnp.where(other == mine, tie, 0.0))
            ranks.append(rank)
        rank = jnp.concatenate(ranks, axis=0)
        drop_t = jnp.where(rank < SEL_TOPK, jnp.where(score > -1.0, 0.0, 1.0), 1.0)
        drop_t = jnp.concatenate([drop_t, jnp.ones((LANES - n_sel_blocks, TQ), F32)], axis=0)
        return drop_t.T.astype(BF16)

    def sel_stream(g, kt, tab):
        key_blk = jnp.right_shift(kt * TQ + lax.broadcasted_iota(jnp.int32, (TQ, NSA_HEAD_DIM), 0), SEL_SHIFT)
        hit = lax.broadcasted_iota(jnp.int32, (TQ, NSA_HEAD_DIM), 1) == key_blk
        k_aug = jnp.concatenate([kv_tile(g, kt, 0), jnp.where(hit, NEG_INF, 0.0).astype(BF16)], axis=1)
        add_fn = None if tab is None else (lambda j: tab_ref[g, tab, head_rows(j), :])
        return SEL, g, q_aug[g], k_aug, kv_tile(g, kt, 1), add_fn

    def win_stream(g, kt, tab):
        return WIN, g, q_heads[g], kv_tile(g, kt, 2), kv_tile(g, kt, 3), lambda j: tab_ref[g, tab, head_rows(j), :]

    flash_step([win_stream(g, qi, 0) for g in groups], first=True)
    cmp_out = [compressed(g) for g in groups]
    drop = [dropped_blocks(cmp_out[g][1]) for g in groups]
    q_aug = [[jnp.concatenate([q_heads[g][j], drop[g][:, :NSA_HEAD_DIM]], axis=1) for j in heads] for g in groups]
    flash_step([sel_stream(g, qi, 0) for g in groups], first=True)

    @pl.when(qi >= 1)
    def _():
        flash_step([sel_stream(g, qi - 1, 1) for g in groups] + [win_stream(g, qi - 1, 1) for g in groups])

    @pl.when(qi >= 2)
    def _():
        flash_step([sel_stream(g, qi - 2, None) for g in groups] + [win_stream(g, qi - 2, 2) for g in groups])

    @pl.loop(0, jnp.maximum(qi - 2, 0))
    def _(kt):
        flash_step([sel_stream(g, kt, None) for g in groups])

    gates_t = jax.nn.sigmoid(sm).T
    for g in groups:
        for j in heads:
            def gate(branch):
                c = SM_GATE + branch * NSA_HEADS + g * NSA_HPG + j
                return gates_t[c:c + 1, :]

            def weighted(branch, gate_row):
                acc = acc_ref[branch, g, j]
                return acc * (gate_row / acc[NSA_HEAD_DIM:NSA_HEAD_DIM + 1, :])

            o_t = gate(0) * cmp_out[g][0][:, head_rows(j)] + weighted(SEL, gate(1)) + weighted(WIN, gate(2))
            o_ref[:, head_cols(g, j)] = o_t.T[:, :NSA_HEAD_DIM].astype(o_ref.dtype)


def _nsa(q, kvsw, k_cmp, v_cmp, sm, cb, tabs, to_sel_t, batch, seq):
    assert WINDOW == 2 * TQ and seq % TQ == 0
    nq = seq // TQ
    n_cmp_pad = seq // CMP_STRIDE
    rows = NSA_HPG * TQ
    cmp_spec = pl.BlockSpec((1, NSA_GROUPS, n_cmp_pad, NSA_HEAD_DIM), lambda b, i: (b, 0, 0, 0))
    return pl.pallas_call(
        _nsa_kernel,
        grid=(batch, nq),
        in_specs=[
            pl.BlockSpec((TQ, NSA_Q), lambda b, i: (b * nq + i, 0)),
            pl.BlockSpec((seq, 4 * NSA_KV), lambda b, i: (b, 0)),
            cmp_spec, cmp_spec,
            pl.BlockSpec((TQ, LANES), lambda b, i: (b * nq + i, 0)),
            pl.BlockSpec((NSA_GROUPS, 1, n_cmp_pad, rows), lambda b, i: (0, i, 0, 0)),
            _const_spec((NSA_GROUPS, 3, rows, TQ)),
            _const_spec((LANES, n_cmp_pad)),
        ],
        out_specs=pl.BlockSpec((TQ, NSA_Q), lambda b, i: (b * nq + i, 0)),
        out_shape=jax.ShapeDtypeStruct((batch * seq, NSA_Q), BF16),
        scratch_shapes=[pltpu.VMEM((2, NSA_GROUPS, NSA_HPG, 8, TQ), F32),
                        pltpu.VMEM((2, NSA_GROUPS, NSA_HPG, 2 * NSA_HEAD_DIM, TQ), F32)],
        compiler_params=_params("parallel", "arbitrary"),
        name="nsa",
    )(q, kvsw, k_cmp, v_cmp, sm, cb, tabs, to_sel_t)


def _softplus(x):
    return jnp.maximum(x, 0.0) + jnp.log(1.0 + jnp.exp(-jnp.abs(x)))


def _l2_normalize(x):
    return x * lax.rsqrt(jnp.sum(x * x, axis=-1, keepdims=True) + NORM_EPS)


def _dn_kernel(dn_ref, z_ref, sm_ref, cw_ref, alog_ref, dtb_ref, gn_ref, o_ref, state_ref):
    c = DN_CHUNK
    n_chunks = dn_ref.shape[0] // c
    ii = lax.broadcasted_iota(jnp.int32, (c, c), 0)
    jj = lax.broadcasted_iota(jnp.int32, (c, c), 1)
    causal = ii >= jj
    strict = ii > jj
    split = ii ^ jj
    tri_ones = jnp.where(causal, 1.0, 0.0).astype(BF16)
    eye = jnp.where(ii == jj, 1.0, 0.0)
    state_ref[...] = jnp.zeros(state_ref.shape, F32)

    span = c * DN_GROUP

    def group(gi, carry):
        r0 = pl.multiple_of(gi * span, span)
        rows = pl.ds(r0, span)
        x = dn_ref[rows, :].astype(F32)
        halo_rows = pl.ds(pl.multiple_of(jnp.maximum(r0 - CONV_HALO, 0), CONV_HALO), CONV_HALO)
        halo = dn_ref[halo_rows, :].astype(F32) * jnp.where(gi > 0, 1.0, 0.0)
        xc = jnp.concatenate([halo, x], axis=0)
        y = x * cw_ref[DN_CONV - 1:DN_CONV, :]
        for k in range(1, DN_CONV):
            y = y + pltpu.roll(xc, k, 0)[CONV_HALO:, :] * cw_ref[DN_CONV - 1 - k:DN_CONV - k, :]
        y = _silu(y)

        smc = sm_ref[rows, :]
        log_decay = -jnp.exp(alog_ref[...]) * _softplus(smc + dtb_ref[...])
        beta = jax.nn.sigmoid(smc)
        ld1 = log_decay.astype(BF16)
        rest = log_decay - ld1.astype(F32)
        ld2 = rest.astype(BF16)
        ld3 = (rest - ld2.astype(F32)).astype(BF16)

        q_l, k_l, v_l, gcol_l, grow_l, gend_l, bcol_l = [], [], [], [], [], [], []
        for k in range(DN_GROUP):
            cr = slice(k * c, (k + 1) * c)
            gc = _dot(tri_ones, ld1[cr]) + _dot(tri_ones, ld2[cr]) + _dot(tri_ones, ld3[cr])
            gc_t = gc.T
            for h in range(DN_HEADS):
                q_l.append(_l2_normalize(y[cr, h * DN_HEAD_DIM:(h + 1) * DN_HEAD_DIM]) * DN_HEAD_DIM ** -0.5)
                k_l.append(_l2_normalize(y[cr, DN_W + h * DN_HEAD_DIM:DN_W + (h + 1) * DN_HEAD_DIM]))
                v_l.append(y[cr, 2 * DN_W + h * DN_HEAD_DIM:2 * DN_W + (h + 1) * DN_HEAD_DIM])
                gcol_l.append(gc[:, SM_A + h:SM_A + h + 1])
                grow_l.append(gc_t[SM_A + h:SM_A + h + 1, :])
                gend_l.append(gc[c - 1:c, SM_A + h:SM_A + h + 1])
                bcol_l.append(beta[cr, SM_B + h:SM_B + h + 1])
        qs, ks, vs = jnp.stack(q_l), jnp.stack(k_l), jnp.stack(v_l)
        g_col, g_row, g_end, b_col = jnp.stack(gcol_l), jnp.stack(grow_l), jnp.stack(gend_l), jnp.stack(bcol_l)

        decay = jnp.exp(jnp.where(causal, g_col - g_row, NEG_INF))
        qb = qs.astype(BF16)
        kb = ks.astype(BF16)
        n_mat = jnp.where(strict, b_col * _bmm_nt(kb, kb) * decay, 0.0)
        inv = eye - jnp.where(split == 1, n_mat, 0.0)
        for lvl in range(1, DN_LEVELS):
            n_lvl = jnp.where(jnp.right_shift(split, lvl) == 1, n_mat, 0.0).astype(BF16)
            inv_b = inv.astype(BF16)
            inv = inv - _bmm(inv_b, _bmm(n_lvl, inv_b).astype(BF16))
        e_col = jnp.exp(g_col)
        rhs = jnp.concatenate([vs * b_col, ks * (b_col * e_col)], axis=2).astype(BF16)
        sol = _bmm(inv.astype(BF16), rhs)
        u = sol[:, :, :DN_HEAD_DIM]
        w = sol[:, :, DN_HEAD_DIM:].astype(BF16)
        qk = (_bmm_nt(qb, kb) * decay).astype(BF16)
        q_dec = (qs * e_col).astype(BF16)
        k_dec = ks * jnp.exp(g_end - g_col)
        k_dec_t = jnp.stack([k_dec[n].T for n in range(DN_GROUP * DN_HEADS)]).astype(BF16)
        g_chunk = jnp.exp(g_end)

        state = state_ref[...]
        for k in range(DN_GROUP):
            hs = slice(k * DN_HEADS, (k + 1) * DN_HEADS)
            state_b = state.astype(BF16)
            v_new = (u[hs] - _bmm(w[hs], state_b)).astype(BF16)
            o = _bmm(q_dec[hs], state_b) + _bmm(qk[hs], v_new)
            state = state * g_chunk[hs] + _bmm(k_dec_t[hs], v_new)
            out_rows = pl.ds(r0 + k * c, c)
            for h in range(DN_HEADS):
                hd = slice(h * DN_HEAD_DIM, (h + 1) * DN_HEAD_DIM)
                zh = z_ref[out_rows, hd].astype(F32)
                o_ref[out_rows, hd] = (_rms(o[h], gn_ref[...]) * _silu(zh)).astype(o_ref.dtype)
        state_ref[...] = state
        return carry

    lax.fori_loop(0, n_chunks // DN_GROUP, group, 0)


def _dn(dn, z, sm, conv_w, a_log, dt_bias, out_gain, batch, seq):
    assert DN_CHUNK == 1 << DN_LEVELS and seq % (DN_CHUNK * DN_GROUP) == 0
    pad = jnp.zeros((1, LANES), F32)
    return pl.pallas_call(
        _dn_kernel,
        grid=(batch,),
        in_specs=[pl.BlockSpec((seq, 3 * DN_W), lambda b: (b, 0)), pl.BlockSpec((seq, DN_W), lambda b: (b, 0)),
                  pl.BlockSpec((seq, LANES), lambda b: (b, 0)), _const_spec((DN_CONV, 3 * DN_W)),
                  _const_spec((1, LANES)), _const_spec((1, LANES)), _const_spec((1, DN_HEAD_DIM))],
        out_specs=pl.BlockSpec((seq, DN_W), lambda b: (b, 0)),
        out_shape=jax.ShapeDtypeStruct((batch * seq, DN_W), BF16),
        scratch_shapes=[pltpu.VMEM((DN_HEADS, DN_HEAD_DIM, DN_HEAD_DIM), F32)],
        compiler_params=_params("parallel"),
        name="dn",
    )(dn, z, sm, conv_w.astype(F32), pad.at[0, SM_A:SM_A + DN_HEADS].set(a_log.astype(F32)),
      pad.at[0, SM_A:SM_A + DN_HEADS].set(dt_bias.astype(F32)), out_gain.reshape(1, -1).astype(F32))


def _memkv_kernel(mem_ref, g_ref, w_ref, k_ref, v_ref):
    mn = _rms(mem_ref[0], g_ref[...]).astype(BF16)
    k_ref[0] = _dot(mn, w_ref[:, :X_W]).astype(k_ref.dtype)
    v_ref[0] = _dot(mn, w_ref[:, X_W:]).astype(v_ref.dtype)


def _memkv(mem, gain, w_kv):
    batch, n_mem, _ = mem.shape
    out = pl.BlockSpec((1, n_mem, X_W), lambda b: (b, 0, 0))
    shape = jax.ShapeDtypeStruct((batch, n_mem, X_W), BF16)
    return pl.pallas_call(
        _memkv_kernel,
        grid=(batch,),
        in_specs=[pl.BlockSpec((1, n_mem, D_MODEL), lambda b: (b, 0, 0)), _const_spec((1, D_MODEL)),
                  _const_spec((D_MODEL, 2 * X_W))],
        out_specs=[out, out],
        out_shape=[shape, shape],
        compiler_params=_params("parallel"),
        name="memkv",
    )(mem, gain.reshape(1, -1), w_kv.astype(BF16))


def _memattn_kernel(q_ref, k_ref, v_ref, o_ref):
    for h in range(X_HEADS):
        hd = slice(h * X_HEAD_DIM, (h + 1) * X_HEAD_DIM)
        s = _dot_nt(q_ref[:, hd], k_ref[0, :, hd])
        e = jnp.exp(s - jnp.max(s, axis=-1, keepdims=True))
        p = e / jnp.sum(e, axis=-1, keepdims=True)
        o_ref[:, hd] = _dot(p.astype(BF16), v_ref[0, :, hd]).astype(o_ref.dtype)


def _memattn(mq, mk, mv, batch, seq):
    nq = seq // TQ_MEM
    n_mem = mk.shape[1]
    kv = pl.BlockSpec((1, n_mem, X_W), lambda b, i: (b, 0, 0))
    row = pl.BlockSpec((TQ_MEM, X_W), lambda b, i: (b * nq + i, 0))
    return pl.pallas_call(
        _memattn_kernel,
        grid=(batch, nq),
        in_specs=[row, kv, kv],
        out_specs=row,
        out_shape=jax.ShapeDtypeStruct((batch * seq, X_W), BF16),
        compiler_params=_params("parallel", "arbitrary"),
        name="memattn",
    )(mq, mk, mv)


def _merge_kernel(x_ref, on_ref, od_ref, om_ref, gpre_ref, wbg_ref, wbn_ref, wbd_ref, wbm_ref, wo_ref, gpost_ref,
                  o_ref):
    x = x_ref[...]
    u = _rms(x, gpre_ref[...]).astype(BF16)
    merged = jnp.zeros(x.shape, F32)
    for i, (br_ref, w_ref) in enumerate(((on_ref, wbn_ref), (od_ref, wbd_ref), (om_ref, wbm_ref))):
        gates = jax.nn.sigmoid(_dot(u, wbg_ref[:, i * D_MODEL:(i + 1) * D_MODEL]))
        merged = merged + gates * _dot(br_ref[...], w_ref[...])
    y = _dot(merged.astype(BF16), wo_ref[...])
    o_ref[...] = x + _rms(y, gpost_ref[...])


def _merge(x, o_nsa, o_dn, o_mem, gpre, wbg, wbn, wbd, wbm, wo, gpost):
    t = x.shape[0]
    row = pl.BlockSpec((TM, D_MODEL), lambda i: (i, 0))
    br = pl.BlockSpec((TM, NSA_Q), lambda i: (i, 0))
    return pl.pallas_call(
        _merge_kernel,
        grid=(t // TM,),
        in_specs=[row, br, br, br, _const_spec((1, D_MODEL)), _const_spec((D_MODEL, N_BRANCH * D_MODEL)),
                  _const_spec((NSA_Q, D_MODEL)), _const_spec((DN_W, D_MODEL)), _const_spec((X_W, D_MODEL)),
                  _const_spec((D_MODEL, D_MODEL)), _const_spec((1, D_MODEL))],
        out_specs=row,
        out_shape=jax.ShapeDtypeStruct((t, D_MODEL), F32),
        compiler_params=_params("parallel"),
        name="merge",
    )(x, o_nsa, o_dn, o_mem, gpre.reshape(1, -1), wbg.astype(BF16), wbn.astype(BF16), wbd.astype(BF16),
      wbm.astype(BF16), wo.astype(BF16), gpost.reshape(1, -1))


def kernel(x, mem, ffn1_pre_norm, ffn1_w_gate, ffn1_w_up, ffn1_w_down, ffn1_post_norm, mix_pre_norm, w_in, cmp_pos_k, cmp_pos_v, cmp_k_w1, cmp_k_w2, cmp_v_w1, cmp_v_w2, rel_bias, dn_conv_w, dn_a_log, dn_dt_bias, dn_out_norm, mem_norm, w_mem_kv, w_branch_nsa, w_branch_dn, w_branch_mem, w_branch_gate, w_out, mix_post_norm, ffn2_pre_norm, ffn2_w_gate, ffn2_w_up, ffn2_w_down, ffn2_post_norm):
    batch, seq, d = x.shape
    xt = x.reshape(batch * seq, d)
    cb, tabs, to_sel_t = _nsa_tables(rel_bias, seq)
    for l in range(ffn1_pre_norm.shape[0]):
        xt = _ffn(xt, ffn1_pre_norm[l], ffn1_w_gate[l], ffn1_w_up[l], ffn1_w_down[l], ffn1_post_norm[l])

        q, kc, vc, kvsw, dn, z, mq, sm = _proj(xt, mix_pre_norm[l], _arrange_w_in(w_in[l]))
        posk, w1k, w2k = _arrange_cmp(cmp_pos_k[l], cmp_k_w1[l], cmp_k_w2[l])
        posv, w1v, w2v = _arrange_cmp(cmp_pos_v[l], cmp_v_w1[l], cmp_v_w2[l])
        k_cmp, v_cmp = _cmp(kc, vc, posk, posv, w1k, w1v, w2k, w2v, batch, seq)
        o_nsa = _nsa(q, kvsw, k_cmp, v_cmp, sm, cb, tabs, to_sel_t, batch, seq)
        o_dn = _dn(dn, z, sm, dn_conv_w[l], dn_a_log[l], dn_dt_bias[l], dn_out_norm[l], batch, seq)
        mk, mv = _memkv(mem, mem_norm[l], w_mem_kv[l])
        o_mem = _memattn(mq, mk, mv, batch, seq)
        xt = _merge(xt, o_nsa, o_dn, o_mem, mix_pre_norm[l], w_branch_gate[l], w_branch_nsa[l], w_branch_dn[l],
                    w_branch_mem[l], w_out[l], mix_post_norm[l])

        xt = _ffn(xt, ffn2_pre_norm[l], ffn2_w_gate[l], ffn2_w_up[l], ffn2_w_down[l], ffn2_post_norm[l])
    return xt.reshape(batch, seq, d)
```

```python
import math

import jax
import jax.numpy as jnp
import numpy as np
from jax import lax
from jax.experimental import pallas as pl
from jax.experimental.pallas import tpu as pltpu

F32 = jnp.float32
BF16 = jnp.bfloat16

D_MODEL = 1024
D_FF = 2816
NORM_EPS = 1e-6
NEG_INF = -1e30
FORCE_SCORE = 1e4

NSA_HEADS = 8
NSA_GROUPS = 2
NSA_HPG = NSA_HEADS // NSA_GROUPS
NSA_HEAD_DIM = 64
CMP_BLOCK = 32
CMP_STRIDE = 16
CMP_HIDDEN = 256
SEL_BLOCK = 64
SEL_SHIFT = 6
SEL_TOPK = 16
WINDOW = 512
REL_BUCKETS = 32
REL_MAX_DIST = 128

DN_HEADS = 4
DN_HEAD_DIM = 128
DN_CONV = 4
X_HEADS = 4
X_HEAD_DIM = 128
N_BRANCH = 3

NSA_Q = NSA_HEADS * NSA_HEAD_DIM
NSA_KV = NSA_GROUPS * NSA_HEAD_DIM
DN_W = DN_HEADS * DN_HEAD_DIM
X_W = X_HEADS * X_HEAD_DIM

LANES = 128
VMEM_LIMIT = 56 * 1024 * 1024

TM = 1024
FF_CHUNK = 256
TQ = 256
NSA_ACC_ROWS = 80
NSA_WAVE = 4
DN_CHUNK = 128
DN_LEVELS = 7
DN_GROUP = 4
CONV_HALO = 16
TQ_MEM = 512

SM_GATE = 0
SM_A = N_BRANCH * NSA_HEADS
SM_B = SM_A + DN_HEADS


def _dot(a, b):
    return jnp.dot(a, b, preferred_element_type=F32)


def _dot_nt(a, b):
    return lax.dot_general(a, b, (((1,), (1,)), ((), ())), preferred_element_type=F32)


def _bmm(a, b):
    return lax.dot_general(a, b, (((2,), (1,)), ((0,), (0,))), preferred_element_type=F32)


def _bmm_nt(a, b):
    return lax.dot_general(a, b, (((2,), (2,)), ((0,), (0,))), preferred_element_type=F32)


def _rms(x, gain):
    return x * lax.rsqrt(jnp.mean(x * x, axis=-1, keepdims=True) + NORM_EPS) * gain


def _silu(x):
    return x * jax.nn.sigmoid(x)


def _const_spec(shape):
    zeros = (0,) * len(shape)
    return pl.BlockSpec(shape, lambda *_: zeros, pipeline_mode=pl.Buffered(1))


def _params(*sem):
    return pltpu.CompilerParams(dimension_semantics=sem, vmem_limit_bytes=VMEM_LIMIT)


def _ffn_kernel(x_ref, pre_ref, wg_ref, wu_ref, wd_ref, post_ref, o_ref):
    x = x_ref[...]
    xn = _rms(x, pre_ref[...]).astype(BF16)
    acc = jnp.zeros(x.shape, F32)
    for c in range(D_FF // FF_CHUNK):
        cols = slice(c * FF_CHUNK, (c + 1) * FF_CHUNK)
        gate = _dot(xn, wg_ref[:, cols])
        up = _dot(xn, wu_ref[:, cols])
        acc = acc + _dot((_silu(gate) * up).astype(BF16), wd_ref[cols, :])
    o_ref[...] = x + 0.5 * _rms(acc, post_ref[...])


def _ffn(x, pre, wg, wu, wd, post):
    t = x.shape[0]
    row = pl.BlockSpec((TM, D_MODEL), lambda i: (i, 0))
    return pl.pallas_call(
        _ffn_kernel,
        grid=(t // TM,),
        in_specs=[row, _const_spec((1, D_MODEL)), _const_spec((D_MODEL, D_FF)), _const_spec((D_MODEL, D_FF)),
                  _const_spec((D_FF, D_MODEL)), _const_spec((1, D_MODEL))],
        out_specs=row,
        out_shape=jax.ShapeDtypeStruct((t, D_MODEL), F32),
        compiler_params=_params("parallel"),
        name="ffn",
    )(x, pre.reshape(1, -1), wg.astype(BF16), wu.astype(BF16), wd.astype(BF16), post.reshape(1, -1))


PROJ_SPLITS = (("q", NSA_Q), ("kc", NSA_KV), ("vc", NSA_KV), ("kvsw", 4 * NSA_KV), ("dn", 3 * DN_W),
               ("z", DN_W), ("mq", X_W), ("sm", LANES))
PROJ_WIDTH = sum(w for _, w in PROJ_SPLITS)
LOG2E = math.log2(math.e)
PROJ_SCALES = {"q": NSA_HEAD_DIM ** -0.5 * LOG2E, "mq": X_HEAD_DIM ** -0.5}


def _proj_kernel(x_ref, g_ref, w_ref, *out_refs):
    xn = _rms(x_ref[...], g_ref[...]).astype(BF16)
    start = 0
    for (name, width), ref in zip(PROJ_SPLITS, out_refs):
        for c0 in range(0, width, 512):
            cw = min(512, width - c0)
            y = _dot(xn, w_ref[:, start + c0:start + c0 + cw])
            if name in PROJ_SCALES:
                y = y * PROJ_SCALES[name]
            ref[:, c0:c0 + cw] = y.astype(ref.dtype)
        start += width


def _arrange_w_in(w_in):
    o_kv = NSA_Q
    o_g = o_kv + 6 * NSA_KV
    o_dn = o_g + N_BRANCH * NSA_HEADS
    o_a = o_dn + 3 * DN_W
    o_z = o_a + 2 * DN_HEADS
    o_mq = o_z + DN_W
    kv = w_in[:, o_kv:o_g].reshape(D_MODEL, 6, NSA_GROUPS, NSA_HEAD_DIM)
    kvsw = kv[:, 2:6].transpose(0, 2, 1, 3).reshape(D_MODEL, 4 * NSA_KV)
    small = jnp.concatenate([w_in[:, o_g:o_dn], w_in[:, o_a:o_z],
                             jnp.zeros((D_MODEL, LANES - N_BRANCH * NSA_HEADS - 2 * DN_HEADS), w_in.dtype)], axis=1)
    return jnp.concatenate([w_in[:, :o_kv], kv[:, 0].reshape(D_MODEL, NSA_KV), kv[:, 1].reshape(D_MODEL, NSA_KV),
                            kvsw, w_in[:, o_dn:o_a], w_in[:, o_z:o_mq], w_in[:, o_mq:], small], axis=1).astype(BF16)


def _proj(x, gain, w_arranged):
    t = x.shape[0]
    outs, specs = [], []
    for name, width in PROJ_SPLITS:
        outs.append(jax.ShapeDtypeStruct((t, width), F32 if name == "sm" else BF16))
        specs.append(pl.BlockSpec((TM, width), lambda i: (i, 0)))
    return pl.pallas_call(
        _proj_kernel,
        grid=(t // TM,),
        in_specs=[pl.BlockSpec((TM, D_MODEL), lambda i: (i, 0)), _const_spec((1, D_MODEL)),
                  _const_spec((D_MODEL, PROJ_WIDTH))],
        out_specs=specs,
        out_shape=outs,
        compiler_params=_params("parallel"),
        name="proj",
    )(x, gain.reshape(1, -1), w_arranged)


CMP_ROW = CMP_STRIDE * NSA_KV


def _cmp_kernel(kc_ref, vc_ref, posk_ref, posv_ref, w1k_ref, w1v_ref, w2k_ref, w2v_ref, ko_ref, vo_ref):
    n_rows = kc_ref.shape[1]

    def one(src_ref, pos_ref, w1_ref, w2_ref, out_ref):
        src = src_ref[0].astype(F32)
        first = _dot((src + pos_ref[0:1, :]).astype(BF16), w1_ref[0])
        second = _dot((src + pos_ref[1:2, :]).astype(BF16), w1_ref[1])
        hidden = _silu(first + pltpu.roll(second, n_rows - 1, 0))
        out = _dot(hidden.astype(BF16), w2_ref[...])
        for g in range(NSA_GROUPS):
            out_ref[0, g] = out[:, g * NSA_HEAD_DIM:(g + 1) * NSA_HEAD_DIM].astype(out_ref.dtype)

    one(kc_ref, posk_ref, w1k_ref, w2k_ref, ko_ref)
    one(vc_ref, posv_ref, w1v_ref, w2v_ref, vo_ref)


def _arrange_cmp(pos, w1, w2):
    hid2 = NSA_GROUPS * CMP_HIDDEN
    w1r = w1.reshape(2, CMP_STRIDE, NSA_HEAD_DIM, CMP_HIDDEN)
    eye = jnp.eye(NSA_GROUPS, dtype=w1.dtype)
    w1g = jnp.einsum("hidk,ge->higdek", w1r, eye).reshape(2, CMP_ROW, hid2).astype(BF16)
    w2g = jnp.einsum("kd,ge->gked", w2, eye).reshape(hid2, NSA_KV).astype(BF16)
    posr = jnp.broadcast_to(pos.reshape(2, CMP_STRIDE, 1, NSA_HEAD_DIM),
                            (2, CMP_STRIDE, NSA_GROUPS, NSA_HEAD_DIM)).reshape(2, CMP_ROW).astype(F32)
    return posr, w1g, w2g


def _cmp(kc, vc, posk, posv, w1k, w1v, w2k, w2v, batch, seq):
    n_rows = seq // CMP_STRIDE
    src = pl.BlockSpec((1, n_rows, CMP_ROW), lambda b: (b, 0, 0))
    out = pl.BlockSpec((1, NSA_GROUPS, n_rows, NSA_HEAD_DIM), lambda b: (b, 0, 0, 0))
    out_shape = jax.ShapeDtypeStruct((batch, NSA_GROUPS, n_rows, NSA_HEAD_DIM), BF16)
    hid2 = NSA_GROUPS * CMP_HIDDEN
    return pl.pallas_call(
        _cmp_kernel,
        grid=(batch,),
        in_specs=[src, src, _const_spec((2, CMP_ROW)), _const_spec((2, CMP_ROW)),
                  _const_spec((2, CMP_ROW, hid2)), _const_spec((2, CMP_ROW, hid2)),
                  _const_spec((hid2, NSA_KV)), _const_spec((hid2, NSA_KV))],
        out_specs=[out, out],
        out_shape=[out_shape, out_shape],
        compiler_params=_params("parallel"),
        name="cmp",
    )(kc.reshape(batch, n_rows, CMP_ROW), vc.reshape(batch, n_rows, CMP_ROW), posk, posv, w1k, w1v, w2k, w2v)


def _rel_bucket_table(n):
    exact = REL_BUCKETS // 2
    dist = np.arange(n)
    d = np.maximum(dist, 1).astype(np.float32)
    log_bucket = exact + (np.log(d / np.float32(exact)) / np.float32(math.log(REL_MAX_DIST / exact))
                          * np.float32(REL_BUCKETS - exact)).astype(np.int32)
    return np.where(dist < exact, dist, np.minimum(log_bucket, REL_BUCKETS - 1))


def _nsa_tables(rel_bias, seq):
    bucket = _rel_bucket_table(seq + 2 * TQ)
    n_cmp_pad = seq // CMP_STRIDE
    bias = rel_bias.astype(F32).T * LOG2E
    neg = jnp.full((NSA_HEADS, 1), NEG_INF, F32)

    def by_distance(values, dist):
        cls = np.where(dist >= 0, bucket[np.maximum(dist, 0)], REL_BUCKETS).astype(np.int32)
        one_hot = (jnp.asarray(cls.reshape(1, -1))
                   == lax.broadcasted_iota(jnp.int32, (REL_BUCKETS + 1, 1), 0)).astype(F32)
        picked = jnp.dot(jnp.concatenate([values, neg], axis=1), one_hot, precision=lax.Precision.HIGHEST)
        return picked.reshape((NSA_GROUPS, NSA_HPG) + dist.shape)

    t = np.arange(seq).reshape(seq // TQ, 1, TQ)
    dist = t - (np.arange(n_cmp_pad)[None, :, None] * CMP_STRIDE + CMP_BLOCK - 1)
    cb = by_distance(bias, dist).transpose(0, 2, 3, 1, 4).reshape(NSA_GROUPS, seq // TQ, n_cmp_pad, NSA_HPG * TQ)
    shifted = bias - bias[:, REL_BUCKETS - 1:]
    q_minus_k = np.arange(TQ)[None, :] - np.arange(TQ)[:, None]
    t0 = by_distance(shifted, q_minus_k)
    t1 = by_distance(shifted, TQ + q_minus_k)
    w2 = jnp.asarray(np.where(q_minus_k < 0, 0.0, NEG_INF), F32)
    tabs = jnp.stack([t0, t1, jnp.broadcast_to(w2, t0.shape)], axis=1)
    tabs = tabs.reshape(NSA_GROUPS, 3, NSA_HPG * TQ, TQ)
    c0 = np.arange(n_cmp_pad)[None, :] * CMP_STRIDE
    s0 = np.arange(LANES)[:, None] * SEL_BLOCK
    overlap = np.maximum(np.minimum(c0 + CMP_BLOCK, s0 + SEL_BLOCK) - np.maximum(c0, s0), 0)
    valid = (np.arange(LANES)[:, None] < seq // SEL_BLOCK) & (np.arange(n_cmp_pad)[None, :] < n_cmp_pad - 1)
    to_sel_t = jnp.asarray(np.where(valid, overlap / CMP_BLOCK, 0.0), BF16)
    return cb, tabs, to_sel_t


SEL, WIN = 0, 1


def _nsa_kernel(q_ref, kv_ref, kc_ref, vc_ref, sm_ref, cb_ref, tab_ref, tosel_ref, o_ref, m_ref, acc_ref):
    qi = pl.program_id(1)
    n_sel_blocks = kv_ref.shape[0] // SEL_BLOCK
    sm = sm_ref[...]
    ones_cols = jnp.ones((TQ, NSA_HEAD_DIM), BF16)
    heads = range(NSA_HPG)
    groups = range(NSA_GROUPS)

    def head_rows(j):
        return slice(j * TQ, (j + 1) * TQ)

    def flash_step(streams, first=False):
        chains = []
        for si, (branch, g, q_list, k_t, v_t, add_fn) in enumerate(streams):
            v_ext_t = jnp.concatenate([v_t, ones_cols], axis=1).astype(F32).T[:NSA_ACC_ROWS].astype(BF16)
            chains += [(si, branch, g, j, (q_list[j], k_t), v_ext_t, add_fn) for j in heads]
        waves = [chains[i:i + NSA_WAVE] for i in range(0, len(chains), NSA_WAVE)]
        s, m_old, m_new, p = {}, {}, {}, {}
        for step in range(len(waves) + 2):
            for si, _, _, j, (q, k_t), _, add_fn in (waves[step] if step < len(waves) else ()):
                s[si, j] = _dot_nt(k_t, q) if add_fn is None else _dot_nt(k_t, q) + add_fn(j)
            for si, branch, g, j, _, _, _ in (waves[step - 1] if 0 < step <= len(waves) else ()):
                c = (si, j)
                m_new[c] = jnp.broadcast_to(jnp.max(s[c], axis=0, keepdims=True), (8, TQ))
                if not first:
                    m_old[c] = m_ref[branch, g, j]
                    m_new[c] = jnp.maximum(m_old[c], m_new[c])
                m_ref[branch, g, j] = m_new[c]
                p[c] = jnp.exp2((s[c] - m_new[c][0:1]).astype(BF16))
            for si, branch, g, j, _, v_ext_t, _ in (waves[step - 2] if step >= 2 else ()):
                c = (si, j)
                acc = _dot(v_ext_t, p[c])
                if not first:
                    acc = jnp.exp2(m_old[c] - m_new[c])[0:1] * acc_ref[branch, g, j] + acc
                acc_ref[branch, g, j] = acc

    def head_cols(g, j):
        c0 = (g * NSA_HPG + j) * NSA_HEAD_DIM
        return slice(c0, c0 + NSA_HEAD_DIM)

    q_heads = [[q_ref[:, head_cols(g, j)] for j in heads] for g in groups]

    def kv_tile(g, kt, which):
        c0 = (g * 4 + which) * NSA_HEAD_DIM
        return kv_ref[pl.ds(pl.multiple_of(kt * TQ, TQ), TQ), c0:c0 + NSA_HEAD_DIM]

    def compressed(g):
        cols = NSA_HPG * TQ
        tok = qi * TQ + (lax.broadcasted_iota(jnp.int32, (1, cols), 1) & (TQ - 1))
        s = _dot_nt(kc_ref[0, g], jnp.concatenate(q_heads[g], axis=0)) + cb_ref[g, 0]
        e = jnp.exp2(s - jnp.max(s, axis=0, keepdims=True))
        p = e * (jnp.where(tok >= CMP_BLOCK - 1, 1.0, 0.0) / jnp.sum(e, axis=0, keepdims=True))
        vc_pad = jnp.concatenate([vc_ref[0, g], jnp.zeros((vc_ref.shape[2], NSA_HEAD_DIM), BF16)], axis=1)
        out_t = _dot(vc_pad.astype(F32).T.astype(BF16), p.astype(BF16))
        p_sum = p[:, 0:TQ]
        for j in range(1, NSA_HPG):
            p_sum = p_sum + p[:, j * TQ:(j + 1) * TQ]
        return out_t, p_sum

    def dropped_blocks(p_sum):
        p_hi = p_sum.astype(BF16)
        p_lo = (p_sum - p_hi.astype(F32)).astype(BF16)
        imp_t = _dot(tosel_ref[...], p_hi) + _dot(tosel_ref[...], p_lo)
        imp_t = imp_t[0:n_sel_blocks]
        blk = lax.broadcasted_iota(jnp.int32, (n_sel_blocks, TQ), 0)
        cur = jnp.right_shift(qi * TQ + lax.broadcasted_iota(jnp.int32, (n_sel_blocks, TQ), 1), SEL_SHIFT)
        score = jnp.where(blk <= cur, imp_t, -FORCE_SCORE)
        for forced_blk in (0, cur, cur - 1):
            score = jnp.where(blk == forced_blk, FORCE_SCORE, score)
        ranks = []
        for r0 in range(0, n_sel_blocks, 8):
            mine = score[r0:r0 + 8]
            row = r0 + lax.broadcasted_iota(jnp.int32, (8, TQ), 0)
            rank = jnp.zeros(mine.shape, F32)
            for m_blk in range(n_sel_blocks):
                other = score[m_blk:m_blk + 1, :]
                if m_blk < r0:
                    rank = jnp.where(other >= mine, rank + 1.0, rank)
                elif m_blk >= r0 + 8:
                    rank = jnp.where(other > mine, rank + 1.0, rank)
                else:
                    tie = jnp.where(row > m_blk, 1.0, 0.0)
                    rank = rank + jnp.where(other > mine, 1.0, jnp.where(other == mine, tie, 0.0))
            ranks.append(rank)
        rank = jnp.concatenate(ranks, axis=0)
        drop_t = jnp.where(rank < SEL_TOPK, jnp.where(score > -1.0, 0.0, 1.0), 1.0)
        drop_t = jnp.concatenate([drop_t, jnp.ones((LANES - n_sel_blocks, TQ), F32)], axis=0)
        return drop_t.T.astype(BF16)

    def sel_stream(q_aug, g, kt, tab):
        key_blk = jnp.right_shift(kt * TQ + lax.broadcasted_iota(jnp.int32, (TQ, NSA_HEAD_DIM), 0), SEL_SHIFT)
        hit = lax.broadcasted_iota(jnp.int32, (TQ, NSA_HEAD_DIM), 1) == key_blk
        k_aug = jnp.concatenate([kv_tile(g, kt, 0), jnp.where(hit, NEG_INF, 0.0).astype(BF16)], axis=1)
        add_fn = None if tab is None else (lambda j: tab_ref[g, tab, head_rows(j), :])
        return SEL, g, q_aug[g], k_aug, kv_tile(g, kt, 1), add_fn

    def win_stream(g, kt, tab):
        return WIN, g, q_heads[g], kv_tile(g, kt, 2), kv_tile(g, kt, 3), lambda j: tab_ref[g, tab, head_rows(j), :]

    def whole_step(n_near):
        flash_step([win_stream(g, qi, 0) for g in groups], first=True)
        cmp_out = [compressed(g) for g in groups]
        flash_step([win_stream(g, qi - 1 - n, 1 + n) for n in range(n_near) for g in groups])
        drop = [dropped_blocks(cmp_out[g][1]) for g in groups]
        q_aug = [[jnp.concatenate([q_heads[g][j], drop[g][:, :NSA_HEAD_DIM]], axis=1) for j in heads]
                 for g in groups]
        flash_step([sel_stream(q_aug, g, qi, 0) for g in groups], first=True)
        flash_step([sel_stream(q_aug, g, qi - 1 - n, 1 if n == 0 else None) for n in range(n_near) for g in groups])
        if n_near == 2:
            @pl.loop(0, qi - 2)
            def _(kt):
                flash_step([sel_stream(q_aug, g, kt, None) for g in groups])

        gates_t = jax.nn.sigmoid(sm).T
        for g in groups:
            for j in heads:
                def gate(branch):
                    c = SM_GATE + branch * NSA_HEADS + g * NSA_HPG + j
                    return gates_t[c:c + 1, :]

                def weighted(branch, gate_row):
                    acc = acc_ref[branch, g, j]
                    return acc[:NSA_HEAD_DIM] * (gate_row / acc[NSA_HEAD_DIM:NSA_HEAD_DIM + 1, :])

                o_t = (gate(0) * cmp_out[g][0][:NSA_HEAD_DIM, head_rows(j)] + weighted(SEL, gate(1))
                       + weighted(WIN, gate(2)))
                o_t = jnp.concatenate([o_t, jnp.zeros_like(o_t)], axis=0)
                o_ref[:, head_cols(g, j)] = o_t.T[:, :NSA_HEAD_DIM].astype(o_ref.dtype)

    pl.when(qi == 0)(lambda: whole_step(0))
    pl.when(qi == 1)(lambda: whole_step(1))
    pl.when(qi >= 2)(lambda: whole_step(2))


def _nsa(q, kvsw, k_cmp, v_cmp, sm, cb, tabs, to_sel_t, batch, seq):
    assert WINDOW == 2 * TQ and seq % TQ == 0
    nq = seq // TQ
    n_cmp_pad = seq // CMP_STRIDE
    rows = NSA_HPG * TQ
    cmp_spec = pl.BlockSpec((1, NSA_GROUPS, n_cmp_pad, NSA_HEAD_DIM), lambda b, i: (b, 0, 0, 0))
    return pl.pallas_call(
        _nsa_kernel,
        grid=(batch, nq),
        in_specs=[
            pl.BlockSpec((TQ, NSA_Q), lambda b, i: (b * nq + i, 0)),
            pl.BlockSpec((seq, 4 * NSA_KV), lambda b, i: (b, 0)),
            cmp_spec, cmp_spec,
            pl.BlockSpec((TQ, LANES), lambda b, i: (b * nq + i, 0)),
            pl.BlockSpec((NSA_GROUPS, 1, n_cmp_pad, rows), lambda b, i: (0, i, 0, 0)),
            _const_spec((NSA_GROUPS, 3, rows, TQ)),
            _const_spec((LANES, n_cmp_pad)),
        ],
        out_specs=pl.BlockSpec((TQ, NSA_Q), lambda b, i: (b * nq + i, 0)),
        out_shape=jax.ShapeDtypeStruct((batch * seq, NSA_Q), BF16),
        scratch_shapes=[pltpu.VMEM((2, NSA_GROUPS, NSA_HPG, 8, TQ), F32),
                        pltpu.VMEM((2, NSA_GROUPS, NSA_HPG, NSA_ACC_ROWS, TQ), F32)],
        compiler_params=_params("parallel", "arbitrary"),
        name="nsa",
    )(q, kvsw, k_cmp, v_cmp, sm, cb, tabs, to_sel_t)


def _softplus(x):
    return jnp.maximum(x, 0.0) + jnp.log(1.0 + jnp.exp(-jnp.abs(x)))


def _l2_normalize(x):
    return x * lax.rsqrt(jnp.sum(x * x, axis=-1, keepdims=True) + NORM_EPS)


def _dn_kernel(dn_ref, z_ref, sm_ref, cw_ref, alog_ref, dtb_ref, gn_ref, o_ref, state_ref):
    c = DN_CHUNK
    n_chunks = dn_ref.shape[0] // c
    ii = lax.broadcasted_iota(jnp.int32, (c, c), 0)
    jj = lax.broadcasted_iota(jnp.int32, (c, c), 1)
    causal = ii >= jj
    strict = ii > jj
    split = ii ^ jj
    tri_ones = jnp.where(causal, 1.0, 0.0).astype(BF16)
    eye = jnp.where(ii == jj, 1.0, 0.0)
    state_ref[...] = jnp.zeros(state_ref.shape, F32)

    span = c * DN_GROUP

    def group(gi, carry):
        r0 = pl.multiple_of(gi * span, span)
        rows = pl.ds(r0, span)
        x = dn_ref[rows, :].astype(F32)
        halo_rows = pl.ds(pl.multiple_of(jnp.maximum(r0 - CONV_HALO, 0), CONV_HALO), CONV_HALO)
        halo = dn_ref[halo_rows, :].astype(F32) * jnp.where(gi > 0, 1.0, 0.0)
        xc = jnp.concatenate([halo, x], axis=0)
        y = x * cw_ref[DN_CONV - 1:DN_CONV, :]
        for k in range(1, DN_CONV):
            y = y + pltpu.roll(xc, k, 0)[CONV_HALO:, :] * cw_ref[DN_CONV - 1 - k:DN_CONV - k, :]
        y = _silu(y)

        smc = sm_ref[rows, :]
        log_decay = -jnp.exp(alog_ref[...]) * _softplus(smc + dtb_ref[...])
        beta = jax.nn.sigmoid(smc)
        ld1 = log_decay.astype(BF16)
        rest = log_decay - ld1.astype(F32)
        ld2 = rest.astype(BF16)
        ld3 = (rest - ld2.astype(F32)).astype(BF16)

        q_l, k_l, v_l, gcol_l, grow_l, gend_l, bcol_l = [], [], [], [], [], [], []
        for k in range(DN_GROUP):
            cr = slice(k * c, (k + 1) * c)
            gc = _dot(tri_ones, ld1[cr]) + _dot(tri_ones, ld2[cr]) + _dot(tri_ones, ld3[cr])
            gc_t = gc.T
            for h in range(DN_HEADS):
                q_l.append(_l2_normalize(y[cr, h * DN_HEAD_DIM:(h + 1) * DN_HEAD_DIM]) * DN_HEAD_DIM ** -0.5)
                k_l.append(_l2_normalize(y[cr, DN_W + h * DN_HEAD_DIM:DN_W + (h + 1) * DN_HEAD_DIM]))
                v_l.append(y[cr, 2 * DN_W + h * DN_HEAD_DIM:2 * DN_W + (h + 1) * DN_HEAD_DIM])
                gcol_l.append(gc[:, SM_A + h:SM_A + h + 1])
                grow_l.append(gc_t[SM_A + h:SM_A + h + 1, :])
                gend_l.append(gc[c - 1:c, SM_A + h:SM_A + h + 1])
                bcol_l.append(beta[cr, SM_B + h:SM_B + h + 1])
        qs, ks, vs = jnp.stack(q_l), jnp.stack(k_l), jnp.stack(v_l)
        g_col, g_row, g_end, b_col = jnp.stack(gcol_l), jnp.stack(grow_l), jnp.stack(gend_l), jnp.stack(bcol_l)

        decay = jnp.exp(jnp.where(causal, g_col - g_row, NEG_INF))
        qb = qs.astype(BF16)
        kb = ks.astype(BF16)
        n_mat = jnp.where(strict, b_col * _bmm_nt(kb, kb) * decay, 0.0)
        inv = eye - jnp.where(split == 1, n_mat, 0.0)
        for lvl in range(1, DN_LEVELS):
            n_lvl = jnp.where(jnp.right_shift(split, lvl) == 1, n_mat, 0.0).astype(BF16)
            inv_b = inv.astype(BF16)
            inv = inv - _bmm(inv_b, _bmm(n_lvl, inv_b).astype(BF16))
        e_col = jnp.exp(g_col)
        rhs = jnp.concatenate([vs * b_col, ks * (b_col * e_col)], axis=2).astype(BF16)
        sol = _bmm(inv.astype(BF16), rhs)
        u = sol[:, :, :DN_HEAD_DIM]
        w = sol[:, :, DN_HEAD_DIM:].astype(BF16)
        qk = (_bmm_nt(qb, kb) * decay).astype(BF16)
        q_dec = (qs * e_col).astype(BF16)
        k_dec = ks * jnp.exp(g_end - g_col)
        k_dec_t = jnp.stack([k_dec[n].T for n in range(DN_GROUP * DN_HEADS)]).astype(BF16)
        g_chunk = jnp.exp(g_end)

        state = state_ref[...]
        for k in range(DN_GROUP):
            hs = slice(k * DN_HEADS, (k + 1) * DN_HEADS)
            state_b = state.astype(BF16)
            v_new = (u[hs] - _bmm(w[hs], state_b)).astype(BF16)
            o = _bmm(q_dec[hs], state_b) + _bmm(qk[hs], v_new)
            state = state * g_chunk[hs] + _bmm(k_dec_t[hs], v_new)
            out_rows = pl.ds(r0 + k * c, c)
            for h in range(DN_HEADS):
                hd = slice(h * DN_HEAD_DIM, (h + 1) * DN_HEAD_DIM)
                zh = z_ref[out_rows, hd].astype(F32)
                o_ref[out_rows, hd] = (_rms(o[h], gn_ref[...]) * _silu(zh)).astype(o_ref.dtype)
        state_ref[...] = state
        return carry

    lax.fori_loop(0, n_chunks // DN_GROUP, group, 0)


def _dn(dn, z, sm, conv_w, a_log, dt_bias, out_gain, batch, seq):
    assert DN_CHUNK == 1 << DN_LEVELS and seq % (DN_CHUNK * DN_GROUP) == 0
    pad = jnp.zeros((1, LANES), F32)
    return pl.pallas_call(
        _dn_kernel,
        grid=(batch,),
        in_specs=[pl.BlockSpec((seq, 3 * DN_W), lambda b: (b, 0)), pl.BlockSpec((seq, DN_W), lambda b: (b, 0)),
                  pl.BlockSpec((seq, LANES), lambda b: (b, 0)), _const_spec((DN_CONV, 3 * DN_W)),
                  _const_spec((1, LANES)), _const_spec((1, LANES)), _const_spec((1, DN_HEAD_DIM))],
        out_specs=pl.BlockSpec((seq, DN_W), lambda b: (b, 0)),
        out_shape=jax.ShapeDtypeStruct((batch * seq, DN_W), BF16),
        scratch_shapes=[pltpu.VMEM((DN_HEADS, DN_HEAD_DIM, DN_HEAD_DIM), F32)],
        compiler_params=_params("parallel"),
        name="dn",
    )(dn, z, sm, conv_w.astype(F32), pad.at[0, SM_A:SM_A + DN_HEADS].set(a_log.astype(F32)),
      pad.at[0, SM_A:SM_A + DN_HEADS].set(dt_bias.astype(F32)), out_gain.reshape(1, -1).astype(F32))


def _memkv_kernel(mem_ref, g_ref, w_ref, k_ref, v_ref):
    mn = _rms(mem_ref[0], g_ref[...]).astype(BF16)
    k_ref[0] = _dot(mn, w_ref[:, :X_W]).astype(k_ref.dtype)
    v_ref[0] = _dot(mn, w_ref[:, X_W:]).astype(v_ref.dtype)


def _memkv(mem, gain, w_kv):
    batch, n_mem, _ = mem.shape
    out = pl.BlockSpec((1, n_mem, X_W), lambda b: (b, 0, 0))
    shape = jax.ShapeDtypeStruct((batch, n_mem, X_W), BF16)
    return pl.pallas_call(
        _memkv_kernel,
        grid=(batch,),
        in_specs=[pl.BlockSpec((1, n_mem, D_MODEL), lambda b: (b, 0, 0)), _const_spec((1, D_MODEL)),
                  _const_spec((D_MODEL, 2 * X_W))],
        out_specs=[out, out],
        out_shape=[shape, shape],
        compiler_params=_params("parallel"),
        name="memkv",
    )(mem, gain.reshape(1, -1), w_kv.astype(BF16))


def _memattn_kernel(q_ref, k_ref, v_ref, o_ref):
    for h in range(X_HEADS):
        hd = slice(h * X_HEAD_DIM, (h + 1) * X_HEAD_DIM)
        s = _dot_nt(q_ref[:, hd], k_ref[0, :, hd])
        e = jnp.exp(s - jnp.max(s, axis=-1, keepdims=True))
        p = e / jnp.sum(e, axis=-1, keepdims=True)
        o_ref[:, hd] = _dot(p.astype(BF16), v_ref[0, :, hd]).astype(o_ref.dtype)


def _memattn(mq, mk, mv, batch, seq):
    nq = seq // TQ_MEM
    n_mem = mk.shape[1]
    kv = pl.BlockSpec((1, n_mem, X_W), lambda b, i: (b, 0, 0))
    row = pl.BlockSpec((TQ_MEM, X_W), lambda b, i: (b * nq + i, 0))
    return pl.pallas_call(
        _memattn_kernel,
        grid=(batch, nq),
        in_specs=[row, kv, kv],
        out_specs=row,
        out_shape=jax.ShapeDtypeStruct((batch * seq, X_W), BF16),
        compiler_params=_params("parallel", "arbitrary"),
        name="memattn",
    )(mq, mk, mv)


def _merge_kernel(x_ref, on_ref, od_ref, om_ref, gpre_ref, wbg_ref, wbn_ref, wbd_ref, wbm_ref, wo_ref, gpost_ref,
                  o_ref):
    x = x_ref[...]
    u = _rms(x, gpre_ref[...]).astype(BF16)
    merged = jnp.zeros(x.shape, F32)
    for i, (br_ref, w_ref) in enumerate(((on_ref, wbn_ref), (od_ref, wbd_ref), (om_ref, wbm_ref))):
        gates = jax.nn.sigmoid(_dot(u, wbg_ref[:, i * D_MODEL:(i + 1) * D_MODEL]))
        merged = merged + gates * _dot(br_ref[...], w_ref[...])
    y = _dot(merged.astype(BF16), wo_ref[...])
    o_ref[...] = x + _rms(y, gpost_ref[...])


def _merge(x, o_nsa, o_dn, o_mem, gpre, wbg, wbn, wbd, wbm, wo, gpost):
    t = x.shape[0]
    row = pl.BlockSpec((TM, D_MODEL), lambda i: (i, 0))
    br = pl.BlockSpec((TM, NSA_Q), lambda i: (i, 0))
    return pl.pallas_call(
        _merge_kernel,
        grid=(t // TM,),
        in_specs=[row, br, br, br, _const_spec((1, D_MODEL)), _const_spec((D_MODEL, N_BRANCH * D_MODEL)),
                  _const_spec((NSA_Q, D_MODEL)), _const_spec((DN_W, D_MODEL)), _const_spec((X_W, D_MODEL)),
                  _const_spec((D_MODEL, D_MODEL)), _const_spec((1, D_MODEL))],
        out_specs=row,
        out_shape=jax.ShapeDtypeStruct((t, D_MODEL), F32),
        compiler_params=_params("parallel"),
        name="merge",
    )(x, o_nsa, o_dn, o_mem, gpre.reshape(1, -1), wbg.astype(BF16), wbn.astype(BF16), wbd.astype(BF16),
      wbm.astype(BF16), wo.astype(BF16), gpost.reshape(1, -1))


def kernel(x, mem, ffn1_pre_norm, ffn1_w_gate, ffn1_w_up, ffn1_w_down, ffn1_post_norm, mix_pre_norm, w_in, cmp_pos_k, cmp_pos_v, cmp_k_w1, cmp_k_w2, cmp_v_w1, cmp_v_w2, rel_bias, dn_conv_w, dn_a_log, dn_dt_bias, dn_out_norm, mem_norm, w_mem_kv, w_branch_nsa, w_branch_dn, w_branch_mem, w_branch_gate, w_out, mix_post_norm, ffn2_pre_norm, ffn2_w_gate, ffn2_w_up, ffn2_w_down, ffn2_post_norm):
    batch, seq, d = x.shape
    xt = x.reshape(batch * seq, d)
    cb, tabs, to_sel_t = _nsa_tables(rel_bias, seq)
    for l in range(ffn1_pre_norm.shape[0]):
        xt = _ffn(xt, ffn1_pre_norm[l], ffn1_w_gate[l], ffn1_w_up[l], ffn1_w_down[l], ffn1_post_norm[l])

        q, kc, vc, kvsw, dn, z, mq, sm = _proj(xt, mix_pre_norm[l], _arrange_w_in(w_in[l]))
        posk, w1k, w2k = _arrange_cmp(cmp_pos_k[l], cmp_k_w1[l], cmp_k_w2[l])
        posv, w1v, w2v = _arrange_cmp(cmp_pos_v[l], cmp_v_w1[l], cmp_v_w2[l])
        k_cmp, v_cmp = _cmp(kc, vc, posk, posv, w1k, w1v, w2k, w2v, batch, seq)
        o_nsa = _nsa(q, kvsw, k_cmp, v_cmp, sm, cb, tabs, to_sel_t, batch, seq)
        o_dn = _dn(dn, z, sm, dn_conv_w[l], dn_a_log[l], dn_dt_bias[l], dn_out_norm[l], batch, seq)
        mk, mv = _memkv(mem, mem_norm[l], w_mem_kv[l])
        o_mem = _memattn(mq, mk, mv, batch, seq)
        xt = _merge(xt, o_nsa, o_dn, o_mem, mix_pre_norm[l], w_branch_gate[l], w_branch_nsa[l], w_branch_dn[l],
                    w_branch_mem[l], w_out[l], mix_post_norm[l])

        xt = _ffn(xt, ffn2_pre_norm[l], ffn2_w_gate[l], ffn2_w_up[l], ffn2_w_down[l], ffn2_post_norm[l])
    return xt.reshape(batch, seq, d)
```

```python
import math

import jax
import jax.numpy as jnp
import numpy as np
from jax import lax
from jax.experimental import pallas as pl
from jax.experimental.pallas import tpu as pltpu

F32 = jnp.float32
BF16 = jnp.bfloat16

D_MODEL = 1024
D_FF = 2816
NORM_EPS = 1e-6
NEG_INF = -1e30
FORCE_SCORE = 1e4

NSA_HEADS = 8
NSA_GROUPS = 2
NSA_HPG = NSA_HEADS // NSA_GROUPS
NSA_HEAD_DIM = 64
CMP_BLOCK = 32
CMP_STRIDE = 16
CMP_HIDDEN = 256
SEL_BLOCK = 64
SEL_SHIFT = 6
SEL_TOPK = 16
WINDOW = 512
REL_BUCKETS = 32
REL_MAX_DIST = 128

DN_HEADS = 4
DN_HEAD_DIM = 128
DN_CONV = 4
X_HEADS = 4
X_HEAD_DIM = 128
N_BRANCH = 3

NSA_Q = NSA_HEADS * NSA_HEAD_DIM
NSA_KV = NSA_GROUPS * NSA_HEAD_DIM
DN_W = DN_HEADS * DN_HEAD_DIM
X_W = X_HEADS * X_HEAD_DIM

LANES = 128
VMEM_LIMIT = 56 * 1024 * 1024

TM = 1024
FF_CHUNK = 256
TQ = 256
NSA_ACC_ROWS = 80
NSA_WAVE = 4
DN_CHUNK = 128
DN_LEVELS = 7
DN_GROUP = 4
CONV_HALO = 16
TQ_MEM = 512

SM_GATE = 0
SM_A = N_BRANCH * NSA_HEADS
SM_B = SM_A + DN_HEADS


def _dot(a, b):
    return jnp.dot(a, b, preferred_element_type=F32)


def _dot_nt(a, b):
    return lax.dot_general(a, b, (((1,), (1,)), ((), ())), preferred_element_type=F32)


def _bmm(a, b):
    return lax.dot_general(a, b, (((2,), (1,)), ((0,), (0,))), preferred_element_type=F32)


def _bmm_nt(a, b):
    return lax.dot_general(a, b, (((2,), (2,)), ((0,), (0,))), preferred_element_type=F32)


def _rms(x, gain):
    return x * lax.rsqrt(jnp.mean(x * x, axis=-1, keepdims=True) + NORM_EPS) * gain


def _silu(x):
    return x * jax.nn.sigmoid(x)


def _const_spec(shape):
    zeros = (0,) * len(shape)
    return pl.BlockSpec(shape, lambda *_: zeros, pipeline_mode=pl.Buffered(1))


def _params(*sem):
    return pltpu.CompilerParams(dimension_semantics=sem, vmem_limit_bytes=VMEM_LIMIT)


def _ffn_kernel(x_ref, pre_ref, wg_ref, wu_ref, wd_ref, post_ref, o_ref):
    x = x_ref[...]
    xn = _rms(x, pre_ref[...]).astype(BF16)
    acc = jnp.zeros(x.shape, F32)
    for c in range(D_FF // FF_CHUNK):
        cols = slice(c * FF_CHUNK, (c + 1) * FF_CHUNK)
        gate = _dot(xn, wg_ref[:, cols])
        up = _dot(xn, wu_ref[:, cols])
        acc = acc + _dot((_silu(gate) * up).astype(BF16), wd_ref[cols, :])
    o_ref[...] = x + 0.5 * _rms(acc, post_ref[...])


def _ffn(x, pre, wg, wu, wd, post):
    t = x.shape[0]
    row = pl.BlockSpec((TM, D_MODEL), lambda i: (i, 0))
    return pl.pallas_call(
        _ffn_kernel,
        grid=(t // TM,),
        in_specs=[row, _const_spec((1, D_MODEL)), _const_spec((D_MODEL, D_FF)), _const_spec((D_MODEL, D_FF)),
                  _const_spec((D_FF, D_MODEL)), _const_spec((1, D_MODEL))],
        out_specs=row,
        out_shape=jax.ShapeDtypeStruct((t, D_MODEL), F32),
        compiler_params=_params("parallel"),
        name="ffn",
    )(x, pre.reshape(1, -1), wg.astype(BF16), wu.astype(BF16), wd.astype(BF16), post.reshape(1, -1))


PROJ_SPLITS = (("q", NSA_Q), ("kc", NSA_KV), ("vc", NSA_KV), ("kvsw", 4 * NSA_KV), ("dn", 3 * DN_W),
               ("z", DN_W), ("mq", X_W), ("sm", LANES))
PROJ_WIDTH = sum(w for _, w in PROJ_SPLITS)
LOG2E = math.log2(math.e)
PROJ_SCALES = {"q": NSA_HEAD_DIM ** -0.5 * LOG2E, "mq": X_HEAD_DIM ** -0.5}


def _proj_kernel(x_ref, g_ref, w_ref, *out_refs):
    xn = _rms(x_ref[...], g_ref[...]).astype(BF16)
    start = 0
    for (name, width), ref in zip(PROJ_SPLITS, out_refs):
        for c0 in range(0, width, 512):
            cw = min(512, width - c0)
            y = _dot(xn, w_ref[:, start + c0:start + c0 + cw])
            if name in PROJ_SCALES:
                y = y * PROJ_SCALES[name]
            ref[:, c0:c0 + cw] = y.astype(ref.dtype)
        start += width


def _arrange_w_in(w_in):
    o_kv = NSA_Q
    o_g = o_kv + 6 * NSA_KV
    o_dn = o_g + N_BRANCH * NSA_HEADS
    o_a = o_dn + 3 * DN_W
    o_z = o_a + 2 * DN_HEADS
    o_mq = o_z + DN_W
    kv = w_in[:, o_kv:o_g].reshape(D_MODEL, 6, NSA_GROUPS, NSA_HEAD_DIM)
    kvsw = kv[:, 2:6].transpose(0, 2, 1, 3).reshape(D_MODEL, 4 * NSA_KV)
    small = jnp.concatenate([w_in[:, o_g:o_dn], w_in[:, o_a:o_z],
                             jnp.zeros((D_MODEL, LANES - N_BRANCH * NSA_HEADS - 2 * DN_HEADS), w_in.dtype)], axis=1)
    return jnp.concatenate([w_in[:, :o_kv], kv[:, 0].reshape(D_MODEL, NSA_KV), kv[:, 1].reshape(D_MODEL, NSA_KV),
                            kvsw, w_in[:, o_dn:o_a], w_in[:, o_z:o_mq], w_in[:, o_mq:], small], axis=1).astype(BF16)


def _proj(x, gain, w_arranged):
    t = x.shape[0]
    outs, specs = [], []
    for name, width in PROJ_SPLITS:
        outs.append(jax.ShapeDtypeStruct((t, width), F32 if name == "sm" else BF16))
        specs.append(pl.BlockSpec((TM, width), lambda i: (i, 0)))
    return pl.pallas_call(
        _proj_kernel,
        grid=(t // TM,),
        in_specs=[pl.BlockSpec((TM, D_MODEL), lambda i: (i, 0)), _const_spec((1, D_MODEL)),
                  _const_spec((D_MODEL, PROJ_WIDTH))],
        out_specs=specs,
        out_shape=outs,
        compiler_params=_params("parallel"),
        name="proj",
    )(x, gain.reshape(1, -1), w_arranged)


CMP_ROW = CMP_STRIDE * NSA_KV


def _cmp_kernel(kc_ref, vc_ref, posk_ref, posv_ref, w1k_ref, w1v_ref, w2k_ref, w2v_ref, ko_ref, vo_ref):
    n_rows = kc_ref.shape[1]

    def one(src_ref, pos_ref, w1_ref, w2_ref, out_ref):
        src = src_ref[0].astype(F32)
        first = _dot((src + pos_ref[0:1, :]).astype(BF16), w1_ref[0])
        second = _dot((src + pos_ref[1:2, :]).astype(BF16), w1_ref[1])
        hidden = _silu(first + pltpu.roll(second, n_rows - 1, 0))
        out = _dot(hidden.astype(BF16), w2_ref[...])
        for g in range(NSA_GROUPS):
            out_ref[0, g] = out[:, g * NSA_HEAD_DIM:(g + 1) * NSA_HEAD_DIM].astype(out_ref.dtype)

    one(kc_ref, posk_ref, w1k_ref, w2k_ref, ko_ref)
    one(vc_ref, posv_ref, w1v_ref, w2v_ref, vo_ref)


def _arrange_cmp(pos, w1, w2):
    hid2 = NSA_GROUPS * CMP_HIDDEN
    w1r = w1.reshape(2, CMP_STRIDE, NSA_HEAD_DIM, CMP_HIDDEN)
    eye = jnp.eye(NSA_GROUPS, dtype=w1.dtype)
    w1g = jnp.einsum("hidk,ge->higdek", w1r, eye).reshape(2, CMP_ROW, hid2).astype(BF16)
    w2g = jnp.einsum("kd,ge->gked", w2, eye).reshape(hid2, NSA_KV).astype(BF16)
    posr = jnp.broadcast_to(pos.reshape(2, CMP_STRIDE, 1, NSA_HEAD_DIM),
                            (2, CMP_STRIDE, NSA_GROUPS, NSA_HEAD_DIM)).reshape(2, CMP_ROW).astype(F32)
    return posr, w1g, w2g


def _cmp(kc, vc, posk, posv, w1k, w1v, w2k, w2v, batch, seq):
    n_rows = seq // CMP_STRIDE
    src = pl.BlockSpec((1, n_rows, CMP_ROW), lambda b: (b, 0, 0))
    out = pl.BlockSpec((1, NSA_GROUPS, n_rows, NSA_HEAD_DIM), lambda b: (b, 0, 0, 0))
    out_shape = jax.ShapeDtypeStruct((batch, NSA_GROUPS, n_rows, NSA_HEAD_DIM), BF16)
    hid2 = NSA_GROUPS * CMP_HIDDEN
    return pl.pallas_call(
        _cmp_kernel,
        grid=(batch,),
        in_specs=[src, src, _const_spec((2, CMP_ROW)), _const_spec((2, CMP_ROW)),
                  _const_spec((2, CMP_ROW, hid2)), _const_spec((2, CMP_ROW, hid2)),
                  _const_spec((hid2, NSA_KV)), _const_spec((hid2, NSA_KV))],
        out_specs=[out, out],
        out_shape=[out_shape, out_shape],
        compiler_params=_params("parallel"),
        name="cmp",
    )(kc.reshape(batch, n_rows, CMP_ROW), vc.reshape(batch, n_rows, CMP_ROW), posk, posv, w1k, w1v, w2k, w2v)


def _rel_bucket_table(n):
    exact = REL_BUCKETS // 2
    dist = np.arange(n)
    d = np.maximum(dist, 1).astype(np.float32)
    log_bucket = exact + (np.log(d / np.float32(exact)) / np.float32(math.log(REL_MAX_DIST / exact))
                          * np.float32(REL_BUCKETS - exact)).astype(np.int32)
    return np.where(dist < exact, dist, np.minimum(log_bucket, REL_BUCKETS - 1))


def _nsa_tables(rel_bias, seq):
    bucket = _rel_bucket_table(seq + 2 * TQ)
    n_cmp_pad = seq // CMP_STRIDE
    bias = rel_bias.astype(F32).T * LOG2E
    neg = jnp.full((NSA_HEADS, 1), NEG_INF, F32)

    def by_distance(values, dist):
        cls = np.where(dist >= 0, bucket[np.maximum(dist, 0)], REL_BUCKETS).astype(np.int32)
        one_hot = (jnp.asarray(cls.reshape(1, -1))
                   == lax.broadcasted_iota(jnp.int32, (REL_BUCKETS + 1, 1), 0)).astype(F32)
        picked = jnp.dot(jnp.concatenate([values, neg], axis=1), one_hot, precision=lax.Precision.HIGHEST)
        return picked.reshape((NSA_GROUPS, NSA_HPG) + dist.shape)

    t = np.arange(seq).reshape(seq // TQ, 1, TQ)
    dist = t - (np.arange(n_cmp_pad)[None, :, None] * CMP_STRIDE + CMP_BLOCK - 1)
    cb = by_distance(bias, dist).transpose(0, 2, 3, 1, 4).reshape(NSA_GROUPS, seq // TQ, n_cmp_pad, NSA_HPG * TQ)
    shifted = bias - bias[:, REL_BUCKETS - 1:]
    q_minus_k = np.arange(TQ)[None, :] - np.arange(TQ)[:, None]
    t0 = by_distance(shifted, q_minus_k)
    t1 = by_distance(shifted, TQ + q_minus_k)
    w2 = jnp.asarray(np.where(q_minus_k < 0, 0.0, NEG_INF), F32)
    tabs = jnp.stack([t0, t1, jnp.broadcast_to(w2, t0.shape)], axis=1)
    tabs = tabs.reshape(NSA_GROUPS, 3, NSA_HPG * TQ, TQ)
    c0 = np.arange(n_cmp_pad)[None, :] * CMP_STRIDE
    s0 = np.arange(LANES)[:, None] * SEL_BLOCK
    overlap = np.maximum(np.minimum(c0 + CMP_BLOCK, s0 + SEL_BLOCK) - np.maximum(c0, s0), 0)
    valid = (np.arange(LANES)[:, None] < seq // SEL_BLOCK) & (np.arange(n_cmp_pad)[None, :] < n_cmp_pad - 1)
    to_sel_t = jnp.asarray(np.where(valid, overlap / CMP_BLOCK, 0.0), BF16)
    return cb, tabs, to_sel_t


SEL, WIN = 0, 1


def _nsa_kernel(q_ref, kv_ref, kc_ref, vc_ref, sm_ref, cb_ref, tab_ref, tosel_ref, o_ref, m_ref, acc_ref):
    qi = pl.program_id(1)
    n_sel_blocks = kv_ref.shape[0] // SEL_BLOCK
    sm = sm_ref[...]
    ones_cols = jnp.ones((TQ, NSA_HEAD_DIM), BF16)
    heads = range(NSA_HPG)
    groups = range(NSA_GROUPS)

    def head_rows(j):
        return slice(j * TQ, (j + 1) * TQ)

    def flash_step(streams, first=False):
        chains = []
        for si, (branch, g, q_list, k_t, v_t, add_fn) in enumerate(streams):
            v_ext_t = jnp.concatenate([v_t, ones_cols], axis=1).astype(F32).T[:NSA_ACC_ROWS].astype(BF16)
            chains += [(si, branch, g, j, (q_list[j], k_t), v_ext_t, add_fn) for j in heads]
        waves = [chains[i:i + NSA_WAVE] for i in range(0, len(chains), NSA_WAVE)]
        s, m_old, m_new, p = {}, {}, {}, {}
        for step in range(len(waves) + 2):
            for si, _, _, j, (q, k_t), _, add_fn in (waves[step] if step < len(waves) else ()):
                s[si, j] = _dot_nt(k_t, q) if add_fn is None else _dot_nt(k_t, q) + add_fn(j)
            for si, branch, g, j, _, _, _ in (waves[step - 1] if 0 < step <= len(waves) else ()):
                c = (si, j)
                m_new[c] = jnp.broadcast_to(jnp.max(s[c], axis=0, keepdims=True), (8, TQ))
                if not first:
                    m_old[c] = m_ref[branch, g, j]
                    m_new[c] = jnp.maximum(m_old[c], m_new[c])
                m_ref[branch, g, j] = m_new[c]
                p[c] = jnp.exp2((s[c] - m_new[c][0:1]).astype(BF16))
            for si, branch, g, j, _, v_ext_t, _ in (waves[step - 2] if step >= 2 else ()):
                c = (si, j)
                acc = _dot(v_ext_t, p[c])
                if not first:
                    acc = jnp.exp2(m_old[c] - m_new[c])[0:1] * acc_ref[branch, g, j] + acc
                acc_ref[branch, g, j] = acc

    def head_cols(g, j):
        c0 = (g * NSA_HPG + j) * NSA_HEAD_DIM
        return slice(c0, c0 + NSA_HEAD_DIM)

    q_heads = [[q_ref[:, head_cols(g, j)] for j in heads] for g in groups]

    def kv_tile(g, kt, which):
        c0 = (g * 4 + which) * NSA_HEAD_DIM
        return kv_ref[pl.ds(pl.multiple_of(kt * TQ, TQ), TQ), c0:c0 + NSA_HEAD_DIM]

    def compressed(g):
        cols = NSA_HPG * TQ
        tok = qi * TQ + (lax.broadcasted_iota(jnp.int32, (1, cols), 1) & (TQ - 1))
        s = _dot_nt(kc_ref[0, g], jnp.concatenate(q_heads[g], axis=0)) + cb_ref[g, 0]
        e = jnp.exp2(s - jnp.max(s, axis=0, keepdims=True))
        p = e * (jnp.where(tok >= CMP_BLOCK - 1, 1.0, 0.0) / jnp.sum(e, axis=0, keepdims=True))
        vc_pad = jnp.concatenate([vc_ref[0, g], jnp.zeros((vc_ref.shape[2], NSA_HEAD_DIM), BF16)], axis=1)
        out_t = _dot(vc_pad.astype(F32).T.astype(BF16), p.astype(BF16))
        p_sum = p[:, 0:TQ]
        for j in range(1, NSA_HPG):
            p_sum = p_sum + p[:, j * TQ:(j + 1) * TQ]
        return out_t, p_sum

    def dropped_blocks(p_sum):
        p_hi = p_sum.astype(BF16)
        p_lo = (p_sum - p_hi.astype(F32)).astype(BF16)
        imp_t = _dot(tosel_ref[...], p_hi) + _dot(tosel_ref[...], p_lo)
        imp_t = imp_t[0:n_sel_blocks]
        blk = lax.broadcasted_iota(jnp.int32, (n_sel_blocks, TQ), 0)
        cur = jnp.right_shift(qi * TQ + lax.broadcasted_iota(jnp.int32, (n_sel_blocks, TQ), 1), SEL_SHIFT)
        score = jnp.where(blk <= cur, imp_t, -FORCE_SCORE)
        for forced_blk in (0, cur, cur - 1):
            score = jnp.where(blk == forced_blk, FORCE_SCORE, score)
        ranks = []
        for r0 in range(0, n_sel_blocks, 8):
            mine = score[r0:r0 + 8]
            row = r0 + lax.broadcasted_iota(jnp.int32, (8, TQ), 0)
            rank = jnp.zeros(mine.shape, F32)
            for m_blk in range(n_sel_blocks):
                other = score[m_blk:m_blk + 1, :]
                if m_blk < r0:
                    rank = jnp.where(other >= mine, rank + 1.0, rank)
                elif m_blk >= r0 + 8:
                    rank = jnp.where(other > mine, rank + 1.0, rank)
                else:
                    tie = jnp.where(row > m_blk, 1.0, 0.0)
                    rank = rank + jnp.where(other > mine, 1.0, jnp.where(other == mine, tie, 0.0))
            ranks.append(rank)
        rank = jnp.concatenate(ranks, axis=0)
        drop_t = jnp.where(rank < SEL_TOPK, jnp.where(score > -1.0, 0.0, 1.0), 1.0)
        drop_t = jnp.concatenate([drop_t, jnp.ones((LANES - n_sel_blocks, TQ), F32)], axis=0)
        return drop_t.T.astype(BF16)

    def sel_stream(q_aug, g, kt, tab):
        key_blk = jnp.right_shift(kt * TQ + lax.broadcasted_iota(jnp.int32, (TQ, NSA_HEAD_DIM), 0), SEL_SHIFT)
        hit = lax.broadcasted_iota(jnp.int32, (TQ, NSA_HEAD_DIM), 1) == key_blk
        k_aug = jnp.concatenate([kv_tile(g, kt, 0), jnp.where(hit, NEG_INF, 0.0).astype(BF16)], axis=1)
        add_fn = None if tab is None else (lambda j: tab_ref[g, tab, head_rows(j), :])
        return SEL, g, q_aug[g], k_aug, kv_tile(g, kt, 1), add_fn

    def win_stream(g, kt, tab):
        return WIN, g, q_heads[g], kv_tile(g, kt, 2), kv_tile(g, kt, 3), lambda j: tab_ref[g, tab, head_rows(j), :]

    def whole_step(n_near):
        flash_step([win_stream(g, qi, 0) for g in groups], first=True)
        cmp_out = [compressed(g) for g in groups]
        flash_step([win_stream(g, qi - 1 - n, 1 + n) for n in range(n_near) for g in groups])
        drop = [dropped_blocks(cmp_out[g][1]) for g in groups]
        q_aug = [[jnp.concatenate([q_heads[g][j], drop[g][:, :NSA_HEAD_DIM]], axis=1) for j in heads]
                 for g in groups]
        flash_step([sel_stream(q_aug, g, qi, 0) for g in groups], first=True)
        flash_step([sel_stream(q_aug, g, qi - 1 - n, 1 if n == 0 else None) for n in range(n_near) for g in groups])
        if n_near == 2:
            n_far = qi - 2

            @pl.loop(0, n_far // 2)
            def _(pair):
                flash_step([sel_stream(q_aug, g, 2 * pair + n, None) for n in range(2) for g in groups])

            @pl.when(n_far % 2 == 1)
            def _():
                flash_step([sel_stream(q_aug, g, n_far - 1, None) for g in groups])

        gates_t = jax.nn.sigmoid(sm).T
        for g in groups:
            for j in heads:
                def gate(branch):
                    c = SM_GATE + branch * NSA_HEADS + g * NSA_HPG + j
                    return gates_t[c:c + 1, :]

                def weighted(branch, gate_row):
                    acc = acc_ref[branch, g, j]
                    return acc[:NSA_HEAD_DIM] * (gate_row / acc[NSA_HEAD_DIM:NSA_HEAD_DIM + 1, :])

                o_t = (gate(0) * cmp_out[g][0][:NSA_HEAD_DIM, head_rows(j)] + weighted(SEL, gate(1))
                       + weighted(WIN, gate(2)))
                o_t = jnp.concatenate([o_t, jnp.zeros_like(o_t)], axis=0)
                o_ref[:, head_cols(g, j)] = o_t.T[:, :NSA_HEAD_DIM].astype(o_ref.dtype)

    pl.when(qi == 0)(lambda: whole_step(0))
    pl.when(qi == 1)(lambda: whole_step(1))
    pl.when(qi >= 2)(lambda: whole_step(2))


def _nsa(q, kvsw, k_cmp, v_cmp, sm, cb, tabs, to_sel_t, batch, seq):
    assert WINDOW == 2 * TQ and seq % TQ == 0
    nq = seq // TQ
    n_cmp_pad = seq // CMP_STRIDE
    rows = NSA_HPG * TQ
    cmp_spec = pl.BlockSpec((1, NSA_GROUPS, n_cmp_pad, NSA_HEAD_DIM), lambda b, i: (b, 0, 0, 0))
    return pl.pallas_call(
        _nsa_kernel,
        grid=(batch, nq),
        in_specs=[
            pl.BlockSpec((TQ, NSA_Q), lambda b, i: (b * nq + i, 0)),
            pl.BlockSpec((seq, 4 * NSA_KV), lambda b, i: (b, 0)),
            cmp_spec, cmp_spec,
            pl.BlockSpec((TQ, LANES), lambda b, i: (b * nq + i, 0)),
            pl.BlockSpec((NSA_GROUPS, 1, n_cmp_pad, rows), lambda b, i: (0, i, 0, 0)),
            _const_spec((NSA_GROUPS, 3, rows, TQ)),
            _const_spec((LANES, n_cmp_pad)),
        ],
        out_specs=pl.BlockSpec((TQ, NSA_Q), lambda b, i: (b * nq + i, 0)),
        out_shape=jax.ShapeDtypeStruct((batch * seq, NSA_Q), BF16),
        scratch_shapes=[pltpu.VMEM((2, NSA_GROUPS, NSA_HPG, 8, TQ), F32),
                        pltpu.VMEM((2, NSA_GROUPS, NSA_HPG, NSA_ACC_ROWS, TQ), F32)],
        compiler_params=_params("parallel", "arbitrary"),
        name="nsa",
    )(q, kvsw, k_cmp, v_cmp, sm, cb, tabs, to_sel_t)


def _softplus(x):
    return jnp.maximum(x, 0.0) + jnp.log(1.0 + jnp.exp(-jnp.abs(x)))


def _l2_normalize(x):
    return x * lax.rsqrt(jnp.sum(x * x, axis=-1, keepdims=True) + NORM_EPS)


def _dn_kernel(dn_ref, z_ref, sm_ref, cw_ref, alog_ref, dtb_ref, gn_ref, o_ref, state_ref):
    c = DN_CHUNK
    n_chunks = dn_ref.shape[0] // c
    ii = lax.broadcasted_iota(jnp.int32, (c, c), 0)
    jj = lax.broadcasted_iota(jnp.int32, (c, c), 1)
    causal = ii >= jj
    strict = ii > jj
    split = ii ^ jj
    tri_ones = jnp.where(causal, 1.0, 0.0).astype(BF16)
    eye = jnp.where(ii == jj, 1.0, 0.0)
    state_ref[...] = jnp.zeros(state_ref.shape, F32)

    span = c * DN_GROUP

    def group(gi, carry):
        r0 = pl.multiple_of(gi * span, span)
        rows = pl.ds(r0, span)
        x = dn_ref[rows, :].astype(F32)
        halo_rows = pl.ds(pl.multiple_of(jnp.maximum(r0 - CONV_HALO, 0), CONV_HALO), CONV_HALO)
        halo = dn_ref[halo_rows, :].astype(F32) * jnp.where(gi > 0, 1.0, 0.0)
        xc = jnp.concatenate([halo, x], axis=0)
        y = x * cw_ref[DN_CONV - 1:DN_CONV, :]
        for k in range(1, DN_CONV):
            y = y + pltpu.roll(xc, k, 0)[CONV_HALO:, :] * cw_ref[DN_CONV - 1 - k:DN_CONV - k, :]
        y = _silu(y)

        smc = sm_ref[rows, :]
        log_decay = -jnp.exp(alog_ref[...]) * _softplus(smc + dtb_ref[...])
        beta = jax.nn.sigmoid(smc)
        ld1 = log_decay.astype(BF16)
        rest = log_decay - ld1.astype(F32)
        ld2 = rest.astype(BF16)
        ld3 = (rest - ld2.astype(F32)).astype(BF16)

        q_l, k_l, v_l, gcol_l, grow_l, gend_l, bcol_l = [], [], [], [], [], [], []
        for k in range(DN_GROUP):
            cr = slice(k * c, (k + 1) * c)
            gc = _dot(tri_ones, ld1[cr]) + _dot(tri_ones, ld2[cr]) + _dot(tri_ones, ld3[cr])
            gc_t = gc.T
            for h in range(DN_HEADS):
                q_l.append(_l2_normalize(y[cr, h * DN_HEAD_DIM:(h + 1) * DN_HEAD_DIM]) * DN_HEAD_DIM ** -0.5)
                k_l.append(_l2_normalize(y[cr, DN_W + h * DN_HEAD_DIM:DN_W + (h + 1) * DN_HEAD_DIM]))
                v_l.append(y[cr, 2 * DN_W + h * DN_HEAD_DIM:2 * DN_W + (h + 1) * DN_HEAD_DIM])
                gcol_l.append(gc[:, SM_A + h:SM_A + h + 1])
                grow_l.append(gc_t[SM_A + h:SM_A + h + 1, :])
                gend_l.append(gc[c - 1:c, SM_A + h:SM_A + h + 1])
                bcol_l.append(beta[cr, SM_B + h:SM_B + h + 1])
        qs, ks, vs = jnp.stack(q_l), jnp.stack(k_l), jnp.stack(v_l)
        g_col, g_row, g_end, b_col = jnp.stack(gcol_l), jnp.stack(grow_l), jnp.stack(gend_l), jnp.stack(bcol_l)

        decay = jnp.exp(jnp.where(causal, g_col - g_row, NEG_INF))
        qb = qs.astype(BF16)
        kb = ks.astype(BF16)
        n_mat = jnp.where(strict, b_col * _bmm_nt(kb, kb) * decay, 0.0)
        inv = eye - jnp.where(split == 1, n_mat, 0.0)
        for lvl in range(1, DN_LEVELS):
            n_lvl = jnp.where(jnp.right_shift(split, lvl) == 1, n_mat, 0.0).astype(BF16)
            inv_b = inv.astype(BF16)
            inv = inv - _bmm(inv_b, _bmm(n_lvl, inv_b).astype(BF16))
        e_col = jnp.exp(g_col)
        rhs = jnp.concatenate([vs * b_col, ks * (b_col * e_col)], axis=2).astype(BF16)
        sol = _bmm(inv.astype(BF16), rhs)
        u = sol[:, :, :DN_HEAD_DIM]
        w = sol[:, :, DN_HEAD_DIM:].astype(BF16)
        qk = (_bmm_nt(qb, kb) * decay).astype(BF16)
        q_dec = (qs * e_col).astype(BF16)
        k_dec = ks * jnp.exp(g_end - g_col)
        k_dec_t = jnp.stack([k_dec[n].T for n in range(DN_GROUP * DN_HEADS)]).astype(BF16)
        g_chunk = jnp.exp(g_end)

        state = state_ref[...]
        for k in range(DN_GROUP):
            hs = slice(k * DN_HEADS, (k + 1) * DN_HEADS)
            state_b = state.astype(BF16)
            v_new = (u[hs] - _bmm(w[hs], state_b)).astype(BF16)
            o = _bmm(q_dec[hs], state_b) + _bmm(qk[hs], v_new)
            state = state * g_chunk[hs] + _bmm(k_dec_t[hs], v_new)
            out_rows = pl.ds(r0 + k * c, c)
            for h in range(DN_HEADS):
                hd = slice(h * DN_HEAD_DIM, (h + 1) * DN_HEAD_DIM)
                zh = z_ref[out_rows, hd].astype(F32)
                o_ref[out_rows, hd] = (_rms(o[h], gn_ref[...]) * _silu(zh)).astype(o_ref.dtype)
        state_ref[...] = state
        return carry

    lax.fori_loop(0, n_chunks // DN_GROUP, group, 0)


def _dn(dn, z, sm, conv_w, a_log, dt_bias, out_gain, batch, seq):
    assert DN_CHUNK == 1 << DN_LEVELS and seq % (DN_CHUNK * DN_GROUP) == 0
    pad = jnp.zeros((1, LANES), F32)
    return pl.pallas_call(
        _dn_kernel,
        grid=(batch,),
        in_specs=[pl.BlockSpec((seq, 3 * DN_W), lambda b: (b, 0)), pl.BlockSpec((seq, DN_W), lambda b: (b, 0)),
                  pl.BlockSpec((seq, LANES), lambda b: (b, 0)), _const_spec((DN_CONV, 3 * DN_W)),
                  _const_spec((1, LANES)), _const_spec((1, LANES)), _const_spec((1, DN_HEAD_DIM))],
        out_specs=pl.BlockSpec((seq, DN_W), lambda b: (b, 0)),
        out_shape=jax.ShapeDtypeStruct((batch * seq, DN_W), BF16),
        scratch_shapes=[pltpu.VMEM((DN_HEADS, DN_HEAD_DIM, DN_HEAD_DIM), F32)],
        compiler_params=_params("parallel"),
        name="dn",
    )(dn, z, sm, conv_w.astype(F32), pad.at[0, SM_A:SM_A + DN_HEADS].set(a_log.astype(F32)),
      pad.at[0, SM_A:SM_A + DN_HEADS].set(dt_bias.astype(F32)), out_gain.reshape(1, -1).astype(F32))


def _memkv_kernel(mem_ref, g_ref, w_ref, k_ref, v_ref):
    mn = _rms(mem_ref[0], g_ref[...]).astype(BF16)
    k_ref[0] = _dot(mn, w_ref[:, :X_W]).astype(k_ref.dtype)
    v_ref[0] = _dot(mn, w_ref[:, X_W:]).astype(v_ref.dtype)


def _memkv(mem, gain, w_kv):
    batch, n_mem, _ = mem.shape
    out = pl.BlockSpec((1, n_mem, X_W), lambda b: (b, 0, 0))
    shape = jax.ShapeDtypeStruct((batch, n_mem, X_W), BF16)
    return pl.pallas_call(
        _memkv_kernel,
        grid=(batch,),
        in_specs=[pl.BlockSpec((1, n_mem, D_MODEL), lambda b: (b, 0, 0)), _const_spec((1, D_MODEL)),
                  _const_spec((D_MODEL, 2 * X_W))],
        out_specs=[out, out],
        out_shape=[shape, shape],
        compiler_params=_params("parallel"),
        name="memkv",
    )(mem, gain.reshape(1, -1), w_kv.astype(BF16))


def _memattn_kernel(q_ref, k_ref, v_ref, o_ref):
    cols = [slice(h * X_HEAD_DIM, (h + 1) * X_HEAD_DIM) for h in range(X_HEADS)]
    s = [_dot_nt(q_ref[:, hd], k_ref[0, :, hd]) for hd in cols]
    e = [jnp.exp(x - jnp.max(x, axis=-1, keepdims=True)) for x in s]
    p = [(x / jnp.sum(x, axis=-1, keepdims=True)).astype(BF16) for x in e]
    for hd, ph in zip(cols, p):
        o_ref[:, hd] = _dot(ph, v_ref[0, :, hd]).astype(o_ref.dtype)


def _memattn(mq, mk, mv, batch, seq):
    nq = seq // TQ_MEM
    n_mem = mk.shape[1]
    kv = pl.BlockSpec((1, n_mem, X_W), lambda b, i: (b, 0, 0))
    row = pl.BlockSpec((TQ_MEM, X_W), lambda b, i: (b * nq + i, 0))
    return pl.pallas_call(
        _memattn_kernel,
        grid=(batch, nq),
        in_specs=[row, kv, kv],
        out_specs=row,
        out_shape=jax.ShapeDtypeStruct((batch * seq, X_W), BF16),
        compiler_params=_params("parallel", "arbitrary"),
        name="memattn",
    )(mq, mk, mv)


def _merge_kernel(x_ref, on_ref, od_ref, om_ref, gpre_ref, wbg_ref, wbn_ref, wbd_ref, wbm_ref, wo_ref, gpost_ref,
                  o_ref):
    x = x_ref[...]
    u = _rms(x, gpre_ref[...]).astype(BF16)
    merged = jnp.zeros(x.shape, F32)
    for i, (br_ref, w_ref) in enumerate(((on_ref, wbn_ref), (od_ref, wbd_ref), (om_ref, wbm_ref))):
        gates = jax.nn.sigmoid(_dot(u, wbg_ref[:, i * D_MODEL:(i + 1) * D_MODEL]))
        merged = merged + gates * _dot(br_ref[...], w_ref[...])
    y = _dot(merged.astype(BF16), wo_ref[...])
    o_ref[...] = x + _rms(y, gpost_ref[...])


def _merge(x, o_nsa, o_dn, o_mem, gpre, wbg, wbn, wbd, wbm, wo, gpost):
    t = x.shape[0]
    row = pl.BlockSpec((TM, D_MODEL), lambda i: (i, 0))
    br = pl.BlockSpec((TM, NSA_Q), lambda i: (i, 0))
    return pl.pallas_call(
        _merge_kernel,
        grid=(t // TM,),
        in_specs=[row, br, br, br, _const_spec((1, D_MODEL)), _const_spec((D_MODEL, N_BRANCH * D_MODEL)),
                  _const_spec((NSA_Q, D_MODEL)), _const_spec((DN_W, D_MODEL)), _const_spec((X_W, D_MODEL)),
                  _const_spec((D_MODEL, D_MODEL)), _const_spec((1, D_MODEL))],
        out_specs=row,
        out_shape=jax.ShapeDtypeStruct((t, D_MODEL), F32),
        compiler_params=_params("parallel"),
        name="merge",
    )(x, o_nsa, o_dn, o_mem, gpre.reshape(1, -1), wbg.astype(BF16), wbn.astype(BF16), wbd.astype(BF16),
      wbm.astype(BF16), wo.astype(BF16), gpost.reshape(1, -1))


def kernel(x, mem, ffn1_pre_norm, ffn1_w_gate, ffn1_w_up, ffn1_w_down, ffn1_post_norm, mix_pre_norm, w_in, cmp_pos_k, cmp_pos_v, cmp_k_w1, cmp_k_w2, cmp_v_w1, cmp_v_w2, rel_bias, dn_conv_w, dn_a_log, dn_dt_bias, dn_out_norm, mem_norm, w_mem_kv, w_branch_nsa, w_branch_dn, w_branch_mem, w_branch_gate, w_out, mix_post_norm, ffn2_pre_norm, ffn2_w_gate, ffn2_w_up, ffn2_w_down, ffn2_post_norm):
    batch, seq, d = x.shape
    xt = x.reshape(batch * seq, d)
    cb, tabs, to_sel_t = _nsa_tables(rel_bias, seq)
    for l in range(ffn1_pre_norm.shape[0]):
        xt = _ffn(xt, ffn1_pre_norm[l], ffn1_w_gate[l], ffn1_w_up[l], ffn1_w_down[l], ffn1_post_norm[l])

        q, kc, vc, kvsw, dn, z, mq, sm = _proj(xt, mix_pre_norm[l], _arrange_w_in(w_in[l]))
        posk, w1k, w2k = _arrange_cmp(cmp_pos_k[l], cmp_k_w1[l], cmp_k_w2[l])
        posv, w1v, w2v = _arrange_cmp(cmp_pos_v[l], cmp_v_w1[l], cmp_v_w2[l])
        k_cmp, v_cmp = _cmp(kc, vc, posk, posv, w1k, w1v, w2k, w2v, batch, seq)
        o_nsa = _nsa(q, kvsw, k_cmp, v_cmp, sm, cb, tabs, to_sel_t, batch, seq)
        o_dn = _dn(dn, z, sm, dn_conv_w[l], dn_a_log[l], dn_dt_bias[l], dn_out_norm[l], batch, seq)
        mk, mv = _memkv(mem, mem_norm[l], w_mem_kv[l])
        o_mem = _memattn(mq, mk, mv, batch, seq)
        xt = _merge(xt, o_nsa, o_dn, o_mem, mix_pre_norm[l], w_branch_gate[l], w_branch_nsa[l], w_branch_dn[l],
                    w_branch_mem[l], w_out[l], mix_post_norm[l])

        xt = _ffn(xt, ffn2_pre_norm[l], ffn2_w_gate[l], ffn2_w_up[l], ffn2_w_down[l], ffn2_post_norm[l])
    return xt.reshape(batch, seq, d)
```

```python
import math

import jax
import jax.numpy as jnp
import numpy as np
from jax import lax
from jax.experimental import pallas as pl
from jax.experimental.pallas import tpu as pltpu

F32 = jnp.float32
BF16 = jnp.bfloat16

D_MODEL = 1024
D_FF = 2816
NORM_EPS = 1e-6
NEG_INF = -1e30
FORCE_SCORE = 1e4

NSA_HEADS = 8
NSA_GROUPS = 2
NSA_HPG = NSA_HEADS // NSA_GROUPS
NSA_HEAD_DIM = 64
CMP_BLOCK = 32
CMP_STRIDE = 16
CMP_HIDDEN = 256
SEL_BLOCK = 64
SEL_SHIFT = 6
SEL_TOPK = 16
WINDOW = 512
REL_BUCKETS = 32
REL_MAX_DIST = 128

DN_HEADS = 4
DN_HEAD_DIM = 128
DN_CONV = 4
X_HEADS = 4
X_HEAD_DIM = 128
N_BRANCH = 3

NSA_Q = NSA_HEADS * NSA_HEAD_DIM
NSA_KV = NSA_GROUPS * NSA_HEAD_DIM
DN_W = DN_HEADS * DN_HEAD_DIM
X_W = X_HEADS * X_HEAD_DIM

LANES = 128
VMEM_LIMIT = 56 * 1024 * 1024

TM = 1024
FF_CHUNK = 256
PROJ_CHUNK = 512
TQ = 256
NSA_ACC_ROWS = 80
NSA_WAVE = 4
DN_CHUNK = 128
DN_LEVELS = 7
DN_GROUP = 4
CONV_HALO = 16

SM_GATE = 0
SM_A = N_BRANCH * NSA_HEADS
SM_B = SM_A + DN_HEADS


def _dot(a, b):
    return jnp.dot(a, b, preferred_element_type=F32)


def _dot_nt(a, b):
    return lax.dot_general(a, b, (((1,), (1,)), ((), ())), preferred_element_type=F32)


def _bmm(a, b):
    return lax.dot_general(a, b, (((2,), (1,)), ((0,), (0,))), preferred_element_type=F32)


def _bmm_nt(a, b):
    return lax.dot_general(a, b, (((2,), (2,)), ((0,), (0,))), preferred_element_type=F32)


def _rms(x, gain):
    return x * lax.rsqrt(jnp.mean(x * x, axis=-1, keepdims=True) + NORM_EPS) * gain


def _silu(x):
    return x * jax.nn.sigmoid(x)


def _const_spec(shape):
    zeros = (0,) * len(shape)
    return pl.BlockSpec(shape, lambda *_: zeros, pipeline_mode=pl.Buffered(1))


def _params(*sem):
    return pltpu.CompilerParams(dimension_semantics=sem, vmem_limit_bytes=VMEM_LIMIT)


def _ffn_kernel(x_ref, pre_ref, wg_ref, wu_ref, wd_ref, post_ref, o_ref):
    x = x_ref[...]
    xn = _rms(x, pre_ref[...]).astype(BF16)
    acc = jnp.zeros(x.shape, F32)
    for c in range(D_FF // FF_CHUNK):
        cols = slice(c * FF_CHUNK, (c + 1) * FF_CHUNK)
        gate = _dot(xn, wg_ref[:, cols])
        up = _dot(xn, wu_ref[:, cols])
        acc = acc + _dot((_silu(gate) * up).astype(BF16), wd_ref[cols, :])
    o_ref[...] = x + 0.5 * _rms(acc, post_ref[...])


def _ffn(x, pre, wg, wu, wd, post):
    t = x.shape[0]
    row = pl.BlockSpec((TM, D_MODEL), lambda i: (i, 0))
    return pl.pallas_call(
        _ffn_kernel,
        grid=(t // TM,),
        in_specs=[row, _const_spec((1, D_MODEL)), _const_spec((D_MODEL, D_FF)), _const_spec((D_MODEL, D_FF)),
                  _const_spec((D_FF, D_MODEL)), _const_spec((1, D_MODEL))],
        out_specs=row,
        out_shape=jax.ShapeDtypeStruct((t, D_MODEL), F32),
        compiler_params=_params("parallel"),
        name="ffn",
    )(x, pre.reshape(1, -1), wg.astype(BF16), wu.astype(BF16), wd.astype(BF16), post.reshape(1, -1))


PROJ_SPLITS = (("q", NSA_Q), ("kc", NSA_KV), ("vc", NSA_KV), ("kvsw", 4 * NSA_KV), ("dn", 3 * DN_W),
               ("z", DN_W), ("mq", X_W), ("sm", LANES))
PROJ_WIDTH = sum(w for _, w in PROJ_SPLITS)
LOG2E = math.log2(math.e)
PROJ_SCALES = {"q": NSA_HEAD_DIM ** -0.5 * LOG2E, "mq": X_HEAD_DIM ** -0.5}


def _proj_kernel(x_ref, g_ref, w_ref, *out_refs):
    xn = _rms(x_ref[...], g_ref[...]).astype(BF16)
    start = 0
    for (name, width), ref in zip(PROJ_SPLITS, out_refs):
        for c0 in range(0, width, PROJ_CHUNK):
            cw = min(PROJ_CHUNK, width - c0)
            y = _dot(xn, w_ref[:, start + c0:start + c0 + cw])
            if name in PROJ_SCALES:
                y = y * PROJ_SCALES[name]
            ref[:, c0:c0 + cw] = y.astype(ref.dtype)
        start += width


def _arrange_w_in(w_in):
    o_kv = NSA_Q
    o_g = o_kv + 6 * NSA_KV
    o_dn = o_g + N_BRANCH * NSA_HEADS
    o_a = o_dn + 3 * DN_W
    o_z = o_a + 2 * DN_HEADS
    o_mq = o_z + DN_W
    kv = w_in[:, o_kv:o_g].reshape(D_MODEL, 6, NSA_GROUPS, NSA_HEAD_DIM)
    kvsw = kv[:, 2:6].transpose(0, 2, 1, 3).reshape(D_MODEL, 4 * NSA_KV)
    small = jnp.concatenate([w_in[:, o_g:o_dn], w_in[:, o_a:o_z],
                             jnp.zeros((D_MODEL, LANES - N_BRANCH * NSA_HEADS - 2 * DN_HEADS), w_in.dtype)], axis=1)
    return jnp.concatenate([w_in[:, :o_kv], kv[:, 0].reshape(D_MODEL, NSA_KV), kv[:, 1].reshape(D_MODEL, NSA_KV),
                            kvsw, w_in[:, o_dn:o_a], w_in[:, o_z:o_mq], w_in[:, o_mq:], small], axis=1).astype(BF16)


def _proj(x, gain, w_arranged):
    t = x.shape[0]
    outs, specs = [], []
    for name, width in PROJ_SPLITS:
        outs.append(jax.ShapeDtypeStruct((t, width), F32 if name == "sm" else BF16))
        specs.append(pl.BlockSpec((TM, width), lambda i: (i, 0)))
    return pl.pallas_call(
        _proj_kernel,
        grid=(t // TM,),
        in_specs=[pl.BlockSpec((TM, D_MODEL), lambda i: (i, 0)), _const_spec((1, D_MODEL)),
                  _const_spec((D_MODEL, PROJ_WIDTH))],
        out_specs=specs,
        out_shape=outs,
        compiler_params=_params("parallel"),
        name="proj",
    )(x, gain.reshape(1, -1), w_arranged)


CMP_ROW = CMP_STRIDE * NSA_KV


def _cmp_kernel(kc_ref, vc_ref, posk_ref, posv_ref, w1k_ref, w1v_ref, w2k_ref, w2v_ref, ko_ref, vo_ref):
    n_rows = kc_ref.shape[1]

    def one(src_ref, pos_ref, w1_ref, w2_ref, out_ref):
        src = src_ref[0].astype(F32)
        first = _dot((src + pos_ref[0:1, :]).astype(BF16), w1_ref[0])
        second = _dot((src + pos_ref[1:2, :]).astype(BF16), w1_ref[1])
        hidden = _silu(first + pltpu.roll(second, n_rows - 1, 0))
        out = _dot(hidden.astype(BF16), w2_ref[...])
        for g in range(NSA_GROUPS):
            out_ref[0, g] = out[:, g * NSA_HEAD_DIM:(g + 1) * NSA_HEAD_DIM].astype(out_ref.dtype)

    one(kc_ref, posk_ref, w1k_ref, w2k_ref, ko_ref)
    one(vc_ref, posv_ref, w1v_ref, w2v_ref, vo_ref)


def _arrange_cmp(pos, w1, w2):
    hid2 = NSA_GROUPS * CMP_HIDDEN
    w1r = w1.reshape(2, CMP_STRIDE, NSA_HEAD_DIM, CMP_HIDDEN)
    eye = jnp.eye(NSA_GROUPS, dtype=w1.dtype)
    w1g = jnp.einsum("hidk,ge->higdek", w1r, eye).reshape(2, CMP_ROW, hid2).astype(BF16)
    w2g = jnp.einsum("kd,ge->gked", w2, eye).reshape(hid2, NSA_KV).astype(BF16)
    posr = jnp.broadcast_to(pos.reshape(2, CMP_STRIDE, 1, NSA_HEAD_DIM),
                            (2, CMP_STRIDE, NSA_GROUPS, NSA_HEAD_DIM)).reshape(2, CMP_ROW).astype(F32)
    return posr, w1g, w2g


def _cmp(kc, vc, posk, posv, w1k, w1v, w2k, w2v, batch, seq):
    n_rows = seq // CMP_STRIDE
    src = pl.BlockSpec((1, n_rows, CMP_ROW), lambda b: (b, 0, 0))
    out = pl.BlockSpec((1, NSA_GROUPS, n_rows, NSA_HEAD_DIM), lambda b: (b, 0, 0, 0))
    out_shape = jax.ShapeDtypeStruct((batch, NSA_GROUPS, n_rows, NSA_HEAD_DIM), BF16)
    hid2 = NSA_GROUPS * CMP_HIDDEN
    return pl.pallas_call(
        _cmp_kernel,
        grid=(batch,),
        in_specs=[src, src, _const_spec((2, CMP_ROW)), _const_spec((2, CMP_ROW)),
                  _const_spec((2, CMP_ROW, hid2)), _const_spec((2, CMP_ROW, hid2)),
                  _const_spec((hid2, NSA_KV)), _const_spec((hid2, NSA_KV))],
        out_specs=[out, out],
        out_shape=[out_shape, out_shape],
        compiler_params=_params("parallel"),
        name="cmp",
    )(kc.reshape(batch, n_rows, CMP_ROW), vc.reshape(batch, n_rows, CMP_ROW), posk, posv, w1k, w1v, w2k, w2v)


def _rel_bucket_table(n):
    exact = REL_BUCKETS // 2
    dist = np.arange(n)
    d = np.maximum(dist, 1).astype(np.float32)
    log_bucket = exact + (np.log(d / np.float32(exact)) / np.float32(math.log(REL_MAX_DIST / exact))
                          * np.float32(REL_BUCKETS - exact)).astype(np.int32)
    return np.where(dist < exact, dist, np.minimum(log_bucket, REL_BUCKETS - 1))


def _nsa_tables(rel_bias, seq):
    bucket = _rel_bucket_table(seq + 2 * TQ)
    n_cmp_pad = seq // CMP_STRIDE
    bias = rel_bias.astype(F32).T * LOG2E
    neg = jnp.full((NSA_HEADS, 1), NEG_INF, F32)

    def by_distance(values, dist):
        cls = np.where(dist >= 0, bucket[np.maximum(dist, 0)], REL_BUCKETS).astype(np.int32)
        one_hot = (jnp.asarray(cls.reshape(1, -1))
                   == lax.broadcasted_iota(jnp.int32, (REL_BUCKETS + 1, 1), 0)).astype(F32)
        picked = jnp.dot(jnp.concatenate([values, neg], axis=1), one_hot, precision=lax.Precision.HIGHEST)
        return picked.reshape((NSA_GROUPS, NSA_HPG) + dist.shape)

    t = np.arange(seq).reshape(seq // TQ, 1, TQ)
    dist = t - (np.arange(n_cmp_pad)[None, :, None] * CMP_STRIDE + CMP_BLOCK - 1)
    cb = by_distance(bias, dist).transpose(0, 2, 3, 1, 4).reshape(NSA_GROUPS, seq // TQ, n_cmp_pad, NSA_HPG * TQ)
    shifted = bias - bias[:, REL_BUCKETS - 1:]
    q_minus_k = np.arange(TQ)[None, :] - np.arange(TQ)[:, None]
    t0 = by_distance(shifted, q_minus_k)
    t1 = by_distance(shifted, TQ + q_minus_k)
    w2 = jnp.asarray(np.where(q_minus_k < 0, 0.0, NEG_INF), F32)
    tabs = jnp.stack([t0, t1, jnp.broadcast_to(w2, t0.shape)], axis=1)
    tabs = tabs.reshape(NSA_GROUPS, 3, NSA_HPG * TQ, TQ)
    c0 = np.arange(n_cmp_pad)[None, :] * CMP_STRIDE
    s0 = np.arange(LANES)[:, None] * SEL_BLOCK
    overlap = np.maximum(np.minimum(c0 + CMP_BLOCK, s0 + SEL_BLOCK) - np.maximum(c0, s0), 0)
    valid = (np.arange(LANES)[:, None] < seq // SEL_BLOCK) & (np.arange(n_cmp_pad)[None, :] < n_cmp_pad - 1)
    to_sel_t = jnp.asarray(np.where(valid, overlap / CMP_BLOCK, 0.0), BF16)
    return cb, tabs, to_sel_t


SEL, WIN = 0, 1


def _nsa_kernel(q_ref, kv_ref, kc_ref, vc_ref, sm_ref, cb_ref, tab_ref, tosel_ref, o_ref, m_ref, acc_ref):
    qi = pl.program_id(1)
    n_sel_blocks = kv_ref.shape[0] // SEL_BLOCK
    sm = sm_ref[...]
    ones_cols = jnp.ones((TQ, NSA_HEAD_DIM), BF16)
    heads = range(NSA_HPG)
    groups = range(NSA_GROUPS)

    def head_rows(j):
        return slice(j * TQ, (j + 1) * TQ)

    def flash_step(streams, first=False):
        chains = []
        for si, (branch, g, q_list, k_t, v_t, add_fn) in enumerate(streams):
            v_ext_t = jnp.concatenate([v_t, ones_cols], axis=1).astype(F32).T[:NSA_ACC_ROWS].astype(BF16)
            chains += [(si, branch, g, j, (q_list[j], k_t), v_ext_t, add_fn) for j in heads]
        waves = [chains[i:i + NSA_WAVE] for i in range(0, len(chains), NSA_WAVE)]
        s, m_old, m_new, p = {}, {}, {}, {}
        for step in range(len(waves) + 2):
            for si, _, _, j, (q, k_t), _, add_fn in (waves[step] if step < len(waves) else ()):
                s[si, j] = _dot_nt(k_t, q) if add_fn is None else _dot_nt(k_t, q) + add_fn(j)
            for si, branch, g, j, _, _, _ in (waves[step - 1] if 0 < step <= len(waves) else ()):
                c = (si, j)
                m_new[c] = jnp.broadcast_to(jnp.max(s[c], axis=0, keepdims=True), (8, TQ))
                if not first:
                    m_old[c] = m_ref[branch, g, j]
                    m_new[c] = jnp.maximum(m_old[c], m_new[c])
                m_ref[branch, g, j] = m_new[c]
                p[c] = jnp.exp2((s[c] - m_new[c][0:1]).astype(BF16))
            for si, branch, g, j, _, v_ext_t, _ in (waves[step - 2] if step >= 2 else ()):
                c = (si, j)
                acc = _dot(v_ext_t, p[c])
                if not first:
                    acc = jnp.exp2(m_old[c] - m_new[c])[0:1] * acc_ref[branch, g, j] + acc
                acc_ref[branch, g, j] = acc

    def head_cols(g, j):
        c0 = (g * NSA_HPG + j) * NSA_HEAD_DIM
        return slice(c0, c0 + NSA_HEAD_DIM)

    q_heads = [[q_ref[:, head_cols(g, j)] for j in heads] for g in groups]

    def kv_tile(g, kt, which):
        c0 = (g * 4 + which) * NSA_HEAD_DIM
        return kv_ref[pl.ds(pl.multiple_of(kt * TQ, TQ), TQ), c0:c0 + NSA_HEAD_DIM]

    def compressed(g):
        cols = NSA_HPG * TQ
        tok = qi * TQ + (lax.broadcasted_iota(jnp.int32, (1, cols), 1) & (TQ - 1))
        s = _dot_nt(kc_ref[0, g], jnp.concatenate(q_heads[g], axis=0)) + cb_ref[g, 0]
        e = jnp.exp2(s - jnp.max(s, axis=0, keepdims=True))
        p = e * (jnp.where(tok >= CMP_BLOCK - 1, 1.0, 0.0) / jnp.sum(e, axis=0, keepdims=True))
        vc_pad = jnp.concatenate([vc_ref[0, g], jnp.zeros((vc_ref.shape[2], NSA_HEAD_DIM), BF16)], axis=1)
        out_t = _dot(vc_pad.astype(F32).T.astype(BF16), p.astype(BF16))
        p_sum = p[:, 0:TQ]
        for j in range(1, NSA_HPG):
            p_sum = p_sum + p[:, j * TQ:(j + 1) * TQ]
        return out_t, p_sum

    def dropped_blocks(p_sum):
        p_hi = p_sum.astype(BF16)
        p_lo = (p_sum - p_hi.astype(F32)).astype(BF16)
        imp_t = _dot(tosel_ref[...], p_hi) + _dot(tosel_ref[...], p_lo)
        imp_t = imp_t[0:n_sel_blocks]
        blk = lax.broadcasted_iota(jnp.int32, (n_sel_blocks, TQ), 0)
        cur = jnp.right_shift(qi * TQ + lax.broadcasted_iota(jnp.int32, (n_sel_blocks, TQ), 1), SEL_SHIFT)
        score = jnp.where(blk <= cur, imp_t, -FORCE_SCORE)
        for forced_blk in (0, cur, cur - 1):
            score = jnp.where(blk == forced_blk, FORCE_SCORE, score)
        ranks = []
        for r0 in range(0, n_sel_blocks, 8):
            mine = score[r0:r0 + 8]
            row = r0 + lax.broadcasted_iota(jnp.int32, (8, TQ), 0)
            rank = jnp.zeros(mine.shape, F32)
            for m_blk in range(n_sel_blocks):
                other = score[m_blk:m_blk + 1, :]
                if m_blk < r0:
                    rank = jnp.where(other >= mine, rank + 1.0, rank)
                elif m_blk >= r0 + 8:
                    rank = jnp.where(other > mine, rank + 1.0, rank)
                else:
                    tie = jnp.where(row > m_blk, 1.0, 0.0)
                    rank = rank + jnp.where(other > mine, 1.0, jnp.where(other == mine, tie, 0.0))
            ranks.append(rank)
        rank = jnp.concatenate(ranks, axis=0)
        drop_t = jnp.where(rank < SEL_TOPK, jnp.where(score > -1.0, 0.0, 1.0), 1.0)
        drop_t = jnp.concatenate([drop_t, jnp.ones((LANES - n_sel_blocks, TQ), F32)], axis=0)
        return drop_t.T.astype(BF16)

    def sel_stream(q_aug, g, kt, tab):
        key_blk = jnp.right_shift(kt * TQ + lax.broadcasted_iota(jnp.int32, (TQ, NSA_HEAD_DIM), 0), SEL_SHIFT)
        hit = lax.broadcasted_iota(jnp.int32, (TQ, NSA_HEAD_DIM), 1) == key_blk
        k_aug = jnp.concatenate([kv_tile(g, kt, 0), jnp.where(hit, NEG_INF, 0.0).astype(BF16)], axis=1)
        add_fn = None if tab is None else (lambda j: tab_ref[g, tab, head_rows(j), :])
        return SEL, g, q_aug[g], k_aug, kv_tile(g, kt, 1), add_fn

    def win_stream(g, kt, tab):
        return WIN, g, q_heads[g], kv_tile(g, kt, 2), kv_tile(g, kt, 3), lambda j: tab_ref[g, tab, head_rows(j), :]

    def whole_step(n_near):
        flash_step([win_stream(g, qi, 0) for g in groups], first=True)
        cmp_out = [compressed(g) for g in groups]
        flash_step([win_stream(g, qi - 1 - n, 1 + n) for n in range(n_near) for g in groups])
        drop = [dropped_blocks(cmp_out[g][1]) for g in groups]
        q_aug = [[jnp.concatenate([q_heads[g][j], drop[g][:, :NSA_HEAD_DIM]], axis=1) for j in heads]
                 for g in groups]
        flash_step([sel_stream(q_aug, g, qi, 0) for g in groups], first=True)
        flash_step([sel_stream(q_aug, g, qi - 1 - n, 1 if n == 0 else None) for n in range(n_near) for g in groups])
        if n_near == 2:
            n_far = qi - 2

            @pl.loop(0, n_far // 2)
            def _(pair):
                flash_step([sel_stream(q_aug, g, 2 * pair + n, None) for n in range(2) for g in groups])

            @pl.when(n_far % 2 == 1)
            def _():
                flash_step([sel_stream(q_aug, g, n_far - 1, None) for g in groups])

        gates_t = jax.nn.sigmoid(sm).T
        for g in groups:
            for j in heads:
                def gate(branch):
                    c = SM_GATE + branch * NSA_HEADS + g * NSA_HPG + j
                    return gates_t[c:c + 1, :]

                def weighted(branch, gate_row):
                    acc = acc_ref[branch, g, j]
                    return acc[:NSA_HEAD_DIM] * (gate_row / acc[NSA_HEAD_DIM:NSA_HEAD_DIM + 1, :])

                o_t = (gate(0) * cmp_out[g][0][:NSA_HEAD_DIM, head_rows(j)] + weighted(SEL, gate(1))
                       + weighted(WIN, gate(2)))
                o_t = jnp.concatenate([o_t, jnp.zeros_like(o_t)], axis=0)
                o_ref[:, head_cols(g, j)] = o_t.T[:, :NSA_HEAD_DIM].astype(o_ref.dtype)

    pl.when(qi == 0)(lambda: whole_step(0))
    pl.when(qi == 1)(lambda: whole_step(1))
    pl.when(qi >= 2)(lambda: whole_step(2))


def _nsa(q, kvsw, k_cmp, v_cmp, sm, cb, tabs, to_sel_t, batch, seq):
    assert WINDOW == 2 * TQ and seq % TQ == 0
    nq = seq // TQ
    n_cmp_pad = seq // CMP_STRIDE
    rows = NSA_HPG * TQ
    cmp_spec = pl.BlockSpec((1, NSA_GROUPS, n_cmp_pad, NSA_HEAD_DIM), lambda b, i: (b, 0, 0, 0))
    return pl.pallas_call(
        _nsa_kernel,
        grid=(batch, nq),
        in_specs=[
            pl.BlockSpec((TQ, NSA_Q), lambda b, i: (b * nq + i, 0)),
            pl.BlockSpec((seq, 4 * NSA_KV), lambda b, i: (b, 0)),
            cmp_spec, cmp_spec,
            pl.BlockSpec((TQ, LANES), lambda b, i: (b * nq + i, 0)),
            pl.BlockSpec((NSA_GROUPS, 1, n_cmp_pad, rows), lambda b, i: (0, i, 0, 0)),
            _const_spec((NSA_GROUPS, 3, rows, TQ)),
            _const_spec((LANES, n_cmp_pad)),
        ],
        out_specs=pl.BlockSpec((TQ, NSA_Q), lambda b, i: (b * nq + i, 0)),
        out_shape=jax.ShapeDtypeStruct((batch * seq, NSA_Q), BF16),
        scratch_shapes=[pltpu.VMEM((2, NSA_GROUPS, NSA_HPG, 8, TQ), F32),
                        pltpu.VMEM((2, NSA_GROUPS, NSA_HPG, NSA_ACC_ROWS, TQ), F32)],
        compiler_params=_params("parallel", "arbitrary"),
        name="nsa",
    )(q, kvsw, k_cmp, v_cmp, sm, cb, tabs, to_sel_t)


def _softplus(x):
    return jnp.maximum(x, 0.0) + jnp.log(1.0 + jnp.exp(-jnp.abs(x)))


def _l2_normalize(x):
    return x * lax.rsqrt(jnp.sum(x * x, axis=-1, keepdims=True) + NORM_EPS)


def _dn_kernel(dn_ref, z_ref, sm_ref, cw_ref, alog_ref, dtb_ref, gn_ref, o_ref, state_ref):
    c = DN_CHUNK
    n_chunks = dn_ref.shape[0] // c
    ii = lax.broadcasted_iota(jnp.int32, (c, c), 0)
    jj = lax.broadcasted_iota(jnp.int32, (c, c), 1)
    causal = ii >= jj
    strict = ii > jj
    split = ii ^ jj
    tri_ones = jnp.where(causal, 1.0, 0.0).astype(BF16)
    eye = jnp.where(ii == jj, 1.0, 0.0)
    state_ref[...] = jnp.zeros(state_ref.shape, F32)

    span = c * DN_GROUP

    def group(gi, carry):
        r0 = pl.multiple_of(gi * span, span)
        rows = pl.ds(r0, span)
        x = dn_ref[rows, :].astype(F32)
        halo_rows = pl.ds(pl.multiple_of(jnp.maximum(r0 - CONV_HALO, 0), CONV_HALO), CONV_HALO)
        halo = dn_ref[halo_rows, :].astype(F32) * jnp.where(gi > 0, 1.0, 0.0)
        xc = jnp.concatenate([halo, x], axis=0)
        y = x * cw_ref[DN_CONV - 1:DN_CONV, :]
        for k in range(1, DN_CONV):
            y = y + pltpu.roll(xc, k, 0)[CONV_HALO:, :] * cw_ref[DN_CONV - 1 - k:DN_CONV - k, :]
        y = _silu(y)

        smc = sm_ref[rows, :]
        log_decay = -jnp.exp(alog_ref[...]) * _softplus(smc + dtb_ref[...])
        beta = jax.nn.sigmoid(smc)
        ld1 = log_decay.astype(BF16)
        rest = log_decay - ld1.astype(F32)
        ld2 = rest.astype(BF16)
        ld3 = (rest - ld2.astype(F32)).astype(BF16)

        q_l, k_l, v_l, gcol_l, grow_l, gend_l, bcol_l = [], [], [], [], [], [], []
        for k in range(DN_GROUP):
            cr = slice(k * c, (k + 1) * c)
            gc = _dot(tri_ones, ld1[cr]) + _dot(tri_ones, ld2[cr]) + _dot(tri_ones, ld3[cr])
            gc_t = gc.T
            for h in range(DN_HEADS):
                q_l.append(_l2_normalize(y[cr, h * DN_HEAD_DIM:(h + 1) * DN_HEAD_DIM]) * DN_HEAD_DIM ** -0.5)
                k_l.append(_l2_normalize(y[cr, DN_W + h * DN_HEAD_DIM:DN_W + (h + 1) * DN_HEAD_DIM]))
                v_l.append(y[cr, 2 * DN_W + h * DN_HEAD_DIM:2 * DN_W + (h + 1) * DN_HEAD_DIM])
                gcol_l.append(gc[:, SM_A + h:SM_A + h + 1])
                grow_l.append(gc_t[SM_A + h:SM_A + h + 1, :])
                gend_l.append(gc[c - 1:c, SM_A + h:SM_A + h + 1])
                bcol_l.append(beta[cr, SM_B + h:SM_B + h + 1])
        qs, ks, vs = jnp.stack(q_l), jnp.stack(k_l), jnp.stack(v_l)
        g_col, g_row, g_end, b_col = jnp.stack(gcol_l), jnp.stack(grow_l), jnp.stack(gend_l), jnp.stack(bcol_l)

        decay = jnp.exp(jnp.where(causal, g_col - g_row, NEG_INF))
        qb = qs.astype(BF16)
        kb = ks.astype(BF16)
        n_mat = jnp.where(strict, b_col * _bmm_nt(kb, kb) * decay, 0.0)
        inv = eye - jnp.where(split == 1, n_mat, 0.0)
        for lvl in range(1, DN_LEVELS):
            n_lvl = jnp.where(jnp.right_shift(split, lvl) == 1, n_mat, 0.0).astype(BF16)
            inv_b = inv.astype(BF16)
            inv = inv - _bmm(inv_b, _bmm(n_lvl, inv_b).astype(BF16))
        e_col = jnp.exp(g_col)
        rhs = jnp.concatenate([vs * b_col, ks * (b_col * e_col)], axis=2).astype(BF16)
        sol = _bmm(inv.astype(BF16), rhs)
        u = sol[:, :, :DN_HEAD_DIM]
        w = sol[:, :, DN_HEAD_DIM:].astype(BF16)
        qk = (_bmm_nt(qb, kb) * decay).astype(BF16)
        q_dec = (qs * e_col).astype(BF16)
        k_dec = ks * jnp.exp(g_end - g_col)
        k_dec_t = jnp.stack([k_dec[n].T for n in range(DN_GROUP * DN_HEADS)]).astype(BF16)
        g_chunk = jnp.exp(g_end)

        state = state_ref[...]
        for k in range(DN_GROUP):
            hs = slice(k * DN_HEADS, (k + 1) * DN_HEADS)
            state_b = state.astype(BF16)
            v_new = (u[hs] - _bmm(w[hs], state_b)).astype(BF16)
            o = _bmm(q_dec[hs], state_b) + _bmm(qk[hs], v_new)
            state = state * g_chunk[hs] + _bmm(k_dec_t[hs], v_new)
            out_rows = pl.ds(r0 + k * c, c)
            for h in range(DN_HEADS):
                hd = slice(h * DN_HEAD_DIM, (h + 1) * DN_HEAD_DIM)
                zh = z_ref[out_rows, hd].astype(F32)
                o_ref[out_rows, hd] = (_rms(o[h], gn_ref[...]) * _silu(zh)).astype(o_ref.dtype)
        state_ref[...] = state
        return carry

    lax.fori_loop(0, n_chunks // DN_GROUP, group, 0)


def _dn(dn, z, sm, conv_w, a_log, dt_bias, out_gain, batch, seq):
    assert DN_CHUNK == 1 << DN_LEVELS and seq % (DN_CHUNK * DN_GROUP) == 0
    pad = jnp.zeros((1, LANES), F32)
    return pl.pallas_call(
        _dn_kernel,
        grid=(batch,),
        in_specs=[pl.BlockSpec((seq, 3 * DN_W), lambda b: (b, 0)), pl.BlockSpec((seq, DN_W), lambda b: (b, 0)),
                  pl.BlockSpec((seq, LANES), lambda b: (b, 0)), _const_spec((DN_CONV, 3 * DN_W)),
                  _const_spec((1, LANES)), _const_spec((1, LANES)), _const_spec((1, DN_HEAD_DIM))],
        out_specs=pl.BlockSpec((seq, DN_W), lambda b: (b, 0)),
        out_shape=jax.ShapeDtypeStruct((batch * seq, DN_W), BF16),
        scratch_shapes=[pltpu.VMEM((DN_HEADS, DN_HEAD_DIM, DN_HEAD_DIM), F32)],
        compiler_params=_params("parallel"),
        name="dn",
    )(dn, z, sm, conv_w.astype(F32), pad.at[0, SM_A:SM_A + DN_HEADS].set(a_log.astype(F32)),
      pad.at[0, SM_A:SM_A + DN_HEADS].set(dt_bias.astype(F32)), out_gain.reshape(1, -1).astype(F32))


def _memkv_kernel(mem_ref, g_ref, w_ref, k_ref, v_ref):
    mn = _rms(mem_ref[0], g_ref[...]).astype(BF16)
    k_ref[0] = _dot(mn, w_ref[:, :X_W]).astype(k_ref.dtype)
    v_ref[0] = _dot(mn, w_ref[:, X_W:]).astype(v_ref.dtype)


def _memkv(mem, gain, w_kv):
    batch, n_mem, _ = mem.shape
    out = pl.BlockSpec((1, n_mem, X_W), lambda b: (b, 0, 0))
    shape = jax.ShapeDtypeStruct((batch, n_mem, X_W), BF16)
    return pl.pallas_call(
        _memkv_kernel,
        grid=(batch,),
        in_specs=[pl.BlockSpec((1, n_mem, D_MODEL), lambda b: (b, 0, 0)), _const_spec((1, D_MODEL)),
                  _const_spec((D_MODEL, 2 * X_W))],
        out_specs=[out, out],
        out_shape=[shape, shape],
        compiler_params=_params("parallel"),
        name="memkv",
    )(mem, gain.reshape(1, -1), w_kv.astype(BF16))


def _memory_attention(q_ref, k_ref, v_ref):
    cols = [slice(h * X_HEAD_DIM, (h + 1) * X_HEAD_DIM) for h in range(X_HEADS)]
    s = [_dot_nt(q_ref[:, hd], k_ref[0, :, hd]) for hd in cols]
    e = [jnp.exp(x - jnp.max(x, axis=-1, keepdims=True)) for x in s]
    p = [(x / jnp.sum(x, axis=-1, keepdims=True)).astype(BF16) for x in e]
    return jnp.concatenate([_dot(ph, v_ref[0, :, hd]).astype(BF16) for hd, ph in zip(cols, p)], axis=1)


def _merge_kernel(x_ref, on_ref, od_ref, mq_ref, mk_ref, mv_ref, gpre_ref, wbg_ref, wbn_ref, wbd_ref, wbm_ref, wo_ref,
                  gpost_ref, o_ref):
    x = x_ref[...]
    u = _rms(x, gpre_ref[...]).astype(BF16)
    branches = (on_ref[...], od_ref[...], _memory_attention(mq_ref, mk_ref, mv_ref))
    merged = jnp.zeros(x.shape, F32)
    for i, (branch, w_ref) in enumerate(zip(branches, (wbn_ref, wbd_ref, wbm_ref))):
        gates = jax.nn.sigmoid(_dot(u, wbg_ref[:, i * D_MODEL:(i + 1) * D_MODEL]))
        merged = merged + gates * _dot(branch, w_ref[...])
    y = _dot(merged.astype(BF16), wo_ref[...])
    o_ref[...] = x + _rms(y, gpost_ref[...])


def _merge(x, o_nsa, o_dn, mq, mk, mv, gpre, wbg, wbn, wbd, wbm, wo, gpost, seq):
    t = x.shape[0]
    assert seq % TM == 0
    n_mem = mk.shape[1]
    row = pl.BlockSpec((TM, D_MODEL), lambda i: (i, 0))
    br = pl.BlockSpec((TM, NSA_Q), lambda i: (i, 0))
    mem = pl.BlockSpec((1, n_mem, X_W), lambda i: (i // (seq // TM), 0, 0))
    return pl.pallas_call(
        _merge_kernel,
        grid=(t // TM,),
        in_specs=[row, br, br, br, mem, mem, _const_spec((1, D_MODEL)), _const_spec((D_MODEL, N_BRANCH * D_MODEL)),
                  _const_spec((NSA_Q, D_MODEL)), _const_spec((DN_W, D_MODEL)), _const_spec((X_W, D_MODEL)),
                  _const_spec((D_MODEL, D_MODEL)), _const_spec((1, D_MODEL))],
        out_specs=row,
        out_shape=jax.ShapeDtypeStruct((t, D_MODEL), F32),
        compiler_params=_params("parallel"),
        name="merge",
    )(x, o_nsa, o_dn, mq, mk, mv, gpre.reshape(1, -1), wbg.astype(BF16), wbn.astype(BF16), wbd.astype(BF16),
      wbm.astype(BF16), wo.astype(BF16), gpost.reshape(1, -1))


def kernel(x, mem, ffn1_pre_norm, ffn1_w_gate, ffn1_w_up, ffn1_w_down, ffn1_post_norm, mix_pre_norm, w_in, cmp_pos_k, cmp_pos_v, cmp_k_w1, cmp_k_w2, cmp_v_w1, cmp_v_w2, rel_bias, dn_conv_w, dn_a_log, dn_dt_bias, dn_out_norm, mem_norm, w_mem_kv, w_branch_nsa, w_branch_dn, w_branch_mem, w_branch_gate, w_out, mix_post_norm, ffn2_pre_norm, ffn2_w_gate, ffn2_w_up, ffn2_w_down, ffn2_post_norm):
    batch, seq, d = x.shape
    xt = x.reshape(batch * seq, d)
    cb, tabs, to_sel_t = _nsa_tables(rel_bias, seq)
    for l in range(ffn1_pre_norm.shape[0]):
        xt = _ffn(xt, ffn1_pre_norm[l], ffn1_w_gate[l], ffn1_w_up[l], ffn1_w_down[l], ffn1_post_norm[l])

        q, kc, vc, kvsw, dn, z, mq, sm = _proj(xt, mix_pre_norm[l], _arrange_w_in(w_in[l]))
        posk, w1k, w2k = _arrange_cmp(cmp_pos_k[l], cmp_k_w1[l], cmp_k_w2[l])
        posv, w1v, w2v = _arrange_cmp(cmp_pos_v[l], cmp_v_w1[l], cmp_v_w2[l])
        k_cmp, v_cmp = _cmp(kc, vc, posk, posv, w1k, w1v, w2k, w2v, batch, seq)
        o_nsa = _nsa(q, kvsw, k_cmp, v_cmp, sm, cb, tabs, to_sel_t, batch, seq)
        o_dn = _dn(dn, z, sm, dn_conv_w[l], dn_a_log[l], dn_dt_bias[l], dn_out_norm[l], batch, seq)
        mk, mv = _memkv(mem, mem_norm[l], w_mem_kv[l])
        xt = _merge(xt, o_nsa, o_dn, mq, mk, mv, mix_pre_norm[l], w_branch_gate[l], w_branch_nsa[l], w_branch_dn[l],
                    w_branch_mem[l], w_out[l], mix_post_norm[l], seq)

        xt = _ffn(xt, ffn2_pre_norm[l], ffn2_w_gate[l], ffn2_w_up[l], ffn2_w_down[l], ffn2_post_norm[l])
    return xt.reshape(batch, seq, d)
```

```python
import math

import jax
import jax.numpy as jnp
import numpy as np
from jax import lax
from jax.experimental import pallas as pl
from jax.experimental.pallas import tpu as pltpu

F32 = jnp.float32
BF16 = jnp.bfloat16

D_MODEL = 1024
D_FF = 2816
NORM_EPS = 1e-6
NEG_INF = -1e30
FORCE_SCORE = 1e4

NSA_HEADS = 8
NSA_GROUPS = 2
NSA_HPG = NSA_HEADS // NSA_GROUPS
NSA_HEAD_DIM = 64
CMP_BLOCK = 32
CMP_STRIDE = 16
CMP_HIDDEN = 256
SEL_BLOCK = 64
SEL_SHIFT = 6
SEL_TOPK = 16
WINDOW = 512
REL_BUCKETS = 32
REL_MAX_DIST = 128

DN_HEADS = 4
DN_HEAD_DIM = 128
DN_CONV = 4
X_HEADS = 4
X_HEAD_DIM = 128
N_BRANCH = 3

NSA_Q = NSA_HEADS * NSA_HEAD_DIM
NSA_KV = NSA_GROUPS * NSA_HEAD_DIM
DN_W = DN_HEADS * DN_HEAD_DIM
X_W = X_HEADS * X_HEAD_DIM

LANES = 128
VMEM_LIMIT = 56 * 1024 * 1024

TM = 1024
FF_CHUNK = 256
PROJ_CHUNK = 512
TQ = 256
NSA_ACC_ROWS = 80
NSA_WAVE = 4
DN_CHUNK = 128
DN_LEVELS = 7
DN_GROUP = 4
CONV_HALO = 16

SM_GATE = 0
SM_A = N_BRANCH * NSA_HEADS
SM_B = SM_A + DN_HEADS


def _dot(a, b):
    return jnp.dot(a, b, preferred_element_type=F32)


def _dot_nt(a, b):
    return lax.dot_general(a, b, (((1,), (1,)), ((), ())), preferred_element_type=F32)


def _bmm(a, b):
    return lax.dot_general(a, b, (((2,), (1,)), ((0,), (0,))), preferred_element_type=F32)


def _bmm_nt(a, b):
    return lax.dot_general(a, b, (((2,), (2,)), ((0,), (0,))), preferred_element_type=F32)


def _rms(x, gain):
    return x * lax.rsqrt(jnp.mean(x * x, axis=-1, keepdims=True) + NORM_EPS) * gain


def _silu(x):
    return x * jax.nn.sigmoid(x)


def _const_spec(shape):
    zeros = (0,) * len(shape)
    return pl.BlockSpec(shape, lambda *_: zeros, pipeline_mode=pl.Buffered(1))


def _params(*sem):
    return pltpu.CompilerParams(dimension_semantics=sem, vmem_limit_bytes=VMEM_LIMIT)


def _ffn_kernel(x_ref, pre_ref, wg_ref, wu_ref, wd_ref, post_ref, o_ref):
    x = x_ref[...]
    xn = _rms(x, pre_ref[...]).astype(BF16)
    acc = jnp.zeros(x.shape, F32)
    for c in range(D_FF // FF_CHUNK):
        cols = slice(c * FF_CHUNK, (c + 1) * FF_CHUNK)
        gate = _dot(xn, wg_ref[:, cols])
        up = _dot(xn, wu_ref[:, cols])
        acc = acc + _dot((_silu(gate) * up).astype(BF16), wd_ref[cols, :])
    o_ref[...] = x + 0.5 * _rms(acc, post_ref[...])


def _ffn(x, pre, wg, wu, wd, post):
    t = x.shape[0]
    row = pl.BlockSpec((TM, D_MODEL), lambda i: (i, 0))
    return pl.pallas_call(
        _ffn_kernel,
        grid=(t // TM,),
        in_specs=[row, _const_spec((1, D_MODEL)), _const_spec((D_MODEL, D_FF)), _const_spec((D_MODEL, D_FF)),
                  _const_spec((D_FF, D_MODEL)), _const_spec((1, D_MODEL))],
        out_specs=row,
        out_shape=jax.ShapeDtypeStruct((t, D_MODEL), F32),
        compiler_params=_params("parallel"),
        name="ffn",
    )(x, pre.reshape(1, -1), wg.astype(BF16), wu.astype(BF16), wd.astype(BF16), post.reshape(1, -1))


PROJ_SPLITS = (("q", NSA_Q), ("kc", NSA_KV), ("vc", NSA_KV), ("kvsw", 4 * NSA_KV), ("dn", 3 * DN_W),
               ("z", DN_W), ("mq", X_W), ("sm", LANES))
PROJ_WIDTH = sum(w for _, w in PROJ_SPLITS)
LOG2E = math.log2(math.e)
PROJ_SCALES = {"q": NSA_HEAD_DIM ** -0.5 * LOG2E, "mq": X_HEAD_DIM ** -0.5}


def _proj_kernel(x_ref, g_ref, w_ref, *out_refs):
    xn = _rms(x_ref[...], g_ref[...]).astype(BF16)
    start = 0
    for (name, width), ref in zip(PROJ_SPLITS, out_refs):
        for c0 in range(0, width, PROJ_CHUNK):
            cw = min(PROJ_CHUNK, width - c0)
            y = _dot(xn, w_ref[:, start + c0:start + c0 + cw])
            if name in PROJ_SCALES:
                y = y * PROJ_SCALES[name]
            ref[:, c0:c0 + cw] = y.astype(ref.dtype)
        start += width


def _arrange_w_in(w_in):
    o_kv = NSA_Q
    o_g = o_kv + 6 * NSA_KV
    o_dn = o_g + N_BRANCH * NSA_HEADS
    o_a = o_dn + 3 * DN_W
    o_z = o_a + 2 * DN_HEADS
    o_mq = o_z + DN_W
    kv = w_in[:, o_kv:o_g].reshape(D_MODEL, 6, NSA_GROUPS, NSA_HEAD_DIM)
    kvsw = kv[:, 2:6].transpose(0, 2, 1, 3).reshape(D_MODEL, 4 * NSA_KV)
    small = jnp.concatenate([w_in[:, o_g:o_dn], w_in[:, o_a:o_z],
                             jnp.zeros((D_MODEL, LANES - N_BRANCH * NSA_HEADS - 2 * DN_HEADS), w_in.dtype)], axis=1)
    return jnp.concatenate([w_in[:, :o_kv], kv[:, 0].reshape(D_MODEL, NSA_KV), kv[:, 1].reshape(D_MODEL, NSA_KV),
                            kvsw, w_in[:, o_dn:o_a], w_in[:, o_z:o_mq], w_in[:, o_mq:], small], axis=1).astype(BF16)


def _proj(x, gain, w_arranged):
    t = x.shape[0]
    outs, specs = [], []
    for name, width in PROJ_SPLITS:
        outs.append(jax.ShapeDtypeStruct((t, width), F32 if name == "sm" else BF16))
        specs.append(pl.BlockSpec((TM, width), lambda i: (i, 0)))
    return pl.pallas_call(
        _proj_kernel,
        grid=(t // TM,),
        in_specs=[pl.BlockSpec((TM, D_MODEL), lambda i: (i, 0)), _const_spec((1, D_MODEL)),
                  _const_spec((D_MODEL, PROJ_WIDTH))],
        out_specs=specs,
        out_shape=outs,
        compiler_params=_params("parallel"),
        name="proj",
    )(x, gain.reshape(1, -1), w_arranged)


CMP_ROW = CMP_STRIDE * NSA_KV


def _cmp_kernel(kc_ref, vc_ref, posk_ref, posv_ref, w1k_ref, w1v_ref, w2k_ref, w2v_ref, ko_ref, vo_ref):
    n_rows = kc_ref.shape[1]

    def one(src_ref, pos_ref, w1_ref, w2_ref, out_ref):
        src = src_ref[0].astype(F32)
        first = _dot((src + pos_ref[0:1, :]).astype(BF16), w1_ref[0])
        second = _dot((src + pos_ref[1:2, :]).astype(BF16), w1_ref[1])
        hidden = _silu(first + pltpu.roll(second, n_rows - 1, 0))
        out = _dot(hidden.astype(BF16), w2_ref[...])
        for g in range(NSA_GROUPS):
            out_ref[0, g] = out[:, g * NSA_HEAD_DIM:(g + 1) * NSA_HEAD_DIM].astype(out_ref.dtype)

    one(kc_ref, posk_ref, w1k_ref, w2k_ref, ko_ref)
    one(vc_ref, posv_ref, w1v_ref, w2v_ref, vo_ref)


def _arrange_cmp(pos, w1, w2):
    hid2 = NSA_GROUPS * CMP_HIDDEN
    w1r = w1.reshape(2, CMP_STRIDE, NSA_HEAD_DIM, CMP_HIDDEN)
    eye = jnp.eye(NSA_GROUPS, dtype=w1.dtype)
    w1g = jnp.einsum("hidk,ge->higdek", w1r, eye).reshape(2, CMP_ROW, hid2).astype(BF16)
    w2g = jnp.einsum("kd,ge->gked", w2, eye).reshape(hid2, NSA_KV).astype(BF16)
    posr = jnp.broadcast_to(pos.reshape(2, CMP_STRIDE, 1, NSA_HEAD_DIM),
                            (2, CMP_STRIDE, NSA_GROUPS, NSA_HEAD_DIM)).reshape(2, CMP_ROW).astype(F32)
    return posr, w1g, w2g


def _cmp(kc, vc, posk, posv, w1k, w1v, w2k, w2v, batch, seq):
    n_rows = seq // CMP_STRIDE
    src = pl.BlockSpec((1, n_rows, CMP_ROW), lambda b: (b, 0, 0))
    out = pl.BlockSpec((1, NSA_GROUPS, n_rows, NSA_HEAD_DIM), lambda b: (b, 0, 0, 0))
    out_shape = jax.ShapeDtypeStruct((batch, NSA_GROUPS, n_rows, NSA_HEAD_DIM), BF16)
    hid2 = NSA_GROUPS * CMP_HIDDEN
    return pl.pallas_call(
        _cmp_kernel,
        grid=(batch,),
        in_specs=[src, src, _const_spec((2, CMP_ROW)), _const_spec((2, CMP_ROW)),
                  _const_spec((2, CMP_ROW, hid2)), _const_spec((2, CMP_ROW, hid2)),
                  _const_spec((hid2, NSA_KV)), _const_spec((hid2, NSA_KV))],
        out_specs=[out, out],
        out_shape=[out_shape, out_shape],
        compiler_params=_params("parallel"),
        name="cmp",
    )(kc.reshape(batch, n_rows, CMP_ROW), vc.reshape(batch, n_rows, CMP_ROW), posk, posv, w1k, w1v, w2k, w2v)


def _rel_bucket_table(n):
    exact = REL_BUCKETS // 2
    dist = np.arange(n)
    d = np.maximum(dist, 1).astype(np.float32)
    log_bucket = exact + (np.log(d / np.float32(exact)) / np.float32(math.log(REL_MAX_DIST / exact))
                          * np.float32(REL_BUCKETS - exact)).astype(np.int32)
    return np.where(dist < exact, dist, np.minimum(log_bucket, REL_BUCKETS - 1))


def _nsa_tables(rel_bias, seq):
    bucket = _rel_bucket_table(seq + 2 * TQ)
    n_cmp_pad = seq // CMP_STRIDE
    bias = rel_bias.astype(F32).T * LOG2E
    neg = jnp.full((NSA_HEADS, 1), NEG_INF, F32)

    def by_distance(values, dist):
        cls = np.where(dist >= 0, bucket[np.maximum(dist, 0)], REL_BUCKETS).astype(np.int32)
        one_hot = (jnp.asarray(cls.reshape(1, -1))
                   == lax.broadcasted_iota(jnp.int32, (REL_BUCKETS + 1, 1), 0)).astype(F32)
        picked = jnp.dot(jnp.concatenate([values, neg], axis=1), one_hot, precision=lax.Precision.HIGHEST)
        return picked.reshape((NSA_GROUPS, NSA_HPG) + dist.shape)

    t = np.arange(seq).reshape(seq // TQ, 1, TQ)
    dist = t - (np.arange(n_cmp_pad)[None, :, None] * CMP_STRIDE + CMP_BLOCK - 1)
    cb = by_distance(bias, dist).transpose(0, 2, 3, 1, 4).reshape(NSA_GROUPS, seq // TQ, n_cmp_pad, NSA_HPG * TQ)
    shifted = bias - bias[:, REL_BUCKETS - 1:]
    q_minus_k = np.arange(TQ)[None, :] - np.arange(TQ)[:, None]
    t0 = by_distance(shifted, q_minus_k)
    t1 = by_distance(shifted, TQ + q_minus_k)
    w2 = jnp.asarray(np.where(q_minus_k < 0, 0.0, NEG_INF), F32)
    tabs = jnp.stack([t0, t1, jnp.broadcast_to(w2, t0.shape)], axis=1)
    tabs = tabs.reshape(NSA_GROUPS, 3, NSA_HPG * TQ, TQ)
    c0 = np.arange(n_cmp_pad)[None, :] * CMP_STRIDE
    s0 = np.arange(LANES)[:, None] * SEL_BLOCK
    overlap = np.maximum(np.minimum(c0 + CMP_BLOCK, s0 + SEL_BLOCK) - np.maximum(c0, s0), 0)
    valid = (np.arange(LANES)[:, None] < seq // SEL_BLOCK) & (np.arange(n_cmp_pad)[None, :] < n_cmp_pad - 1)
    to_sel_t = jnp.asarray(np.where(valid, overlap / CMP_BLOCK, 0.0), BF16)
    return cb, tabs, to_sel_t


SEL, WIN = 0, 1


def _nsa_kernel(q_ref, kv_ref, kc_ref, vc_ref, sm_ref, cb_ref, tab_ref, tosel_ref, o_ref, m_ref, acc_ref):
    qi = pl.program_id(0)
    n_sel_blocks = kv_ref.shape[0] // SEL_BLOCK
    sm = sm_ref[...]
    ones_cols = jnp.ones((TQ, NSA_HEAD_DIM), BF16)
    heads = range(NSA_HPG)
    groups = range(NSA_GROUPS)

    def head_rows(j):
        return slice(j * TQ, (j + 1) * TQ)

    def flash_step(streams, first=False):
        chains = []
        for si, (branch, g, q_list, k_t, v_t, add_fn) in enumerate(streams):
            v_ext_t = jnp.concatenate([v_t, ones_cols], axis=1).astype(F32).T[:NSA_ACC_ROWS].astype(BF16)
            chains += [(si, branch, g, j, (q_list[j], k_t), v_ext_t, add_fn) for j in heads]
        waves = [chains[i:i + NSA_WAVE] for i in range(0, len(chains), NSA_WAVE)]
        s, m_old, m_new, p = {}, {}, {}, {}
        for step in range(len(waves) + 2):
            for si, _, _, j, (q, k_t), _, add_fn in (waves[step] if step < len(waves) else ()):
                s[si, j] = _dot_nt(k_t, q) if add_fn is None else _dot_nt(k_t, q) + add_fn(j)
            for si, branch, g, j, _, _, _ in (waves[step - 1] if 0 < step <= len(waves) else ()):
                c = (si, j)
                m_new[c] = jnp.broadcast_to(jnp.max(s[c], axis=0, keepdims=True), (8, TQ))
                if not first:
                    m_old[c] = m_ref[branch, g, j]
                    m_new[c] = jnp.maximum(m_old[c], m_new[c])
                m_ref[branch, g, j] = m_new[c]
                p[c] = jnp.exp2((s[c] - m_new[c][0:1]).astype(BF16))
            for si, branch, g, j, _, v_ext_t, _ in (waves[step - 2] if step >= 2 else ()):
                c = (si, j)
                acc = _dot(v_ext_t, p[c])
                if not first:
                    acc = jnp.exp2(m_old[c] - m_new[c])[0:1] * acc_ref[branch, g, j] + acc
                acc_ref[branch, g, j] = acc

    def head_cols(g, j):
        c0 = (g * NSA_HPG + j) * NSA_HEAD_DIM
        return slice(c0, c0 + NSA_HEAD_DIM)

    q_heads = [[q_ref[:, head_cols(g, j)] for j in heads] for g in groups]

    def kv_tile(g, kt, which):
        c0 = (g * 4 + which) * NSA_HEAD_DIM
        return kv_ref[pl.ds(pl.multiple_of(kt * TQ, TQ), TQ), c0:c0 + NSA_HEAD_DIM]

    def compressed(g):
        cols = NSA_HPG * TQ
        tok = qi * TQ + (lax.broadcasted_iota(jnp.int32, (1, cols), 1) & (TQ - 1))
        s = _dot_nt(kc_ref[0, g], jnp.concatenate(q_heads[g], axis=0)) + cb_ref[g, 0]
        e = jnp.exp2(s - jnp.max(s, axis=0, keepdims=True))
        p = e * (jnp.where(tok >= CMP_BLOCK - 1, 1.0, 0.0) / jnp.sum(e, axis=0, keepdims=True))
        vc_pad = jnp.concatenate([vc_ref[0, g], jnp.zeros((vc_ref.shape[2], NSA_HEAD_DIM), BF16)], axis=1)
        out_t = _dot(vc_pad.astype(F32).T.astype(BF16), p.astype(BF16))
        p_sum = p[:, 0:TQ]
        for j in range(1, NSA_HPG):
            p_sum = p_sum + p[:, j * TQ:(j + 1) * TQ]
        return out_t, p_sum

    def dropped_blocks(p_sum):
        p_hi = p_sum.astype(BF16)
        p_lo = (p_sum - p_hi.astype(F32)).astype(BF16)
        imp_t = _dot(tosel_ref[...], p_hi) + _dot(tosel_ref[...], p_lo)
        imp_t = imp_t[0:n_sel_blocks]
        blk = lax.broadcasted_iota(jnp.int32, (n_sel_blocks, TQ), 0)
        cur = jnp.right_shift(qi * TQ + lax.broadcasted_iota(jnp.int32, (n_sel_blocks, TQ), 1), SEL_SHIFT)
        score = jnp.where(blk <= cur, imp_t, -FORCE_SCORE)
        for forced_blk in (0, cur, cur - 1):
            score = jnp.where(blk == forced_blk, FORCE_SCORE, score)
        ranks = []
        for r0 in range(0, n_sel_blocks, 8):
            mine = score[r0:r0 + 8]
            row = r0 + lax.broadcasted_iota(jnp.int32, (8, TQ), 0)
            rank = jnp.zeros(mine.shape, F32)
            for m_blk in range(n_sel_blocks):
                other = score[m_blk:m_blk + 1, :]
                if m_blk < r0:
                    rank = jnp.where(other >= mine, rank + 1.0, rank)
                elif m_blk >= r0 + 8:
                    rank = jnp.where(other > mine, rank + 1.0, rank)
                else:
                    tie = jnp.where(row > m_blk, 1.0, 0.0)
                    rank = rank + jnp.where(other > mine, 1.0, jnp.where(other == mine, tie, 0.0))
            ranks.append(rank)
        rank = jnp.concatenate(ranks, axis=0)
        drop_t = jnp.where(rank < SEL_TOPK, jnp.where(score > -1.0, 0.0, 1.0), 1.0)
        drop_t = jnp.concatenate([drop_t, jnp.ones((LANES - n_sel_blocks, TQ), F32)], axis=0)
        return drop_t.T.astype(BF16)

    def sel_stream(q_aug, g, kt, tab):
        key_blk = jnp.right_shift(kt * TQ + lax.broadcasted_iota(jnp.int32, (TQ, NSA_HEAD_DIM), 0), SEL_SHIFT)
        hit = lax.broadcasted_iota(jnp.int32, (TQ, NSA_HEAD_DIM), 1) == key_blk
        k_aug = jnp.concatenate([kv_tile(g, kt, 0), jnp.where(hit, NEG_INF, 0.0).astype(BF16)], axis=1)
        add_fn = None if tab is None else (lambda j: tab_ref[g, tab, head_rows(j), :])
        return SEL, g, q_aug[g], k_aug, kv_tile(g, kt, 1), add_fn

    def win_stream(g, kt, tab):
        return WIN, g, q_heads[g], kv_tile(g, kt, 2), kv_tile(g, kt, 3), lambda j: tab_ref[g, tab, head_rows(j), :]

    def whole_step(n_near):
        flash_step([win_stream(g, qi, 0) for g in groups], first=True)
        cmp_out = [compressed(g) for g in groups]
        flash_step([win_stream(g, qi - 1 - n, 1 + n) for n in range(n_near) for g in groups])
        drop = [dropped_blocks(cmp_out[g][1]) for g in groups]
        q_aug = [[jnp.concatenate([q_heads[g][j], drop[g][:, :NSA_HEAD_DIM]], axis=1) for j in heads]
                 for g in groups]
        flash_step([sel_stream(q_aug, g, qi, 0) for g in groups], first=True)
        flash_step([sel_stream(q_aug, g, qi - 1 - n, 1 if n == 0 else None) for n in range(n_near) for g in groups])
        if n_near == 2:
            n_far = qi - 2

            @pl.loop(0, n_far // 2)
            def _(pair):
                flash_step([sel_stream(q_aug, g, 2 * pair + n, None) for n in range(2) for g in groups])

            @pl.when(n_far % 2 == 1)
            def _():
                flash_step([sel_stream(q_aug, g, n_far - 1, None) for g in groups])

        gates_t = jax.nn.sigmoid(sm).T
        for g in groups:
            for j in heads:
                def gate(branch):
                    c = SM_GATE + branch * NSA_HEADS + g * NSA_HPG + j
                    return gates_t[c:c + 1, :]

                def weighted(branch, gate_row):
                    acc = acc_ref[branch, g, j]
                    return acc[:NSA_HEAD_DIM] * (gate_row / acc[NSA_HEAD_DIM:NSA_HEAD_DIM + 1, :])

                o_t = (gate(0) * cmp_out[g][0][:NSA_HEAD_DIM, head_rows(j)] + weighted(SEL, gate(1))
                       + weighted(WIN, gate(2)))
                o_t = jnp.concatenate([o_t, jnp.zeros_like(o_t)], axis=0)
                o_ref[:, head_cols(g, j)] = o_t.T[:, :NSA_HEAD_DIM].astype(o_ref.dtype)

    pl.when(qi == 0)(lambda: whole_step(0))
    pl.when(qi == 1)(lambda: whole_step(1))
    pl.when(qi >= 2)(lambda: whole_step(2))


def _nsa(q, kvsw, k_cmp, v_cmp, sm, cb, tabs, to_sel_t, batch, seq):
    assert WINDOW == 2 * TQ and seq % TQ == 0
    nq = seq // TQ
    n_cmp_pad = seq // CMP_STRIDE
    rows = NSA_HPG * TQ
    cmp_spec = pl.BlockSpec((1, NSA_GROUPS, n_cmp_pad, NSA_HEAD_DIM), lambda i, b: (b, 0, 0, 0))
    return pl.pallas_call(
        _nsa_kernel,
        grid=(nq, batch),
        in_specs=[
            pl.BlockSpec((TQ, NSA_Q), lambda i, b: (b * nq + i, 0)),
            pl.BlockSpec((seq, 4 * NSA_KV), lambda i, b: (b, 0)),
            cmp_spec, cmp_spec,
            pl.BlockSpec((TQ, LANES), lambda i, b: (b * nq + i, 0)),
            pl.BlockSpec((NSA_GROUPS, 1, n_cmp_pad, rows), lambda i, b: (0, i, 0, 0)),
            _const_spec((NSA_GROUPS, 3, rows, TQ)),
            _const_spec((LANES, n_cmp_pad)),
        ],
        out_specs=pl.BlockSpec((TQ, NSA_Q), lambda i, b: (b * nq + i, 0)),
        out_shape=jax.ShapeDtypeStruct((batch * seq, NSA_Q), BF16),
        scratch_shapes=[pltpu.VMEM((2, NSA_GROUPS, NSA_HPG, 8, TQ), F32),
                        pltpu.VMEM((2, NSA_GROUPS, NSA_HPG, NSA_ACC_ROWS, TQ), F32)],
        compiler_params=_params("parallel", "parallel"),
        name="nsa",
    )(q, kvsw, k_cmp, v_cmp, sm, cb, tabs, to_sel_t)


def _softplus(x):
    return jnp.maximum(x, 0.0) + jnp.log(1.0 + jnp.exp(-jnp.abs(x)))


def _l2_normalize(x):
    return x * lax.rsqrt(jnp.sum(x * x, axis=-1, keepdims=True) + NORM_EPS)


def _dn_kernel(dn_ref, z_ref, sm_ref, cw_ref, alog_ref, dtb_ref, gn_ref, o_ref, state_ref):
    c = DN_CHUNK
    n_chunks = dn_ref.shape[0] // c
    ii = lax.broadcasted_iota(jnp.int32, (c, c), 0)
    jj = lax.broadcasted_iota(jnp.int32, (c, c), 1)
    causal = ii >= jj
    strict = ii > jj
    split = ii ^ jj
    tri_ones = jnp.where(causal, 1.0, 0.0).astype(BF16)
    eye = jnp.where(ii == jj, 1.0, 0.0)
    state_ref[...] = jnp.zeros(state_ref.shape, F32)

    span = c * DN_GROUP

    def group(gi, carry):
        r0 = pl.multiple_of(gi * span, span)
        rows = pl.ds(r0, span)
        x = dn_ref[rows, :].astype(F32)
        halo_rows = pl.ds(pl.multiple_of(jnp.maximum(r0 - CONV_HALO, 0), CONV_HALO), CONV_HALO)
        halo = dn_ref[halo_rows, :].astype(F32) * jnp.where(gi > 0, 1.0, 0.0)
        xc = jnp.concatenate([halo, x], axis=0)
        y = x * cw_ref[DN_CONV - 1:DN_CONV, :]
        for k in range(1, DN_CONV):
            y = y + pltpu.roll(xc, k, 0)[CONV_HALO:, :] * cw_ref[DN_CONV - 1 - k:DN_CONV - k, :]
        y = _silu(y)

        smc = sm_ref[rows, :]
        log_decay = -jnp.exp(alog_ref[...]) * _softplus(smc + dtb_ref[...])
        beta = jax.nn.sigmoid(smc)
        ld1 = log_decay.astype(BF16)
        rest = log_decay - ld1.astype(F32)
        ld2 = rest.astype(BF16)
        ld3 = (rest - ld2.astype(F32)).astype(BF16)

        q_l, k_l, v_l, gcol_l, grow_l, gend_l, bcol_l = [], [], [], [], [], [], []
        for k in range(DN_GROUP):
            cr = slice(k * c, (k + 1) * c)
            gc = _dot(tri_ones, ld1[cr]) + _dot(tri_ones, ld2[cr]) + _dot(tri_ones, ld3[cr])
            gc_t = gc.T
            for h in range(DN_HEADS):
                q_l.append(_l2_normalize(y[cr, h * DN_HEAD_DIM:(h + 1) * DN_HEAD_DIM]) * DN_HEAD_DIM ** -0.5)
                k_l.append(_l2_normalize(y[cr, DN_W + h * DN_HEAD_DIM:DN_W + (h + 1) * DN_HEAD_DIM]))
                v_l.append(y[cr, 2 * DN_W + h * DN_HEAD_DIM:2 * DN_W + (h + 1) * DN_HEAD_DIM])
                gcol_l.append(gc[:, SM_A + h:SM_A + h + 1])
                grow_l.append(gc_t[SM_A + h:SM_A + h + 1, :])
                gend_l.append(gc[c - 1:c, SM_A + h:SM_A + h + 1])
                bcol_l.append(beta[cr, SM_B + h:SM_B + h + 1])
        qs, ks, vs = jnp.stack(q_l), jnp.stack(k_l), jnp.stack(v_l)
        g_col, g_row, g_end, b_col = jnp.stack(gcol_l), jnp.stack(grow_l), jnp.stack(gend_l), jnp.stack(bcol_l)

        decay = jnp.exp(jnp.where(causal, g_col - g_row, NEG_INF))
        qb = qs.astype(BF16)
        kb = ks.astype(BF16)
        n_mat = jnp.where(strict, b_col * _bmm_nt(kb, kb) * decay, 0.0)
        inv = eye - jnp.where(split == 1, n_mat, 0.0)
        for lvl in range(1, DN_LEVELS):
            n_lvl = jnp.where(jnp.right_shift(split, lvl) == 1, n_mat, 0.0).astype(BF16)
            inv_b = inv.astype(BF16)
            inv = inv - _bmm(inv_b, _bmm(n_lvl, inv_b).astype(BF16))
        e_col = jnp.exp(g_col)
        rhs = jnp.concatenate([vs * b_col, ks * (b_col * e_col)], axis=2).astype(BF16)
        sol = _bmm(inv.astype(BF16), rhs)
        u = sol[:, :, :DN_HEAD_DIM]
        w = sol[:, :, DN_HEAD_DIM:].astype(BF16)
        qk = (_bmm_nt(qb, kb) * decay).astype(BF16)
        q_dec = (qs * e_col).astype(BF16)
        k_dec = ks * jnp.exp(g_end - g_col)
        k_dec_t = jnp.stack([k_dec[n].T for n in range(DN_GROUP * DN_HEADS)]).astype(BF16)
        g_chunk = jnp.exp(g_end)

        state = state_ref[...]
        for k in range(DN_GROUP):
            hs = slice(k * DN_HEADS, (k + 1) * DN_HEADS)
            state_b = state.astype(BF16)
            v_new = (u[hs] - _bmm(w[hs], state_b)).astype(BF16)
            o = _bmm(q_dec[hs], state_b) + _bmm(qk[hs], v_new)
            state = state * g_chunk[hs] + _bmm(k_dec_t[hs], v_new)
            out_rows = pl.ds(r0 + k * c, c)
            for h in range(DN_HEADS):
                hd = slice(h * DN_HEAD_DIM, (h + 1) * DN_HEAD_DIM)
                zh = z_ref[out_rows, hd].astype(F32)
                o_ref[out_rows, hd] = (_rms(o[h], gn_ref[...]) * _silu(zh)).astype(o_ref.dtype)
        state_ref[...] = state
        return carry

    lax.fori_loop(0, n_chunks // DN_GROUP, group, 0)


def _dn(dn, z, sm, conv_w, a_log, dt_bias, out_gain, batch, seq):
    assert DN_CHUNK == 1 << DN_LEVELS and seq % (DN_CHUNK * DN_GROUP) == 0
    pad = jnp.zeros((1, LANES), F32)
    return pl.pallas_call(
        _dn_kernel,
        grid=(batch,),
        in_specs=[pl.BlockSpec((seq, 3 * DN_W), lambda b: (b, 0)), pl.BlockSpec((seq, DN_W), lambda b: (b, 0)),
                  pl.BlockSpec((seq, LANES), lambda b: (b, 0)), _const_spec((DN_CONV, 3 * DN_W)),
                  _const_spec((1, LANES)), _const_spec((1, LANES)), _const_spec((1, DN_HEAD_DIM))],
        out_specs=pl.BlockSpec((seq, DN_W), lambda b: (b, 0)),
        out_shape=jax.ShapeDtypeStruct((batch * seq, DN_W), BF16),
        scratch_shapes=[pltpu.VMEM((DN_HEADS, DN_HEAD_DIM, DN_HEAD_DIM), F32)],
        compiler_params=_params("parallel"),
        name="dn",
    )(dn, z, sm, conv_w.astype(F32), pad.at[0, SM_A:SM_A + DN_HEADS].set(a_log.astype(F32)),
      pad.at[0, SM_A:SM_A + DN_HEADS].set(dt_bias.astype(F32)), out_gain.reshape(1, -1).astype(F32))


def _memkv_kernel(mem_ref, g_ref, w_ref, k_ref, v_ref):
    mn = _rms(mem_ref[0], g_ref[...]).astype(BF16)
    k_ref[0] = _dot(mn, w_ref[:, :X_W]).astype(k_ref.dtype)
    v_ref[0] = _dot(mn, w_ref[:, X_W:]).astype(v_ref.dtype)


def _memkv(mem, gain, w_kv):
    batch, n_mem, _ = mem.shape
    out = pl.BlockSpec((1, n_mem, X_W), lambda b: (b, 0, 0))
    shape = jax.ShapeDtypeStruct((batch, n_mem, X_W), BF16)
    return pl.pallas_call(
        _memkv_kernel,
        grid=(batch,),
        in_specs=[pl.BlockSpec((1, n_mem, D_MODEL), lambda b: (b, 0, 0)), _const_spec((1, D_MODEL)),
                  _const_spec((D_MODEL, 2 * X_W))],
        out_specs=[out, out],
        out_shape=[shape, shape],
        compiler_params=_params("parallel"),
        name="memkv",
    )(mem, gain.reshape(1, -1), w_kv.astype(BF16))


def _memory_attention(q_ref, k_ref, v_ref):
    cols = [slice(h * X_HEAD_DIM, (h + 1) * X_HEAD_DIM) for h in range(X_HEADS)]
    s = [_dot_nt(q_ref[:, hd], k_ref[0, :, hd]) for hd in cols]
    e = [jnp.exp(x - jnp.max(x, axis=-1, keepdims=True)) for x in s]
    p = [(x / jnp.sum(x, axis=-1, keepdims=True)).astype(BF16) for x in e]
    return jnp.concatenate([_dot(ph, v_ref[0, :, hd]).astype(BF16) for hd, ph in zip(cols, p)], axis=1)


def _merge_kernel(x_ref, on_ref, od_ref, mq_ref, mk_ref, mv_ref, gpre_ref, wbg_ref, wbn_ref, wbd_ref, wbm_ref, wo_ref,
                  gpost_ref, o_ref):
    x = x_ref[...]
    u = _rms(x, gpre_ref[...]).astype(BF16)
    branches = (on_ref[...], od_ref[...], _memory_attention(mq_ref, mk_ref, mv_ref))
    merged = jnp.zeros(x.shape, F32)
    for i, (branch, w_ref) in enumerate(zip(branches, (wbn_ref, wbd_ref, wbm_ref))):
        gates = jax.nn.sigmoid(_dot(u, wbg_ref[:, i * D_MODEL:(i + 1) * D_MODEL]))
        merged = merged + gates * _dot(branch, w_ref[...])
    y = _dot(merged.astype(BF16), wo_ref[...])
    o_ref[...] = x + _rms(y, gpost_ref[...])


def _merge(x, o_nsa, o_dn, mq, mk, mv, gpre, wbg, wbn, wbd, wbm, wo, gpost, seq):
    t = x.shape[0]
    assert seq % TM == 0
    n_mem = mk.shape[1]
    row = pl.BlockSpec((TM, D_MODEL), lambda i: (i, 0))
    br = pl.BlockSpec((TM, NSA_Q), lambda i: (i, 0))
    mem = pl.BlockSpec((1, n_mem, X_W), lambda i: (i // (seq // TM), 0, 0))
    return pl.pallas_call(
        _merge_kernel,
        grid=(t // TM,),
        in_specs=[row, br, br, br, mem, mem, _const_spec((1, D_MODEL)), _const_spec((D_MODEL, N_BRANCH * D_MODEL)),
                  _const_spec((NSA_Q, D_MODEL)), _const_spec((DN_W, D_MODEL)), _const_spec((X_W, D_MODEL)),
                  _const_spec((D_MODEL, D_MODEL)), _const_spec((1, D_MODEL))],
        out_specs=row,
        out_shape=jax.ShapeDtypeStruct((t, D_MODEL), F32),
        compiler_params=_params("parallel"),
        name="merge",
    )(x, o_nsa, o_dn, mq, mk, mv, gpre.reshape(1, -1), wbg.astype(BF16), wbn.astype(BF16), wbd.astype(BF16),
      wbm.astype(BF16), wo.astype(BF16), gpost.reshape(1, -1))


def kernel(x, mem, ffn1_pre_norm, ffn1_w_gate, ffn1_w_up, ffn1_w_down, ffn1_post_norm, mix_pre_norm, w_in, cmp_pos_k, cmp_pos_v, cmp_k_w1, cmp_k_w2, cmp_v_w1, cmp_v_w2, rel_bias, dn_conv_w, dn_a_log, dn_dt_bias, dn_out_norm, mem_norm, w_mem_kv, w_branch_nsa, w_branch_dn, w_branch_mem, w_branch_gate, w_out, mix_post_norm, ffn2_pre_norm, ffn2_w_gate, ffn2_w_up, ffn2_w_down, ffn2_post_norm):
    batch, seq, d = x.shape
    xt = x.reshape(batch * seq, d)
    cb, tabs, to_sel_t = _nsa_tables(rel_bias, seq)
    for l in range(ffn1_pre_norm.shape[0]):
        xt = _ffn(xt, ffn1_pre_norm[l], ffn1_w_gate[l], ffn1_w_up[l], ffn1_w_down[l], ffn1_post_norm[l])

        q, kc, vc, kvsw, dn, z, mq, sm = _proj(xt, mix_pre_norm[l], _arrange_w_in(w_in[l]))
        posk, w1k, w2k = _arrange_cmp(cmp_pos_k[l], cmp_k_w1[l], cmp_k_w2[l])
        posv, w1v, w2v = _arrange_cmp(cmp_pos_v[l], cmp_v_w1[l], cmp_v_w2[l])
        k_cmp, v_cmp = _cmp(kc, vc, posk, posv, w1k, w1v, w2k, w2v, batch, seq)
        o_nsa = _nsa(q, kvsw, k_cmp, v_cmp, sm, cb, tabs, to_sel_t, batch, seq)
        o_dn = _dn(dn, z, sm, dn_conv_w[l], dn_a_log[l], dn_dt_bias[l], dn_out_norm[l], batch, seq)
        mk, mv = _memkv(mem, mem_norm[l], w_mem_kv[l])
        xt = _merge(xt, o_nsa, o_dn, mq, mk, mv, mix_pre_norm[l], w_branch_gate[l], w_branch_nsa[l], w_branch_dn[l],
                    w_branch_mem[l], w_out[l], mix_post_norm[l], seq)

        xt = _ffn(xt, ffn2_pre_norm[l], ffn2_w_gate[l], ffn2_w_up[l], ffn2_w_down[l], ffn2_post_norm[l])
    return xt.reshape(batch, seq, d)
```

```python
import math

import jax
import jax.numpy as jnp
import numpy as np
from jax import lax
from jax.experimental import pallas as pl
from jax.experimental.pallas import tpu as pltpu

F32 = jnp.float32
BF16 = jnp.bfloat16

D_MODEL = 1024
D_FF = 2816
NORM_EPS = 1e-6
NEG_INF = -1e30
FORCE_SCORE = 1e4

NSA_HEADS = 8
NSA_GROUPS = 2
NSA_HPG = NSA_HEADS // NSA_GROUPS
NSA_HEAD_DIM = 64
CMP_BLOCK = 32
CMP_STRIDE = 16
CMP_HIDDEN = 256
SEL_BLOCK = 64
SEL_SHIFT = 6
SEL_TOPK = 16
WINDOW = 512
REL_BUCKETS = 32
REL_MAX_DIST = 128

DN_HEADS = 4
DN_HEAD_DIM = 128
DN_CONV = 4
X_HEADS = 4
X_HEAD_DIM = 128
N_BRANCH = 3

NSA_Q = NSA_HEADS * NSA_HEAD_DIM
NSA_KV = NSA_GROUPS * NSA_HEAD_DIM
DN_W = DN_HEADS * DN_HEAD_DIM
X_W = X_HEADS * X_HEAD_DIM

LANES = 128
VMEM_LIMIT = 56 * 1024 * 1024

TM = 1024
FF_CHUNK = 256
PROJ_CHUNK = 512
TQ = 256
NSA_ACC_ROWS = 80
NSA_WAVE = 4
DN_CHUNK = 128
DN_LEVELS = 7
DN_GROUP = 4
CONV_HALO = 16

SM_GATE = 0
SM_A = N_BRANCH * NSA_HEADS
SM_B = SM_A + DN_HEADS


def _dot(a, b):
    return jnp.dot(a, b, preferred_element_type=F32)


def _dot_nt(a, b):
    return lax.dot_general(a, b, (((1,), (1,)), ((), ())), preferred_element_type=F32)


def _bmm(a, b):
    return lax.dot_general(a, b, (((2,), (1,)), ((0,), (0,))), preferred_element_type=F32)


def _bmm_nt(a, b):
    return lax.dot_general(a, b, (((2,), (2,)), ((0,), (0,))), preferred_element_type=F32)


def _rms(x, gain):
    return x * lax.rsqrt(jnp.mean(x * x, axis=-1, keepdims=True) + NORM_EPS) * gain


def _silu(x):
    return x * jax.nn.sigmoid(x)


def _const_spec(shape):
    zeros = (0,) * len(shape)
    return pl.BlockSpec(shape, lambda *_: zeros, pipeline_mode=pl.Buffered(1))


def _params(*sem):
    return pltpu.CompilerParams(dimension_semantics=sem, vmem_limit_bytes=VMEM_LIMIT)


def _ffn_kernel(x_ref, pre_ref, wg_ref, wu_ref, wd_ref, post_ref, o_ref):
    x = x_ref[...]
    xn = _rms(x, pre_ref[...]).astype(BF16)
    acc = jnp.zeros(x.shape, F32)
    for c in range(D_FF // FF_CHUNK):
        cols = slice(c * FF_CHUNK, (c + 1) * FF_CHUNK)
        gate = _dot(xn, wg_ref[:, cols])
        up = _dot(xn, wu_ref[:, cols])
        acc = acc + _dot((_silu(gate) * up).astype(BF16), wd_ref[cols, :])
    o_ref[...] = x + 0.5 * _rms(acc, post_ref[...])


def _ffn(x, pre, wg, wu, wd, post):
    t = x.shape[0]
    row = pl.BlockSpec((TM, D_MODEL), lambda i: (i, 0))
    return pl.pallas_call(
        _ffn_kernel,
        grid=(t // TM,),
        in_specs=[row, _const_spec((1, D_MODEL)), _const_spec((D_MODEL, D_FF)), _const_spec((D_MODEL, D_FF)),
                  _const_spec((D_FF, D_MODEL)), _const_spec((1, D_MODEL))],
        out_specs=row,
        out_shape=jax.ShapeDtypeStruct((t, D_MODEL), F32),
        compiler_params=_params("parallel"),
        name="ffn",
    )(x, pre.reshape(1, -1), wg.astype(BF16), wu.astype(BF16), wd.astype(BF16), post.reshape(1, -1))


PROJ_SPLITS = (("q", NSA_Q), ("kc", NSA_KV), ("vc", NSA_KV), ("kvsw", 4 * NSA_KV), ("dn", 3 * DN_W),
               ("z", DN_W), ("mq", X_W), ("sm", LANES))
PROJ_WIDTH = sum(w for _, w in PROJ_SPLITS)
LOG2E = math.log2(math.e)
PROJ_SCALES = {"q": NSA_HEAD_DIM ** -0.5 * LOG2E, "mq": X_HEAD_DIM ** -0.5}


def _proj_kernel(x_ref, g_ref, w_ref, *out_refs):
    xn = _rms(x_ref[...], g_ref[...]).astype(BF16)
    start = 0
    for (name, width), ref in zip(PROJ_SPLITS, out_refs):
        for c0 in range(0, width, PROJ_CHUNK):
            cw = min(PROJ_CHUNK, width - c0)
            y = _dot(xn, w_ref[:, start + c0:start + c0 + cw])
            if name in PROJ_SCALES:
                y = y * PROJ_SCALES[name]
            ref[:, c0:c0 + cw] = y.astype(ref.dtype)
        start += width


def _arrange_w_in(w_in):
    o_kv = NSA_Q
    o_g = o_kv + 6 * NSA_KV
    o_dn = o_g + N_BRANCH * NSA_HEADS
    o_a = o_dn + 3 * DN_W
    o_z = o_a + 2 * DN_HEADS
    o_mq = o_z + DN_W
    kv = w_in[:, o_kv:o_g].reshape(D_MODEL, 6, NSA_GROUPS, NSA_HEAD_DIM)
    kvsw = kv[:, 2:6].transpose(0, 2, 1, 3).reshape(D_MODEL, 4 * NSA_KV)
    small = jnp.concatenate([w_in[:, o_g:o_dn], w_in[:, o_a:o_z],
                             jnp.zeros((D_MODEL, LANES - N_BRANCH * NSA_HEADS - 2 * DN_HEADS), w_in.dtype)], axis=1)
    return jnp.concatenate([w_in[:, :o_kv], kv[:, 0].reshape(D_MODEL, NSA_KV), kv[:, 1].reshape(D_MODEL, NSA_KV),
                            kvsw, w_in[:, o_dn:o_a], w_in[:, o_z:o_mq], w_in[:, o_mq:], small], axis=1).astype(BF16)


def _proj(x, gain, w_arranged):
    t = x.shape[0]
    outs, specs = [], []
    for name, width in PROJ_SPLITS:
        outs.append(jax.ShapeDtypeStruct((t, width), F32 if name == "sm" else BF16))
        specs.append(pl.BlockSpec((TM, width), lambda i: (i, 0)))
    return pl.pallas_call(
        _proj_kernel,
        grid=(t // TM,),
        in_specs=[pl.BlockSpec((TM, D_MODEL), lambda i: (i, 0)), _const_spec((1, D_MODEL)),
                  _const_spec((D_MODEL, PROJ_WIDTH))],
        out_specs=specs,
        out_shape=outs,
        compiler_params=_params("parallel"),
        name="proj",
    )(x, gain.reshape(1, -1), w_arranged)


CMP_ROW = CMP_STRIDE * NSA_KV


def _cmp_kernel(kc_ref, vc_ref, posk_ref, posv_ref, w1k_ref, w1v_ref, w2k_ref, w2v_ref, ko_ref, vo_ref):
    n_rows = kc_ref.shape[1]

    def one(src_ref, pos_ref, w1_ref, w2_ref, out_ref):
        src = src_ref[0].astype(F32)
        first = _dot((src + pos_ref[0:1, :]).astype(BF16), w1_ref[0])
        second = _dot((src + pos_ref[1:2, :]).astype(BF16), w1_ref[1])
        hidden = _silu(first + pltpu.roll(second, n_rows - 1, 0))
        out = _dot(hidden.astype(BF16), w2_ref[...])
        for g in range(NSA_GROUPS):
            out_ref[0, g] = out[:, g * NSA_HEAD_DIM:(g + 1) * NSA_HEAD_DIM].astype(out_ref.dtype)

    one(kc_ref, posk_ref, w1k_ref, w2k_ref, ko_ref)
    one(vc_ref, posv_ref, w1v_ref, w2v_ref, vo_ref)


def _arrange_cmp(pos, w1, w2):
    hid2 = NSA_GROUPS * CMP_HIDDEN
    w1r = w1.reshape(2, CMP_STRIDE, NSA_HEAD_DIM, CMP_HIDDEN)
    eye = jnp.eye(NSA_GROUPS, dtype=w1.dtype)
    w1g = jnp.einsum("hidk,ge->higdek", w1r, eye).reshape(2, CMP_ROW, hid2).astype(BF16)
    w2g = jnp.einsum("kd,ge->gked", w2, eye).reshape(hid2, NSA_KV).astype(BF16)
    posr = jnp.broadcast_to(pos.reshape(2, CMP_STRIDE, 1, NSA_HEAD_DIM),
                            (2, CMP_STRIDE, NSA_GROUPS, NSA_HEAD_DIM)).reshape(2, CMP_ROW).astype(F32)
    return posr, w1g, w2g


def _cmp(kc, vc, posk, posv, w1k, w1v, w2k, w2v, batch, seq):
    n_rows = seq // CMP_STRIDE
    src = pl.BlockSpec((1, n_rows, CMP_ROW), lambda b: (b, 0, 0))
    out = pl.BlockSpec((1, NSA_GROUPS, n_rows, NSA_HEAD_DIM), lambda b: (b, 0, 0, 0))
    out_shape = jax.ShapeDtypeStruct((batch, NSA_GROUPS, n_rows, NSA_HEAD_DIM), BF16)
    hid2 = NSA_GROUPS * CMP_HIDDEN
    return pl.pallas_call(
        _cmp_kernel,
        grid=(batch,),
        in_specs=[src, src, _const_spec((2, CMP_ROW)), _const_spec((2, CMP_ROW)),
                  _const_spec((2, CMP_ROW, hid2)), _const_spec((2, CMP_ROW, hid2)),
                  _const_spec((hid2, NSA_KV)), _const_spec((hid2, NSA_KV))],
        out_specs=[out, out],
        out_shape=[out_shape, out_shape],
        compiler_params=_params("parallel"),
        name="cmp",
    )(kc.reshape(batch, n_rows, CMP_ROW), vc.reshape(batch, n_rows, CMP_ROW), posk, posv, w1k, w1v, w2k, w2v)


def _rel_bucket_table(n):
    exact = REL_BUCKETS // 2
    dist = np.arange(n)
    d = np.maximum(dist, 1).astype(np.float32)
    log_bucket = exact + (np.log(d / np.float32(exact)) / np.float32(math.log(REL_MAX_DIST / exact))
                          * np.float32(REL_BUCKETS - exact)).astype(np.int32)
    return np.where(dist < exact, dist, np.minimum(log_bucket, REL_BUCKETS - 1))


def _nsa_tables(rel_bias, seq):
    bucket = _rel_bucket_table(seq + 2 * TQ)
    n_cmp_pad = seq // CMP_STRIDE
    bias = rel_bias.astype(F32).T * LOG2E
    neg = jnp.full((NSA_HEADS, 1), NEG_INF, F32)

    def by_distance(values, dist):
        cls = np.where(dist >= 0, bucket[np.maximum(dist, 0)], REL_BUCKETS).astype(np.int32)
        one_hot = (jnp.asarray(cls.reshape(1, -1))
                   == lax.broadcasted_iota(jnp.int32, (REL_BUCKETS + 1, 1), 0)).astype(F32)
        picked = jnp.dot(jnp.concatenate([values, neg], axis=1), one_hot, precision=lax.Precision.HIGHEST)
        return picked.reshape((NSA_GROUPS, NSA_HPG) + dist.shape)

    t = np.arange(seq).reshape(seq // TQ, 1, TQ)
    dist = t - (np.arange(n_cmp_pad)[None, :, None] * CMP_STRIDE + CMP_BLOCK - 1)
    cb = by_distance(bias, dist).transpose(0, 2, 3, 1, 4).reshape(NSA_GROUPS, seq // TQ, n_cmp_pad, NSA_HPG * TQ)
    shifted = bias - bias[:, REL_BUCKETS - 1:]
    q_minus_k = np.arange(TQ)[None, :] - np.arange(TQ)[:, None]
    t0 = by_distance(shifted, q_minus_k)
    t1 = by_distance(shifted, TQ + q_minus_k)
    w2 = jnp.asarray(np.where(q_minus_k < 0, 0.0, NEG_INF), F32)
    tabs = jnp.stack([t0, t1, jnp.broadcast_to(w2, t0.shape)], axis=1)
    tabs = tabs.reshape(NSA_GROUPS, 3, NSA_HPG * TQ, TQ)
    c0 = np.arange(n_cmp_pad)[None, :] * CMP_STRIDE
    s0 = np.arange(LANES)[:, None] * SEL_BLOCK
    overlap = np.maximum(np.minimum(c0 + CMP_BLOCK, s0 + SEL_BLOCK) - np.maximum(c0, s0), 0)
    valid = (np.arange(LANES)[:, None] < seq // SEL_BLOCK) & (np.arange(n_cmp_pad)[None, :] < n_cmp_pad - 1)
    to_sel_t = jnp.asarray(np.where(valid, overlap / CMP_BLOCK, 0.0), BF16)
    return cb, tabs, to_sel_t


SEL, WIN = 0, 1


def _nsa_kernel(q_ref, kv_ref, kc_ref, vc_ref, sm_ref, cb_ref, tab_ref, tosel_ref, o_ref, m_ref, acc_ref):
    qi = pl.program_id(1)
    n_sel_blocks = kv_ref.shape[0] // SEL_BLOCK
    sm = sm_ref[...]
    ones_cols = jnp.ones((TQ, NSA_HEAD_DIM), BF16)
    heads = range(NSA_HPG)
    groups = range(NSA_GROUPS)

    def head_rows(j):
        return slice(j * TQ, (j + 1) * TQ)

    def flash_step(streams, first=False):
        chains = []
        for si, (branch, g, q_list, k_t, v_t, add_fn) in enumerate(streams):
            v_ext_t = jnp.concatenate([v_t, ones_cols], axis=1).astype(F32).T[:NSA_ACC_ROWS].astype(BF16)
            chains += [(si, branch, g, j, (q_list[j], k_t), v_ext_t, add_fn) for j in heads]
        waves = [chains[i:i + NSA_WAVE] for i in range(0, len(chains), NSA_WAVE)]
        s, m_old, m_new, p = {}, {}, {}, {}
        for step in range(len(waves) + 2):
            for si, _, _, j, (q, k_t), _, add_fn in (waves[step] if step < len(waves) else ()):
                s[si, j] = _dot_nt(k_t, q) if add_fn is None else _dot_nt(k_t, q) + add_fn(j)
            for si, branch, g, j, _, _, _ in (waves[step - 1] if 0 < step <= len(waves) else ()):
                c = (si, j)
                m_new[c] = jnp.broadcast_to(jnp.max(s[c], axis=0, keepdims=True), (8, TQ))
                if not first:
                    m_old[c] = m_ref[branch, g, j]
                    m_new[c] = jnp.maximum(m_old[c], m_new[c])
                m_ref[branch, g, j] = m_new[c]
                p[c] = jnp.exp2((s[c] - m_new[c][0:1]).astype(BF16))
            for si, branch, g, j, _, v_ext_t, _ in (waves[step - 2] if step >= 2 else ()):
                c = (si, j)
                acc = _dot(v_ext_t, p[c])
                if not first:
                    acc = jnp.exp2(m_old[c] - m_new[c])[0:1] * acc_ref[branch, g, j] + acc
                acc_ref[branch, g, j] = acc

    def head_cols(g, j):
        c0 = (g * NSA_HPG + j) * NSA_HEAD_DIM
        return slice(c0, c0 + NSA_HEAD_DIM)

    q_heads = [[q_ref[:, head_cols(g, j)] for j in heads] for g in groups]

    def kv_tile(g, kt, which):
        c0 = (g * 4 + which) * NSA_HEAD_DIM
        return kv_ref[pl.ds(pl.multiple_of(kt * TQ, TQ), TQ), c0:c0 + NSA_HEAD_DIM]

    def compressed(g):
        cols = NSA_HPG * TQ
        tok = qi * TQ + (lax.broadcasted_iota(jnp.int32, (1, cols), 1) & (TQ - 1))
        s = _dot_nt(kc_ref[0, g], jnp.concatenate(q_heads[g], axis=0)) + cb_ref[g, 0]
        e = jnp.exp2(s - jnp.max(s, axis=0, keepdims=True))
        p = e * (jnp.where(tok >= CMP_BLOCK - 1, 1.0, 0.0) / jnp.sum(e, axis=0, keepdims=True))
        vc_pad = jnp.concatenate([vc_ref[0, g], jnp.zeros((vc_ref.shape[2], NSA_HEAD_DIM), BF16)], axis=1)
        out_t = _dot(vc_pad.astype(F32).T.astype(BF16), p.astype(BF16))
        p_sum = p[:, 0:TQ]
        for j in range(1, NSA_HPG):
            p_sum = p_sum + p[:, j * TQ:(j + 1) * TQ]
        return out_t, p_sum

    def dropped_blocks(p_sum):
        p_hi = p_sum.astype(BF16)
        p_lo = (p_sum - p_hi.astype(F32)).astype(BF16)
        imp_t = _dot(tosel_ref[...], p_hi) + _dot(tosel_ref[...], p_lo)
        imp_t = imp_t[0:n_sel_blocks]
        blk = lax.broadcasted_iota(jnp.int32, (n_sel_blocks, TQ), 0)
        cur = jnp.right_shift(qi * TQ + lax.broadcasted_iota(jnp.int32, (n_sel_blocks, TQ), 1), SEL_SHIFT)
        score = jnp.where(blk <= cur, imp_t, -FORCE_SCORE)
        for forced_blk in (0, cur, cur - 1):
            score = jnp.where(blk == forced_blk, FORCE_SCORE, score)
        ranks = []
        for r0 in range(0, n_sel_blocks, 8):
            mine = score[r0:r0 + 8]
            row = r0 + lax.broadcasted_iota(jnp.int32, (8, TQ), 0)
            rank = jnp.zeros(mine.shape, F32)
            for m_blk in range(n_sel_blocks):
                other = score[m_blk:m_blk + 1, :]
                if m_blk < r0:
                    rank = jnp.where(other >= mine, rank + 1.0, rank)
                elif m_blk >= r0 + 8:
                    rank = jnp.where(other > mine, rank + 1.0, rank)
                else:
                    tie = jnp.where(row > m_blk, 1.0, 0.0)
                    rank = rank + jnp.where(other > mine, 1.0, jnp.where(other == mine, tie, 0.0))
            ranks.append(rank)
        rank = jnp.concatenate(ranks, axis=0)
        drop_t = jnp.where(rank < SEL_TOPK, jnp.where(score > -1.0, 0.0, 1.0), 1.0)
        drop_t = jnp.concatenate([drop_t, jnp.ones((LANES - n_sel_blocks, TQ), F32)], axis=0)
        return drop_t.T.astype(BF16)

    def sel_stream(q_aug, g, kt, tab):
        key_blk = jnp.right_shift(kt * TQ + lax.broadcasted_iota(jnp.int32, (TQ, NSA_HEAD_DIM), 0), SEL_SHIFT)
        hit = lax.broadcasted_iota(jnp.int32, (TQ, NSA_HEAD_DIM), 1) == key_blk
        k_aug = jnp.concatenate([kv_tile(g, kt, 0), jnp.where(hit, NEG_INF, 0.0).astype(BF16)], axis=1)
        add_fn = None if tab is None else (lambda j: tab_ref[g, tab, head_rows(j), :])
        return SEL, g, q_aug[g], k_aug, kv_tile(g, kt, 1), add_fn

    def win_stream(g, kt, tab):
        return WIN, g, q_heads[g], kv_tile(g, kt, 2), kv_tile(g, kt, 3), lambda j: tab_ref[g, tab, head_rows(j), :]

    def whole_step(n_near):
        cmp_out = [compressed(g) for g in groups]
        drop = [dropped_blocks(cmp_out[g][1]) for g in groups]
        flash_step([win_stream(g, qi, 0) for g in groups], first=True)
        flash_step([win_stream(g, qi - 1 - n, 1 + n) for n in range(n_near) for g in groups])
        q_aug = [[jnp.concatenate([q_heads[g][j], drop[g][:, :NSA_HEAD_DIM]], axis=1) for j in heads]
                 for g in groups]
        flash_step([sel_stream(q_aug, g, qi, 0) for g in groups], first=True)
        flash_step([sel_stream(q_aug, g, qi - 1 - n, 1 if n == 0 else None) for n in range(n_near) for g in groups])
        if n_near == 2:
            n_far = qi - 2

            @pl.loop(0, n_far // 2)
            def _(pair):
                flash_step([sel_stream(q_aug, g, 2 * pair + n, None) for n in range(2) for g in groups])

            @pl.when(n_far % 2 == 1)
            def _():
                flash_step([sel_stream(q_aug, g, n_far - 1, None) for g in groups])

        gates_t = jax.nn.sigmoid(sm).T
        for g in groups:
            for j in heads:
                def gate(branch):
                    c = SM_GATE + branch * NSA_HEADS + g * NSA_HPG + j
                    return gates_t[c:c + 1, :]

                def weighted(branch, gate_row):
                    acc = acc_ref[branch, g, j]
                    return acc[:NSA_HEAD_DIM] * (gate_row / acc[NSA_HEAD_DIM:NSA_HEAD_DIM + 1, :])

                o_t = (gate(0) * cmp_out[g][0][:NSA_HEAD_DIM, head_rows(j)] + weighted(SEL, gate(1))
                       + weighted(WIN, gate(2)))
                o_t = jnp.concatenate([o_t, jnp.zeros_like(o_t)], axis=0)
                o_ref[:, head_cols(g, j)] = o_t.T[:, :NSA_HEAD_DIM].astype(o_ref.dtype)

    pl.when(qi == 0)(lambda: whole_step(0))
    pl.when(qi == 1)(lambda: whole_step(1))
    pl.when(qi >= 2)(lambda: whole_step(2))


def _nsa(q, kvsw, k_cmp, v_cmp, sm, cb, tabs, to_sel_t, batch, seq):
    assert WINDOW == 2 * TQ and seq % TQ == 0
    nq = seq // TQ
    n_cmp_pad = seq // CMP_STRIDE
    rows = NSA_HPG * TQ
    cmp_spec = pl.BlockSpec((1, NSA_GROUPS, n_cmp_pad, NSA_HEAD_DIM), lambda b, i: (b, 0, 0, 0))
    return pl.pallas_call(
        _nsa_kernel,
        grid=(batch, nq),
        in_specs=[
            pl.BlockSpec((TQ, NSA_Q), lambda b, i: (b * nq + i, 0)),
            pl.BlockSpec((seq, 4 * NSA_KV), lambda b, i: (b, 0)),
            cmp_spec, cmp_spec,
            pl.BlockSpec((TQ, LANES), lambda b, i: (b * nq + i, 0)),
            pl.BlockSpec((NSA_GROUPS, 1, n_cmp_pad, rows), lambda b, i: (0, i, 0, 0)),
            _const_spec((NSA_GROUPS, 3, rows, TQ)),
            _const_spec((LANES, n_cmp_pad)),
        ],
        out_specs=pl.BlockSpec((TQ, NSA_Q), lambda b, i: (b * nq + i, 0)),
        out_shape=jax.ShapeDtypeStruct((batch * seq, NSA_Q), BF16),
        scratch_shapes=[pltpu.VMEM((2, NSA_GROUPS, NSA_HPG, 8, TQ), F32),
                        pltpu.VMEM((2, NSA_GROUPS, NSA_HPG, NSA_ACC_ROWS, TQ), F32)],
        compiler_params=_params("parallel", "arbitrary"),
        name="nsa",
    )(q, kvsw, k_cmp, v_cmp, sm, cb, tabs, to_sel_t)


def _softplus(x):
    return jnp.maximum(x, 0.0) + jnp.log(1.0 + jnp.exp(-jnp.abs(x)))


def _l2_normalize(x):
    return x * lax.rsqrt(jnp.sum(x * x, axis=-1, keepdims=True) + NORM_EPS)


def _dn_kernel(dn_ref, z_ref, sm_ref, cw_ref, alog_ref, dtb_ref, gn_ref, o_ref, state_ref):
    c = DN_CHUNK
    n_chunks = dn_ref.shape[0] // c
    ii = lax.broadcasted_iota(jnp.int32, (c, c), 0)
    jj = lax.broadcasted_iota(jnp.int32, (c, c), 1)
    causal = ii >= jj
    strict = ii > jj
    split = ii ^ jj
    tri_ones = jnp.where(causal, 1.0, 0.0).astype(BF16)
    eye = jnp.where(ii == jj, 1.0, 0.0)
    state_ref[...] = jnp.zeros(state_ref.shape, F32)

    span = c * DN_GROUP

    def group(gi, carry):
        r0 = pl.multiple_of(gi * span, span)
        rows = pl.ds(r0, span)
        x = dn_ref[rows, :].astype(F32)
        halo_rows = pl.ds(pl.multiple_of(jnp.maximum(r0 - CONV_HALO, 0), CONV_HALO), CONV_HALO)
        halo = dn_ref[halo_rows, :].astype(F32) * jnp.where(gi > 0, 1.0, 0.0)
        xc = jnp.concatenate([halo, x], axis=0)
        y = x * cw_ref[DN_CONV - 1:DN_CONV, :]
        for k in range(1, DN_CONV):
            y = y + pltpu.roll(xc, k, 0)[CONV_HALO:, :] * cw_ref[DN_CONV - 1 - k:DN_CONV - k, :]
        y = _silu(y)

        smc = sm_ref[rows, :]
        log_decay = -jnp.exp(alog_ref[...]) * _softplus(smc + dtb_ref[...])
        beta = jax.nn.sigmoid(smc)
        ld1 = log_decay.astype(BF16)
        rest = log_decay - ld1.astype(F32)
        ld2 = rest.astype(BF16)
        ld3 = (rest - ld2.astype(F32)).astype(BF16)

        q_l, k_l, v_l, gcol_l, grow_l, gend_l, bcol_l = [], [], [], [], [], [], []
        for k in range(DN_GROUP):
            cr = slice(k * c, (k + 1) * c)
            gc = _dot(tri_ones, ld1[cr]) + _dot(tri_ones, ld2[cr]) + _dot(tri_ones, ld3[cr])
            gc_t = gc.T
            for h in range(DN_HEADS):
                q_l.append(_l2_normalize(y[cr, h * DN_HEAD_DIM:(h + 1) * DN_HEAD_DIM]) * DN_HEAD_DIM ** -0.5)
                k_l.append(_l2_normalize(y[cr, DN_W + h * DN_HEAD_DIM:DN_W + (h + 1) * DN_HEAD_DIM]))
                v_l.append(y[cr, 2 * DN_W + h * DN_HEAD_DIM:2 * DN_W + (h + 1) * DN_HEAD_DIM])
                gcol_l.append(gc[:, SM_A + h:SM_A + h + 1])
                grow_l.append(gc_t[SM_A + h:SM_A + h + 1, :])
                gend_l.append(gc[c - 1:c, SM_A + h:SM_A + h + 1])
                bcol_l.append(beta[cr, SM_B + h:SM_B + h + 1])
        qs, ks, vs = jnp.stack(q_l), jnp.stack(k_l), jnp.stack(v_l)
        g_col, g_row, g_end, b_col = jnp.stack(gcol_l), jnp.stack(grow_l), jnp.stack(gend_l), jnp.stack(bcol_l)

        decay = jnp.exp(jnp.where(causal, g_col - g_row, NEG_INF))
        qb = qs.astype(BF16)
        kb = ks.astype(BF16)
        n_mat = jnp.where(strict, b_col * _bmm_nt(kb, kb) * decay, 0.0)
        inv = eye - jnp.where(split == 1, n_mat, 0.0)
        for lvl in range(1, DN_LEVELS):
            n_lvl = jnp.where(jnp.right_shift(split, lvl) == 1, n_mat, 0.0).astype(BF16)
            inv_b = inv.astype(BF16)
            inv = inv - _bmm(inv_b, _bmm(n_lvl, inv_b).astype(BF16))
        e_col = jnp.exp(g_col)
        rhs = jnp.concatenate([vs * b_col, ks * (b_col * e_col)], axis=2).astype(BF16)
        sol = _bmm(inv.astype(BF16), rhs)
        u = sol[:, :, :DN_HEAD_DIM]
        w = sol[:, :, DN_HEAD_DIM:].astype(BF16)
        qk = (_bmm_nt(qb, kb) * decay).astype(BF16)
        q_dec = (qs * e_col).astype(BF16)
        k_dec = ks * jnp.exp(g_end - g_col)
        k_dec_t = jnp.stack([k_dec[n].T for n in range(DN_GROUP * DN_HEADS)]).astype(BF16)
        g_chunk = jnp.exp(g_end)

        state = state_ref[...]
        for k in range(DN_GROUP):
            hs = slice(k * DN_HEADS, (k + 1) * DN_HEADS)
            state_b = state.astype(BF16)
            v_new = (u[hs] - _bmm(w[hs], state_b)).astype(BF16)
            o = _bmm(q_dec[hs], state_b) + _bmm(qk[hs], v_new)
            state = state * g_chunk[hs] + _bmm(k_dec_t[hs], v_new)
            out_rows = pl.ds(r0 + k * c, c)
            for h in range(DN_HEADS):
                hd = slice(h * DN_HEAD_DIM, (h + 1) * DN_HEAD_DIM)
                zh = z_ref[out_rows, hd].astype(F32)
                o_ref[out_rows, hd] = (_rms(o[h], gn_ref[...]) * _silu(zh)).astype(o_ref.dtype)
        state_ref[...] = state
        return carry

    lax.fori_loop(0, n_chunks // DN_GROUP, group, 0)


def _dn(dn, z, sm, conv_w, a_log, dt_bias, out_gain, batch, seq):
    assert DN_CHUNK == 1 << DN_LEVELS and seq % (DN_CHUNK * DN_GROUP) == 0
    pad = jnp.zeros((1, LANES), F32)
    return pl.pallas_call(
        _dn_kernel,
        grid=(batch,),
        in_specs=[pl.BlockSpec((seq, 3 * DN_W), lambda b: (b, 0)), pl.BlockSpec((seq, DN_W), lambda b: (b, 0)),
                  pl.BlockSpec((seq, LANES), lambda b: (b, 0)), _const_spec((DN_CONV, 3 * DN_W)),
                  _const_spec((1, LANES)), _const_spec((1, LANES)), _const_spec((1, DN_HEAD_DIM))],
        out_specs=pl.BlockSpec((seq, DN_W), lambda b: (b, 0)),
        out_shape=jax.ShapeDtypeStruct((batch * seq, DN_W), BF16),
        scratch_shapes=[pltpu.VMEM((DN_HEADS, DN_HEAD_DIM, DN_HEAD_DIM), F32)],
        compiler_params=_params("parallel"),
        name="dn",
    )(dn, z, sm, conv_w.astype(F32), pad.at[0, SM_A:SM_A + DN_HEADS].set(a_log.astype(F32)),
      pad.at[0, SM_A:SM_A + DN_HEADS].set(dt_bias.astype(F32)), out_gain.reshape(1, -1).astype(F32))


def _memkv_kernel(mem_ref, g_ref, w_ref, k_ref, v_ref):
    mn = _rms(mem_ref[0], g_ref[...]).astype(BF16)
    k_ref[0] = _dot(mn, w_ref[:, :X_W]).astype(k_ref.dtype)
    v_ref[0] = _dot(mn, w_ref[:, X_W:]).astype(v_ref.dtype)


def _memkv(mem, gain, w_kv):
    batch, n_mem, _ = mem.shape
    out = pl.BlockSpec((1, n_mem, X_W), lambda b: (b, 0, 0))
    shape = jax.ShapeDtypeStruct((batch, n_mem, X_W), BF16)
    return pl.pallas_call(
        _memkv_kernel,
        grid=(batch,),
        in_specs=[pl.BlockSpec((1, n_mem, D_MODEL), lambda b: (b, 0, 0)), _const_spec((1, D_MODEL)),
                  _const_spec((D_MODEL, 2 * X_W))],
        out_specs=[out, out],
        out_shape=[shape, shape],
        compiler_params=_params("parallel"),
        name="memkv",
    )(mem, gain.reshape(1, -1), w_kv.astype(BF16))


def _memory_attention(q_ref, k_ref, v_ref):
    cols = [slice(h * X_HEAD_DIM, (h + 1) * X_HEAD_DIM) for h in range(X_HEADS)]
    s = [_dot_nt(q_ref[:, hd], k_ref[0, :, hd]) for hd in cols]
    e = [jnp.exp(x - jnp.max(x, axis=-1, keepdims=True)) for x in s]
    p = [(x / jnp.sum(x, axis=-1, keepdims=True)).astype(BF16) for x in e]
    return jnp.concatenate([_dot(ph, v_ref[0, :, hd]).astype(BF16) for hd, ph in zip(cols, p)], axis=1)


def _merge_kernel(x_ref, on_ref, od_ref, mq_ref, mk_ref, mv_ref, gpre_ref, wbg_ref, wbn_ref, wbd_ref, wbm_ref, wo_ref,
                  gpost_ref, o_ref):
    x = x_ref[...]
    u = _rms(x, gpre_ref[...]).astype(BF16)
    branches = (on_ref[...], od_ref[...], _memory_attention(mq_ref, mk_ref, mv_ref))
    merged = jnp.zeros(x.shape, F32)
    for i, (branch, w_ref) in enumerate(zip(branches, (wbn_ref, wbd_ref, wbm_ref))):
        gates = jax.nn.sigmoid(_dot(u, wbg_ref[:, i * D_MODEL:(i + 1) * D_MODEL]))
        merged = merged + gates * _dot(branch, w_ref[...])
    y = _dot(merged.astype(BF16), wo_ref[...])
    o_ref[...] = x + _rms(y, gpost_ref[...])


def _merge(x, o_nsa, o_dn, mq, mk, mv, gpre, wbg, wbn, wbd, wbm, wo, gpost, seq):
    t = x.shape[0]
    assert seq % TM == 0
    n_mem = mk.shape[1]
    row = pl.BlockSpec((TM, D_MODEL), lambda i: (i, 0))
    br = pl.BlockSpec((TM, NSA_Q), lambda i: (i, 0))
    mem = pl.BlockSpec((1, n_mem, X_W), lambda i: (i // (seq // TM), 0, 0))
    return pl.pallas_call(
        _merge_kernel,
        grid=(t // TM,),
        in_specs=[row, br, br, br, mem, mem, _const_spec((1, D_MODEL)), _const_spec((D_MODEL, N_BRANCH * D_MODEL)),
                  _const_spec((NSA_Q, D_MODEL)), _const_spec((DN_W, D_MODEL)), _const_spec((X_W, D_MODEL)),
                  _const_spec((D_MODEL, D_MODEL)), _const_spec((1, D_MODEL))],
        out_specs=row,
        out_shape=jax.ShapeDtypeStruct((t, D_MODEL), F32),
        compiler_params=_params("parallel"),
        name="merge",
    )(x, o_nsa, o_dn, mq, mk, mv, gpre.reshape(1, -1), wbg.astype(BF16), wbn.astype(BF16), wbd.astype(BF16),
      wbm.astype(BF16), wo.astype(BF16), gpost.reshape(1, -1))


def kernel(x, mem, ffn1_pre_norm, ffn1_w_gate, ffn1_w_up, ffn1_w_down, ffn1_post_norm, mix_pre_norm, w_in, cmp_pos_k, cmp_pos_v, cmp_k_w1, cmp_k_w2, cmp_v_w1, cmp_v_w2, rel_bias, dn_conv_w, dn_a_log, dn_dt_bias, dn_out_norm, mem_norm, w_mem_kv, w_branch_nsa, w_branch_dn, w_branch_mem, w_branch_gate, w_out, mix_post_norm, ffn2_pre_norm, ffn2_w_gate, ffn2_w_up, ffn2_w_down, ffn2_post_norm):
    batch, seq, d = x.shape
    xt = x.reshape(batch * seq, d)
    cb, tabs, to_sel_t = _nsa_tables(rel_bias, seq)
    for l in range(ffn1_pre_norm.shape[0]):
        xt = _ffn(xt, ffn1_pre_norm[l], ffn1_w_gate[l], ffn1_w_up[l], ffn1_w_down[l], ffn1_post_norm[l])

        q, kc, vc, kvsw, dn, z, mq, sm = _proj(xt, mix_pre_norm[l], _arrange_w_in(w_in[l]))
        posk, w1k, w2k = _arrange_cmp(cmp_pos_k[l], cmp_k_w1[l], cmp_k_w2[l])
        posv, w1v, w2v = _arrange_cmp(cmp_pos_v[l], cmp_v_w1[l], cmp_v_w2[l])
        k_cmp, v_cmp = _cmp(kc, vc, posk, posv, w1k, w1v, w2k, w2v, batch, seq)
        o_nsa = _nsa(q, kvsw, k_cmp, v_cmp, sm, cb, tabs, to_sel_t, batch, seq)
        o_dn = _dn(dn, z, sm, dn_conv_w[l], dn_a_log[l], dn_dt_bias[l], dn_out_norm[l], batch, seq)
        mk, mv = _memkv(mem, mem_norm[l], w_mem_kv[l])
        xt = _merge(xt, o_nsa, o_dn, mq, mk, mv, mix_pre_norm[l], w_branch_gate[l], w_branch_nsa[l], w_branch_dn[l],
                    w_branch_mem[l], w_out[l], mix_post_norm[l], seq)

        xt = _ffn(xt, ffn2_pre_norm[l], ffn2_w_gate[l], ffn2_w_up[l], ffn2_w_down[l], ffn2_post_norm[l])
    return xt.reshape(batch, seq, d)
```

```python
import math

import jax
import jax.numpy as jnp
import numpy as np
from jax import lax
from jax.experimental import pallas as pl
from jax.experimental.pallas import tpu as pltpu

F32 = jnp.float32
BF16 = jnp.bfloat16

D_MODEL = 1024
D_FF = 2816
NORM_EPS = 1e-6
NEG_INF = -1e30
FORCE_SCORE = 1e4

NSA_HEADS = 8
NSA_GROUPS = 2
NSA_HPG = NSA_HEADS // NSA_GROUPS
NSA_HEAD_DIM = 64
CMP_BLOCK = 32
CMP_STRIDE = 16
CMP_HIDDEN = 256
SEL_BLOCK = 64
SEL_SHIFT = 6
SEL_TOPK = 16
WINDOW = 512
REL_BUCKETS = 32
REL_MAX_DIST = 128

DN_HEADS = 4
DN_HEAD_DIM = 128
DN_CONV = 4
X_HEADS = 4
X_HEAD_DIM = 128
N_BRANCH = 3

NSA_Q = NSA_HEADS * NSA_HEAD_DIM
NSA_KV = NSA_GROUPS * NSA_HEAD_DIM
DN_W = DN_HEADS * DN_HEAD_DIM
X_W = X_HEADS * X_HEAD_DIM

LANES = 128
VMEM_LIMIT = 56 * 1024 * 1024

TM = 1024
FF_CHUNK = 256
PROJ_CHUNK = 512
TQ = 256
NSA_ACC_ROWS = 80
NSA_WAVE = 4
DN_CHUNK = 128
DN_LEVELS = 7
DN_GROUP = 4
CONV_HALO = 16

SM_GATE = 0
SM_A = N_BRANCH * NSA_HEADS
SM_B = SM_A + DN_HEADS


def _dot(a, b):
    return jnp.dot(a, b, preferred_element_type=F32)


def _dot_nt(a, b):
    return lax.dot_general(a, b, (((1,), (1,)), ((), ())), preferred_element_type=F32)


def _bmm(a, b):
    return lax.dot_general(a, b, (((2,), (1,)), ((0,), (0,))), preferred_element_type=F32)


def _bmm_nt(a, b):
    return lax.dot_general(a, b, (((2,), (2,)), ((0,), (0,))), preferred_element_type=F32)


def _rms(x, gain):
    return x * lax.rsqrt(jnp.mean(x * x, axis=-1, keepdims=True) + NORM_EPS) * gain


def _silu(x):
    return x * jax.nn.sigmoid(x)


def _const_spec(shape):
    zeros = (0,) * len(shape)
    return pl.BlockSpec(shape, lambda *_: zeros, pipeline_mode=pl.Buffered(1))


def _params(*sem):
    return pltpu.CompilerParams(dimension_semantics=sem, vmem_limit_bytes=VMEM_LIMIT)


def _ffn_kernel(x_ref, pre_ref, wg_ref, wu_ref, wd_ref, post_ref, o_ref):
    x = x_ref[...]
    xn = _rms(x, pre_ref[...]).astype(BF16)
    acc = jnp.zeros(x.shape, F32)
    for c in range(D_FF // FF_CHUNK):
        cols = slice(c * FF_CHUNK, (c + 1) * FF_CHUNK)
        gate = _dot(xn, wg_ref[:, cols])
        up = _dot(xn, wu_ref[:, cols])
        acc = acc + _dot((_silu(gate) * up).astype(BF16), wd_ref[cols, :])
    o_ref[...] = x + 0.5 * _rms(acc, post_ref[...])


def _ffn(x, pre, wg, wu, wd, post):
    t = x.shape[0]
    row = pl.BlockSpec((TM, D_MODEL), lambda i: (i, 0))
    return pl.pallas_call(
        _ffn_kernel,
        grid=(t // TM,),
        in_specs=[row, _const_spec((1, D_MODEL)), _const_spec((D_MODEL, D_FF)), _const_spec((D_MODEL, D_FF)),
                  _const_spec((D_FF, D_MODEL)), _const_spec((1, D_MODEL))],
        out_specs=row,
        out_shape=jax.ShapeDtypeStruct((t, D_MODEL), F32),
        compiler_params=_params("parallel"),
        name="ffn",
    )(x, pre.reshape(1, -1), wg.astype(BF16), wu.astype(BF16), wd.astype(BF16), post.reshape(1, -1))


PROJ_SPLITS = (("q", NSA_Q), ("kc", NSA_KV), ("vc", NSA_KV), ("kvsw", 4 * NSA_KV), ("dn", 3 * DN_W),
               ("z", DN_W), ("mq", X_W), ("sm", LANES))
PROJ_WIDTH = sum(w for _, w in PROJ_SPLITS)
LOG2E = math.log2(math.e)
PROJ_SCALES = {"q": NSA_HEAD_DIM ** -0.5 * LOG2E, "mq": X_HEAD_DIM ** -0.5}


def _proj_kernel(x_ref, g_ref, w_ref, *out_refs):
    xn = _rms(x_ref[...], g_ref[...]).astype(BF16)
    start = 0
    for (name, width), ref in zip(PROJ_SPLITS, out_refs):
        for c0 in range(0, width, PROJ_CHUNK):
            cw = min(PROJ_CHUNK, width - c0)
            y = _dot(xn, w_ref[:, start + c0:start + c0 + cw])
            if name in PROJ_SCALES:
                y = y * PROJ_SCALES[name]
            ref[:, c0:c0 + cw] = y.astype(ref.dtype)
        start += width


def _arrange_w_in(w_in):
    o_kv = NSA_Q
    o_g = o_kv + 6 * NSA_KV
    o_dn = o_g + N_BRANCH * NSA_HEADS
    o_a = o_dn + 3 * DN_W
    o_z = o_a + 2 * DN_HEADS
    o_mq = o_z + DN_W
    kv = w_in[:, o_kv:o_g].reshape(D_MODEL, 6, NSA_GROUPS, NSA_HEAD_DIM)
    kvsw = kv[:, 2:6].transpose(0, 2, 1, 3).reshape(D_MODEL, 4 * NSA_KV)
    small = jnp.concatenate([w_in[:, o_g:o_dn], w_in[:, o_a:o_z],
                             jnp.zeros((D_MODEL, LANES - N_BRANCH * NSA_HEADS - 2 * DN_HEADS), w_in.dtype)], axis=1)
    return jnp.concatenate([w_in[:, :o_kv], kv[:, 0].reshape(D_MODEL, NSA_KV), kv[:, 1].reshape(D_MODEL, NSA_KV),
                            kvsw, w_in[:, o_dn:o_a], w_in[:, o_z:o_mq], w_in[:, o_mq:], small], axis=1).astype(BF16)


def _proj(x, gain, w_arranged):
    t = x.shape[0]
    outs, specs = [], []
    for name, width in PROJ_SPLITS:
        outs.append(jax.ShapeDtypeStruct((t, width), F32 if name == "sm" else BF16))
        specs.append(pl.BlockSpec((TM, width), lambda i: (i, 0)))
    return pl.pallas_call(
        _proj_kernel,
        grid=(t // TM,),
        in_specs=[pl.BlockSpec((TM, D_MODEL), lambda i: (i, 0)), _const_spec((1, D_MODEL)),
                  _const_spec((D_MODEL, PROJ_WIDTH))],
        out_specs=specs,
        out_shape=outs,
        compiler_params=_params("parallel"),
        name="proj",
    )(x, gain.reshape(1, -1), w_arranged)


CMP_ROW = CMP_STRIDE * NSA_KV


def _cmp_kernel(kc_ref, vc_ref, posk_ref, posv_ref, w1k_ref, w1v_ref, w2k_ref, w2v_ref, ko_ref, vo_ref):
    n_rows = kc_ref.shape[1]

    def one(src_ref, pos_ref, w1_ref, w2_ref, out_ref):
        src = src_ref[0].astype(F32)
        first = _dot((src + pos_ref[0:1, :]).astype(BF16), w1_ref[0])
        second = _dot((src + pos_ref[1:2, :]).astype(BF16), w1_ref[1])
        hidden = _silu(first + pltpu.roll(second, n_rows - 1, 0))
        out = _dot(hidden.astype(BF16), w2_ref[...])
        for g in range(NSA_GROUPS):
            out_ref[0, g] = out[:, g * NSA_HEAD_DIM:(g + 1) * NSA_HEAD_DIM].astype(out_ref.dtype)

    one(kc_ref, posk_ref, w1k_ref, w2k_ref, ko_ref)
    one(vc_ref, posv_ref, w1v_ref, w2v_ref, vo_ref)


def _arrange_cmp(pos, w1, w2):
    hid2 = NSA_GROUPS * CMP_HIDDEN
    w1r = w1.reshape(2, CMP_STRIDE, NSA_HEAD_DIM, CMP_HIDDEN)
    eye = jnp.eye(NSA_GROUPS, dtype=w1.dtype)
    w1g = jnp.einsum("hidk,ge->higdek", w1r, eye).reshape(2, CMP_ROW, hid2).astype(BF16)
    w2g = jnp.einsum("kd,ge->gked", w2, eye).reshape(hid2, NSA_KV).astype(BF16)
    posr = jnp.broadcast_to(pos.reshape(2, CMP_STRIDE, 1, NSA_HEAD_DIM),
                            (2, CMP_STRIDE, NSA_GROUPS, NSA_HEAD_DIM)).reshape(2, CMP_ROW).astype(F32)
    return posr, w1g, w2g


def _cmp(kc, vc, posk, posv, w1k, w1v, w2k, w2v, batch, seq):
    n_rows = seq // CMP_STRIDE
    src = pl.BlockSpec((1, n_rows, CMP_ROW), lambda b: (b, 0, 0))
    out = pl.BlockSpec((1, NSA_GROUPS, n_rows, NSA_HEAD_DIM), lambda b: (b, 0, 0, 0))
    out_shape = jax.ShapeDtypeStruct((batch, NSA_GROUPS, n_rows, NSA_HEAD_DIM), BF16)
    hid2 = NSA_GROUPS * CMP_HIDDEN
    return pl.pallas_call(
        _cmp_kernel,
        grid=(batch,),
        in_specs=[src, src, _const_spec((2, CMP_ROW)), _const_spec((2, CMP_ROW)),
                  _const_spec((2, CMP_ROW, hid2)), _const_spec((2, CMP_ROW, hid2)),
                  _const_spec((hid2, NSA_KV)), _const_spec((hid2, NSA_KV))],
        out_specs=[out, out],
        out_shape=[out_shape, out_shape],
        compiler_params=_params("parallel"),
        name="cmp",
    )(kc.reshape(batch, n_rows, CMP_ROW), vc.reshape(batch, n_rows, CMP_ROW), posk, posv, w1k, w1v, w2k, w2v)


def _rel_bucket_table(n):
    exact = REL_BUCKETS // 2
    dist = np.arange(n)
    d = np.maximum(dist, 1).astype(np.float32)
    log_bucket = exact + (np.log(d / np.float32(exact)) / np.float32(math.log(REL_MAX_DIST / exact))
                          * np.float32(REL_BUCKETS - exact)).astype(np.int32)
    return np.where(dist < exact, dist, np.minimum(log_bucket, REL_BUCKETS - 1))


def _nsa_tables(rel_bias, seq):
    bucket = _rel_bucket_table(seq + 2 * TQ)
    n_cmp_pad = seq // CMP_STRIDE
    bias = rel_bias.astype(F32).T * LOG2E
    neg = jnp.full((NSA_HEADS, 1), NEG_INF, F32)

    def by_distance(values, dist):
        cls = np.where(dist >= 0, bucket[np.maximum(dist, 0)], REL_BUCKETS).astype(np.int32)
        one_hot = (jnp.asarray(cls.reshape(1, -1))
                   == lax.broadcasted_iota(jnp.int32, (REL_BUCKETS + 1, 1), 0)).astype(F32)
        picked = jnp.dot(jnp.concatenate([values, neg], axis=1), one_hot, precision=lax.Precision.HIGHEST)
        return picked.reshape((NSA_GROUPS, NSA_HPG) + dist.shape)

    t = np.arange(seq).reshape(seq // TQ, 1, TQ)
    dist = t - (np.arange(n_cmp_pad)[None, :, None] * CMP_STRIDE + CMP_BLOCK - 1)
    cb = by_distance(bias, dist).transpose(0, 2, 3, 1, 4).reshape(NSA_GROUPS, seq // TQ, n_cmp_pad, NSA_HPG * TQ)
    shifted = bias - bias[:, REL_BUCKETS - 1:]
    q_minus_k = np.arange(TQ)[None, :] - np.arange(TQ)[:, None]
    t0 = by_distance(shifted, q_minus_k)
    t1 = by_distance(shifted, TQ + q_minus_k)
    w2 = jnp.asarray(np.where(q_minus_k < 0, 0.0, NEG_INF), F32)
    tabs = jnp.stack([t0, t1, jnp.broadcast_to(w2, t0.shape)], axis=1)
    tabs = tabs.reshape(NSA_GROUPS, 3, NSA_HPG * TQ, TQ)
    c0 = np.arange(n_cmp_pad)[None, :] * CMP_STRIDE
    s0 = np.arange(LANES)[:, None] * SEL_BLOCK
    overlap = np.maximum(np.minimum(c0 + CMP_BLOCK, s0 + SEL_BLOCK) - np.maximum(c0, s0), 0)
    valid = (np.arange(LANES)[:, None] < seq // SEL_BLOCK) & (np.arange(n_cmp_pad)[None, :] < n_cmp_pad - 1)
    to_sel_t = jnp.asarray(np.where(valid, overlap / CMP_BLOCK, 0.0), BF16)
    return cb, tabs, to_sel_t


SEL, WIN = 0, 1


def _nsa_kernel(q_ref, kv_ref, kc_ref, vc_ref, sm_ref, cb_ref, tab_ref, tosel_ref, o_ref, m_ref, acc_ref):
    qi = pl.program_id(1)
    n_sel_blocks = kv_ref.shape[0] // SEL_BLOCK
    sm = sm_ref[...]
    ones_cols = jnp.ones((TQ, NSA_HEAD_DIM), BF16)
    heads = range(NSA_HPG)
    groups = range(NSA_GROUPS)

    def head_rows(j):
        return slice(j * TQ, (j + 1) * TQ)

    def flash_step(streams, first=False):
        chains = []
        for si, (branch, g, q_list, k_t, v_t, add_fn) in enumerate(streams):
            v_ext_t = jnp.concatenate([v_t, ones_cols], axis=1).astype(F32).T[:NSA_ACC_ROWS].astype(BF16)
            chains += [(si, branch, g, j, (q_list[j], k_t), v_ext_t, add_fn) for j in heads]
        waves = [chains[i:i + NSA_WAVE] for i in range(0, len(chains), NSA_WAVE)]
        s, m_old, m_new, p = {}, {}, {}, {}
        for step in range(len(waves) + 2):
            for si, _, _, j, (q, k_t), _, add_fn in (waves[step] if step < len(waves) else ()):
                s[si, j] = _dot_nt(k_t, q) if add_fn is None else _dot_nt(k_t, q) + add_fn(j)
            for si, branch, g, j, _, _, _ in (waves[step - 1] if 0 < step <= len(waves) else ()):
                c = (si, j)
                m_new[c] = jnp.broadcast_to(jnp.max(s[c], axis=0, keepdims=True), (8, TQ))
                if not first:
                    m_old[c] = m_ref[branch, g, j]
                    m_new[c] = jnp.maximum(m_old[c], m_new[c])
                m_ref[branch, g, j] = m_new[c]
                p[c] = jnp.exp2((s[c] - m_new[c][0:1]).astype(BF16))
            for si, branch, g, j, _, v_ext_t, _ in (waves[step - 2] if step >= 2 else ()):
                c = (si, j)
                acc = _dot(v_ext_t, p[c])
                if not first:
                    acc = jnp.exp2(m_old[c] - m_new[c])[0:1] * acc_ref[branch, g, j] + acc
                acc_ref[branch, g, j] = acc

    def head_cols(g, j):
        c0 = (g * NSA_HPG + j) * NSA_HEAD_DIM
        return slice(c0, c0 + NSA_HEAD_DIM)

    q_heads = [[q_ref[:, head_cols(g, j)] for j in heads] for g in groups]

    def kv_tile(g, kt, which):
        c0 = (g * 4 + which) * NSA_HEAD_DIM
        return kv_ref[pl.ds(pl.multiple_of(kt * TQ, TQ), TQ), c0:c0 + NSA_HEAD_DIM]

    def compressed(g):
        cols = NSA_HPG * TQ
        tok = qi * TQ + (lax.broadcasted_iota(jnp.int32, (1, cols), 1) & (TQ - 1))
        s = _dot_nt(kc_ref[0, g], jnp.concatenate(q_heads[g], axis=0)) + cb_ref[g, 0]
        e = jnp.exp2(s - jnp.max(s, axis=0, keepdims=True))
        p = e * (jnp.where(tok >= CMP_BLOCK - 1, 1.0, 0.0) / jnp.sum(e, axis=0, keepdims=True))
        vc_pad = jnp.concatenate([vc_ref[0, g], jnp.zeros((vc_ref.shape[2], NSA_HEAD_DIM), BF16)], axis=1)
        out_t = _dot(vc_pad.astype(F32).T.astype(BF16), p.astype(BF16))
        p_sum = p[:, 0:TQ]
        for j in range(1, NSA_HPG):
            p_sum = p_sum + p[:, j * TQ:(j + 1) * TQ]
        return out_t, p_sum

    def dropped_blocks(p_sum):
        p_hi = p_sum.astype(BF16)
        p_lo = (p_sum - p_hi.astype(F32)).astype(BF16)
        imp_t = _dot(tosel_ref[...], p_hi) + _dot(tosel_ref[...], p_lo)
        imp_t = imp_t[0:n_sel_blocks]
        blk = lax.broadcasted_iota(jnp.int32, (n_sel_blocks, TQ), 0)
        cur = jnp.right_shift(qi * TQ + lax.broadcasted_iota(jnp.int32, (n_sel_blocks, TQ), 1), SEL_SHIFT)
        score = jnp.where(blk <= cur, imp_t, -FORCE_SCORE)
        for forced_blk in (0, cur, cur - 1):
            score = jnp.where(blk == forced_blk, FORCE_SCORE, score)
        ranks = []
        for r0 in range(0, n_sel_blocks, 8):
            mine = score[r0:r0 + 8]
            row = r0 + lax.broadcasted_iota(jnp.int32, (8, TQ), 0)
            rank = jnp.zeros(mine.shape, F32)
            for m_blk in range(n_sel_blocks):
                other = score[m_blk:m_blk + 1, :]
                if m_blk < r0:
                    rank = jnp.where(other >= mine, rank + 1.0, rank)
                elif m_blk >= r0 + 8:
                    rank = jnp.where(other > mine, rank + 1.0, rank)
                else:
                    tie = jnp.where(row > m_blk, 1.0, 0.0)
                    rank = rank + jnp.where(other > mine, 1.0, jnp.where(other == mine, tie, 0.0))
            ranks.append(rank)
        rank = jnp.concatenate(ranks, axis=0)
        drop_t = jnp.where(rank < SEL_TOPK, jnp.where(score > -1.0, 0.0, 1.0), 1.0)
        drop_t = jnp.concatenate([drop_t, jnp.ones((LANES - n_sel_blocks, TQ), F32)], axis=0)
        return drop_t.T.astype(BF16)

    def sel_stream(q_aug, g, kt, tab):
        key_blk = jnp.right_shift(kt * TQ + lax.broadcasted_iota(jnp.int32, (TQ, NSA_HEAD_DIM), 0), SEL_SHIFT)
        hit = lax.broadcasted_iota(jnp.int32, (TQ, NSA_HEAD_DIM), 1) == key_blk
        k_aug = jnp.concatenate([kv_tile(g, kt, 0), jnp.where(hit, NEG_INF, 0.0).astype(BF16)], axis=1)
        add_fn = None if tab is None else (lambda j: tab_ref[g, tab, head_rows(j), :])
        return SEL, g, q_aug[g], k_aug, kv_tile(g, kt, 1), add_fn

    def win_stream(g, kt, tab):
        return WIN, g, q_heads[g], kv_tile(g, kt, 2), kv_tile(g, kt, 3), lambda j: tab_ref[g, tab, head_rows(j), :]

    def whole_step(n_near):
        cmp_out = [compressed(g) for g in groups]
        drop = [dropped_blocks(cmp_out[g][1]) for g in groups]
        flash_step([win_stream(g, qi, 0) for g in groups], first=True)
        flash_step([win_stream(g, qi - 1 - n, 1 + n) for n in range(n_near) for g in groups])
        q_aug = [[jnp.concatenate([q_heads[g][j], drop[g][:, :NSA_HEAD_DIM]], axis=1) for j in heads]
                 for g in groups]
        flash_step([sel_stream(q_aug, g, qi, 0) for g in groups], first=True)
        flash_step([sel_stream(q_aug, g, qi - 1 - n, 1 if n == 0 else None) for n in range(n_near) for g in groups])
        if n_near == 2:
            n_far = qi - 2

            @pl.loop(0, n_far // 2)
            def _(pair):
                flash_step([sel_stream(q_aug, g, 2 * pair + n, None) for n in range(2) for g in groups])

            @pl.when(n_far % 2 == 1)
            def _():
                flash_step([sel_stream(q_aug, g, n_far - 1, None) for g in groups])

        gates_t = jax.nn.sigmoid(sm).T

        def head_out_t(g, j):
            def gate(branch):
                c = SM_GATE + branch * NSA_HEADS + g * NSA_HPG + j
                return gates_t[c:c + 1, :]

            def weighted(branch, gate_row):
                acc = acc_ref[branch, g, j]
                return acc[:NSA_HEAD_DIM] * (gate_row / acc[NSA_HEAD_DIM:NSA_HEAD_DIM + 1, :])

            return (gate(0) * cmp_out[g][0][:NSA_HEAD_DIM, head_rows(j)] + weighted(SEL, gate(1))
                    + weighted(WIN, gate(2)))

        for g in groups:
            for j in range(0, NSA_HPG, 2):
                pair_t = jnp.concatenate([head_out_t(g, j), head_out_t(g, j + 1)], axis=0)
                c0 = (g * NSA_HPG + j) * NSA_HEAD_DIM
                o_ref[:, c0:c0 + 2 * NSA_HEAD_DIM] = pair_t.T.astype(o_ref.dtype)

    pl.when(qi == 0)(lambda: whole_step(0))
    pl.when(qi == 1)(lambda: whole_step(1))
    pl.when(qi >= 2)(lambda: whole_step(2))


def _nsa(q, kvsw, k_cmp, v_cmp, sm, cb, tabs, to_sel_t, batch, seq):
    assert WINDOW == 2 * TQ and seq % TQ == 0
    nq = seq // TQ
    n_cmp_pad = seq // CMP_STRIDE
    rows = NSA_HPG * TQ
    cmp_spec = pl.BlockSpec((1, NSA_GROUPS, n_cmp_pad, NSA_HEAD_DIM), lambda b, i: (b, 0, 0, 0))
    return pl.pallas_call(
        _nsa_kernel,
        grid=(batch, nq),
        in_specs=[
            pl.BlockSpec((TQ, NSA_Q), lambda b, i: (b * nq + i, 0)),
            pl.BlockSpec((seq, 4 * NSA_KV), lambda b, i: (b, 0)),
            cmp_spec, cmp_spec,
            pl.BlockSpec((TQ, LANES), lambda b, i: (b * nq + i, 0)),
            pl.BlockSpec((NSA_GROUPS, 1, n_cmp_pad, rows), lambda b, i: (0, i, 0, 0)),
            _const_spec((NSA_GROUPS, 3, rows, TQ)),
            _const_spec((LANES, n_cmp_pad)),
        ],
        out_specs=pl.BlockSpec((TQ, NSA_Q), lambda b, i: (b * nq + i, 0)),
        out_shape=jax.ShapeDtypeStruct((batch * seq, NSA_Q), BF16),
        scratch_shapes=[pltpu.VMEM((2, NSA_GROUPS, NSA_HPG, 8, TQ), F32),
                        pltpu.VMEM((2, NSA_GROUPS, NSA_HPG, NSA_ACC_ROWS, TQ), F32)],
        compiler_params=_params("parallel", "arbitrary"),
        name="nsa",
    )(q, kvsw, k_cmp, v_cmp, sm, cb, tabs, to_sel_t)


def _softplus(x):
    return jnp.maximum(x, 0.0) + jnp.log(1.0 + jnp.exp(-jnp.abs(x)))


def _l2_normalize(x):
    return x * lax.rsqrt(jnp.sum(x * x, axis=-1, keepdims=True) + NORM_EPS)


def _dn_kernel(dn_ref, z_ref, sm_ref, cw_ref, alog_ref, dtb_ref, gn_ref, o_ref, state_ref):
    c = DN_CHUNK
    n_chunks = dn_ref.shape[0] // c
    ii = lax.broadcasted_iota(jnp.int32, (c, c), 0)
    jj = lax.broadcasted_iota(jnp.int32, (c, c), 1)
    causal = ii >= jj
    strict = ii > jj
    split = ii ^ jj
    tri_ones = jnp.where(causal, 1.0, 0.0).astype(BF16)
    eye = jnp.where(ii == jj, 1.0, 0.0)
    state_ref[...] = jnp.zeros(state_ref.shape, F32)

    span = c * DN_GROUP

    def group(gi, carry):
        r0 = pl.multiple_of(gi * span, span)
        rows = pl.ds(r0, span)
        x = dn_ref[rows, :].astype(F32)
        halo_rows = pl.ds(pl.multiple_of(jnp.maximum(r0 - CONV_HALO, 0), CONV_HALO), CONV_HALO)
        halo = dn_ref[halo_rows, :].astype(F32) * jnp.where(gi > 0, 1.0, 0.0)
        xc = jnp.concatenate([halo, x], axis=0)
        y = x * cw_ref[DN_CONV - 1:DN_CONV, :]
        for k in range(1, DN_CONV):
            y = y + pltpu.roll(xc, k, 0)[CONV_HALO:, :] * cw_ref[DN_CONV - 1 - k:DN_CONV - k, :]
        y = _silu(y)

        smc = sm_ref[rows, :]
        log_decay = -jnp.exp(alog_ref[...]) * _softplus(smc + dtb_ref[...])
        beta = jax.nn.sigmoid(smc)
        ld1 = log_decay.astype(BF16)
        rest = log_decay - ld1.astype(F32)
        ld2 = rest.astype(BF16)
        ld3 = (rest - ld2.astype(F32)).astype(BF16)

        q_l, k_l, v_l, gcol_l, grow_l, gend_l, bcol_l = [], [], [], [], [], [], []
        for k in range(DN_GROUP):
            cr = slice(k * c, (k + 1) * c)
            gc = _dot(tri_ones, ld1[cr]) + _dot(tri_ones, ld2[cr]) + _dot(tri_ones, ld3[cr])
            gc_t = gc.T
            for h in range(DN_HEADS):
                q_l.append(_l2_normalize(y[cr, h * DN_HEAD_DIM:(h + 1) * DN_HEAD_DIM]) * DN_HEAD_DIM ** -0.5)
                k_l.append(_l2_normalize(y[cr, DN_W + h * DN_HEAD_DIM:DN_W + (h + 1) * DN_HEAD_DIM]))
                v_l.append(y[cr, 2 * DN_W + h * DN_HEAD_DIM:2 * DN_W + (h + 1) * DN_HEAD_DIM])
                gcol_l.append(gc[:, SM_A + h:SM_A + h + 1])
                grow_l.append(gc_t[SM_A + h:SM_A + h + 1, :])
                gend_l.append(gc[c - 1:c, SM_A + h:SM_A + h + 1])
                bcol_l.append(beta[cr, SM_B + h:SM_B + h + 1])
        qs, ks, vs = jnp.stack(q_l), jnp.stack(k_l), jnp.stack(v_l)
        g_col, g_row, g_end, b_col = jnp.stack(gcol_l), jnp.stack(grow_l), jnp.stack(gend_l), jnp.stack(bcol_l)

        decay = jnp.exp(jnp.where(causal, g_col - g_row, NEG_INF))
        qb = qs.astype(BF16)
        kb = ks.astype(BF16)
        n_mat = jnp.where(strict, b_col * _bmm_nt(kb, kb) * decay, 0.0)
        inv = eye - jnp.where(split == 1, n_mat, 0.0)
        for lvl in range(1, DN_LEVELS):
            n_lvl = jnp.where(jnp.right_shift(split, lvl) == 1, n_mat, 0.0).astype(BF16)
            inv_b = inv.astype(BF16)
            inv = inv - _bmm(inv_b, _bmm(n_lvl, inv_b).astype(BF16))
        e_col = jnp.exp(g_col)
        rhs = jnp.concatenate([vs * b_col, ks * (b_col * e_col)], axis=2).astype(BF16)
        sol = _bmm(inv.astype(BF16), rhs)
        u = sol[:, :, :DN_HEAD_DIM]
        w = sol[:, :, DN_HEAD_DIM:].astype(BF16)
        qk = (_bmm_nt(qb, kb) * decay).astype(BF16)
        q_dec = (qs * e_col).astype(BF16)
        k_dec = ks * jnp.exp(g_end - g_col)
        k_dec_t = jnp.stack([k_dec[n].T for n in range(DN_GROUP * DN_HEADS)]).astype(BF16)
        g_chunk = jnp.exp(g_end)

        state = state_ref[...]
        for k in range(DN_GROUP):
            hs = slice(k * DN_HEADS, (k + 1) * DN_HEADS)
            state_b = state.astype(BF16)
            v_new = (u[hs] - _bmm(w[hs], state_b)).astype(BF16)
            o = _bmm(q_dec[hs], state_b) + _bmm(qk[hs], v_new)
            state = state * g_chunk[hs] + _bmm(k_dec_t[hs], v_new)
            out_rows = pl.ds(r0 + k * c, c)
            for h in range(DN_HEADS):
                hd = slice(h * DN_HEAD_DIM, (h + 1) * DN_HEAD_DIM)
                zh = z_ref[out_rows, hd].astype(F32)
                o_ref[out_rows, hd] = (_rms(o[h], gn_ref[...]) * _silu(zh)).astype(o_ref.dtype)
        state_ref[...] = state
        return carry

    lax.fori_loop(0, n_chunks // DN_GROUP, group, 0)


def _dn(dn, z, sm, conv_w, a_log, dt_bias, out_gain, batch, seq):
    assert DN_CHUNK == 1 << DN_LEVELS and seq % (DN_CHUNK * DN_GROUP) == 0
    pad = jnp.zeros((1, LANES), F32)
    return pl.pallas_call(
        _dn_kernel,
        grid=(batch,),
        in_specs=[pl.BlockSpec((seq, 3 * DN_W), lambda b: (b, 0)), pl.BlockSpec((seq, DN_W), lambda b: (b, 0)),
                  pl.BlockSpec((seq, LANES), lambda b: (b, 0)), _const_spec((DN_CONV, 3 * DN_W)),
                  _const_spec((1, LANES)), _const_spec((1, LANES)), _const_spec((1, DN_HEAD_DIM))],
        out_specs=pl.BlockSpec((seq, DN_W), lambda b: (b, 0)),
        out_shape=jax.ShapeDtypeStruct((batch * seq, DN_W), BF16),
        scratch_shapes=[pltpu.VMEM((DN_HEADS, DN_HEAD_DIM, DN_HEAD_DIM), F32)],
        compiler_params=_params("parallel"),
        name="dn",
    )(dn, z, sm, conv_w.astype(F32), pad.at[0, SM_A:SM_A + DN_HEADS].set(a_log.astype(F32)),
      pad.at[0, SM_A:SM_A + DN_HEADS].set(dt_bias.astype(F32)), out_gain.reshape(1, -1).astype(F32))


def _memkv_kernel(mem_ref, g_ref, w_ref, k_ref, v_ref):
    mn = _rms(mem_ref[0], g_ref[...]).astype(BF16)
    k_ref[0] = _dot(mn, w_ref[:, :X_W]).astype(k_ref.dtype)
    v_ref[0] = _dot(mn, w_ref[:, X_W:]).astype(v_ref.dtype)


def _memkv(mem, gain, w_kv):
    batch, n_mem, _ = mem.shape
    out = pl.BlockSpec((1, n_mem, X_W), lambda b: (b, 0, 0))
    shape = jax.ShapeDtypeStruct((batch, n_mem, X_W), BF16)
    return pl.pallas_call(
        _memkv_kernel,
        grid=(batch,),
        in_specs=[pl.BlockSpec((1, n_mem, D_MODEL), lambda b: (b, 0, 0)), _const_spec((1, D_MODEL)),
                  _const_spec((D_MODEL, 2 * X_W))],
        out_specs=[out, out],
        out_shape=[shape, shape],
        compiler_params=_params("parallel"),
        name="memkv",
    )(mem, gain.reshape(1, -1), w_kv.astype(BF16))


def _memory_attention(q_ref, k_ref, v_ref):
    cols = [slice(h * X_HEAD_DIM, (h + 1) * X_HEAD_DIM) for h in range(X_HEADS)]
    s = [_dot_nt(q_ref[:, hd], k_ref[0, :, hd]) for hd in cols]
    e = [jnp.exp(x - jnp.max(x, axis=-1, keepdims=True)) for x in s]
    p = [(x / jnp.sum(x, axis=-1, keepdims=True)).astype(BF16) for x in e]
    return jnp.concatenate([_dot(ph, v_ref[0, :, hd]).astype(BF16) for hd, ph in zip(cols, p)], axis=1)


def _merge_kernel(x_ref, on_ref, od_ref, mq_ref, mk_ref, mv_ref, gpre_ref, wbg_ref, wbn_ref, wbd_ref, wbm_ref, wo_ref,
                  gpost_ref, o_ref):
    x = x_ref[...]
    u = _rms(x, gpre_ref[...]).astype(BF16)
    branches = (on_ref[...], od_ref[...], _memory_attention(mq_ref, mk_ref, mv_ref))
    merged = jnp.zeros(x.shape, F32)
    for i, (branch, w_ref) in enumerate(zip(branches, (wbn_ref, wbd_ref, wbm_ref))):
        gates = jax.nn.sigmoid(_dot(u, wbg_ref[:, i * D_MODEL:(i + 1) * D_MODEL]))
        merged = merged + gates * _dot(branch, w_ref[...])
    y = _dot(merged.astype(BF16), wo_ref[...])
    o_ref[...] = x + _rms(y, gpost_ref[...])


def _merge(x, o_nsa, o_dn, mq, mk, mv, gpre, wbg, wbn, wbd, wbm, wo, gpost, seq):
    t = x.shape[0]
    assert seq % TM == 0
    n_mem = mk.shape[1]
    row = pl.BlockSpec((TM, D_MODEL), lambda i: (i, 0))
    br = pl.BlockSpec((TM, NSA_Q), lambda i: (i, 0))
    mem = pl.BlockSpec((1, n_mem, X_W), lambda i: (i // (seq // TM), 0, 0))
    return pl.pallas_call(
        _merge_kernel,
        grid=(t // TM,),
        in_specs=[row, br, br, br, mem, mem, _const_spec((1, D_MODEL)), _const_spec((D_MODEL, N_BRANCH * D_MODEL)),
                  _const_spec((NSA_Q, D_MODEL)), _const_spec((DN_W, D_MODEL)), _const_spec((X_W, D_MODEL)),
                  _const_spec((D_MODEL, D_MODEL)), _const_spec((1, D_MODEL))],
        out_specs=row,
        out_shape=jax.ShapeDtypeStruct((t, D_MODEL), F32),
        compiler_params=_params("parallel"),
        name="merge",
    )(x, o_nsa, o_dn, mq, mk, mv, gpre.reshape(1, -1), wbg.astype(BF16), wbn.astype(BF16), wbd.astype(BF16),
      wbm.astype(BF16), wo.astype(BF16), gpost.reshape(1, -1))


def kernel(x, mem, ffn1_pre_norm, ffn1_w_gate, ffn1_w_up, ffn1_w_down, ffn1_post_norm, mix_pre_norm, w_in, cmp_pos_k, cmp_pos_v, cmp_k_w1, cmp_k_w2, cmp_v_w1, cmp_v_w2, rel_bias, dn_conv_w, dn_a_log, dn_dt_bias, dn_out_norm, mem_norm, w_mem_kv, w_branch_nsa, w_branch_dn, w_branch_mem, w_branch_gate, w_out, mix_post_norm, ffn2_pre_norm, ffn2_w_gate, ffn2_w_up, ffn2_w_down, ffn2_post_norm):
    batch, seq, d = x.shape
    xt = x.reshape(batch * seq, d)
    cb, tabs, to_sel_t = _nsa_tables(rel_bias, seq)
    for l in range(ffn1_pre_norm.shape[0]):
        xt = _ffn(xt, ffn1_pre_norm[l], ffn1_w_gate[l], ffn1_w_up[l], ffn1_w_down[l], ffn1_post_norm[l])

        q, kc, vc, kvsw, dn, z, mq, sm = _proj(xt, mix_pre_norm[l], _arrange_w_in(w_in[l]))
        posk, w1k, w2k = _arrange_cmp(cmp_pos_k[l], cmp_k_w1[l], cmp_k_w2[l])
        posv, w1v, w2v = _arrange_cmp(cmp_pos_v[l], cmp_v_w1[l], cmp_v_w2[l])
        k_cmp, v_cmp = _cmp(kc, vc, posk, posv, w1k, w1v, w2k, w2v, batch, seq)
        o_nsa = _nsa(q, kvsw, k_cmp, v_cmp, sm, cb, tabs, to_sel_t, batch, seq)
        o_dn = _dn(dn, z, sm, dn_conv_w[l], dn_a_log[l], dn_dt_bias[l], dn_out_norm[l], batch, seq)
        mk, mv = _memkv(mem, mem_norm[l], w_mem_kv[l])
        xt = _merge(xt, o_nsa, o_dn, mq, mk, mv, mix_pre_norm[l], w_branch_gate[l], w_branch_nsa[l], w_branch_dn[l],
                    w_branch_mem[l], w_out[l], mix_post_norm[l], seq)

        xt = _ffn(xt, ffn2_pre_norm[l], ffn2_w_gate[l], ffn2_w_up[l], ffn2_w_down[l], ffn2_post_norm[l])
    return xt.reshape(batch, seq, d)
```
